```python
import jax, jax.numpy as jnp
from jax import lax
import numpy as np

D_MODEL = 1024
BATCH = 8
SEQ = 2048
DEPTH = 4
DEC_BATCH = 32
DEC_SEQ = 4
PAST_LEN = 8192
PAGE_SIZE = 128

N_HEADS = 16
HEAD_DIM = D_MODEL // N_HEADS
KV_HEADS = 4
HEADS_PER_KV = N_HEADS // KV_HEADS
KV_WIDTH = KV_HEADS * HEAD_DIM
ROT_DIM = HEAD_DIM // 4
ROPE_THETA = 500000.0
CMP_BLOCK = 32
SEL_BLOCK = 64
CMP_PER_SEL = SEL_BLOCK // CMP_BLOCK
TOP_N = 8
WINDOW = 512
Q_BLOCK = 64
N_BRANCH = 3
FORCE_BONUS = float(HEADS_PER_KV + 1)
NSA_IN_WIDTH = N_HEADS * HEAD_DIM + 6 * KV_WIDTH + N_BRANCH * N_HEADS
CONV_WIDTH = 3
CONV_DIM = D_MODEL
D_FF = (7 * D_MODEL) // 2
N_EXPERTS = 8
TOP_K = 2
N_ATTN_LAYERS = (DEPTH + 1) // 2
N_CONV_LAYERS = DEPTH // 2
N_DENSE_FFN = (DEPTH + 1) // 2
N_MOE_FFN = DEPTH // 2
EPS = 1e-6

kernel_name = 'nsa_shortconv_moe_hybrid_step'


def _rmsnorm(x, g):
    x32 = x.astype(jnp.float32)
    y = x32 * lax.rsqrt(jnp.mean(x32 * x32, axis=-1, keepdims=True) + EPS)
    return y.astype(x.dtype) * g


def _rope(x, pos):
    half = ROT_DIM // 2
    inv = 1.0 / (ROPE_THETA ** (jnp.arange(half, dtype=jnp.float32) * (2.0 / ROT_DIM)))
    ang = pos.astype(jnp.float32)[:, None] * inv[None, :]
    cos = jnp.cos(ang)[:, None, :].astype(x.dtype)
    sin = jnp.sin(ang)[:, None, :].astype(x.dtype)
    x1 = x[..., :half]
    x2 = x[..., half:ROT_DIM]
    return jnp.concatenate([x1 * cos - x2 * sin, x2 * cos + x1 * sin, x[..., ROT_DIM:]], axis=-1)


def _masked_softmax(s, mask):
    s = jnp.where(mask, s, -jnp.inf)
    m = jnp.max(s, axis=-1, keepdims=True)
    m = jnp.where(jnp.isfinite(m), m, 0.0)
    e = jnp.where(mask, jnp.exp(s - m), 0.0)
    return e / jnp.maximum(jnp.sum(e, axis=-1, keepdims=True), 1e-30)


def _compress(rows, w, pe):
    b, tk = rows.shape[0], rows.shape[1]
    nc = tk // CMP_BLOCK
    blocks = rows[:, :nc * CMP_BLOCK].reshape(b, nc, CMP_BLOCK, KV_HEADS, HEAD_DIM) + pe[None, None, :, None, :]
    return jnp.einsum('bnlgd,lde->bnge', blocks, w)


def _nsa_attend(q, gates, q_pos0, k_cmp, v_cmp, k_sel, v_sel, k_win, v_win, win_pos0, w_ck, pe_ck, w_cv, pe_cv):
    b, tq = q.shape[0], q.shape[1]
    tk = k_cmp.shape[1]
    scale = HEAD_DIM ** -0.5
    ck = _compress(k_cmp, w_ck, pe_ck)
    cv = _compress(v_cmp, w_cv, pe_cv)
    nc = ck.shape[1]
    c_end = jnp.arange(nc) * CMP_BLOCK + (CMP_BLOCK - 1)
    ck = _rope(ck, c_end)
    ns = -(-tk // SEL_BLOCK)
    pad = ns * SEL_BLOCK - tk
    kb = jnp.pad(k_sel, ((0, 0), (0, pad), (0, 0), (0, 0))).reshape(b, ns, SEL_BLOCK, KV_HEADS, HEAD_DIM).transpose(0, 3, 1, 2, 4)
    vb = jnp.pad(v_sel, ((0, 0), (0, pad), (0, 0), (0, 0))).reshape(b, ns, SEL_BLOCK, KV_HEADS, HEAD_DIM).transpose(0, 3, 1, 2, 4)
    n_sel = min(TOP_N, ns)
    bi = jnp.arange(b)[:, None, None]
    gi = jnp.arange(KV_HEADS)[None, :, None]
    kwp = jnp.pad(k_win, ((0, 0), (WINDOW, 0), (0, 0), (0, 0)))
    vwp = jnp.pad(v_win, ((0, 0), (WINDOW, 0), (0, 0), (0, 0)))
    qb_len = Q_BLOCK if (tq > Q_BLOCK and tq % Q_BLOCK == 0) else tq
    nb = tq // qb_len

    def block(args):
        qb, gb, i = args
        p0 = q_pos0 + i * qb_len
        t = p0 + jnp.arange(qb_len)
        qg = qb.reshape(b, qb_len, KV_HEADS, HEADS_PER_KV, HEAD_DIM)
        s = jnp.einsum('bqgrd,bngd->bqgrn', qg, ck, preferred_element_type=jnp.float32) * scale
        vis = (c_end[None, :] <= t[:, None])[None, :, None, None, :]
        p_c = _masked_softmax(s, vis)
        o_c = jnp.einsum('bqgrn,bngd->bqgrd', p_c.astype(cv.dtype), cv)
        imp = jnp.sum(p_c, axis=3)
        imp = jnp.pad(imp, ((0, 0), (0, 0), (0, 0), (0, ns * CMP_PER_SEL - nc)))
        imp = imp.reshape(b, qb_len, KV_HEADS, ns, CMP_PER_SEL).sum(-1)
        blk = jnp.arange(ns)
        cur = t // SEL_BLOCK
        forced = (blk[None, :] == 0) | (blk[None, :] == cur[:, None])
        valid = blk[None, :] <= cur[:, None]
        score = jnp.where(valid[None, :, None, :], imp + FORCE_BONUS * forced[None, :, None, :], -jnp.inf)
        _, idx = lax.top_k(score, n_sel)
        idx_t = idx.transpose(0, 2, 1, 3).reshape(b, KV_HEADS, qb_len * n_sel)
        ks = kb[bi, gi, idx_t].reshape(b, KV_HEADS, qb_len, n_sel * SEL_BLOCK, HEAD_DIM)
        vs = vb[bi, gi, idx_t].reshape(b, KV_HEADS, qb_len, n_sel * SEL_BLOCK, HEAD_DIM)
        kpos = (idx[..., None] * SEL_BLOCK + jnp.arange(SEL_BLOCK)).reshape(b, qb_len, KV_HEADS, n_sel * SEL_BLOCK)
        smask = (kpos <= t[None, :, None, None])[:, :, :, None, :]
        s = jnp.einsum('bqgrd,bgqkd->bqgrk', qg, ks, preferred_element_type=jnp.float32) * scale
        p_s = _masked_softmax(s, smask)
        o_s = jnp.einsum('bqgrk,bgqkd->bqgrd', p_s.astype(vs.dtype), vs)
        start = p0 - win_pos0
        kw = lax.dynamic_slice_in_dim(kwp, start, WINDOW + qb_len, axis=1)
        vw = lax.dynamic_slice_in_dim(vwp, start, WINDOW + qb_len, axis=1)
        wpos = p0 - WINDOW + jnp.arange(WINDOW + qb_len)
        wmask = (wpos[None, :] <= t[:, None]) & (wpos[None, :] >= t[:, None] - WINDOW) & (wpos[None, :] >= win_pos0)
        s = jnp.einsum('bqgrd,bkgd->bqgrk', qg, kw, preferred_element_type=jnp.float32) * scale
        p_w = _masked_softmax(s, wmask[None, :, None, None, :])
        o_w = jnp.einsum('bqgrk,bkgd->bqgrd', p_w.astype(vw.dtype), vw)
        g = gb.reshape(b, qb_len, KV_HEADS, HEADS_PER_KV, N_BRANCH)
        o = g[..., 0:1] * o_c + g[..., 1:2] * o_s + g[..., 2:3] * o_w
        return o.reshape(b, qb_len, N_HEADS, HEAD_DIM)

    if nb == 1:
        return block((q, gates, 0))
    qs = q.reshape(b, nb, qb_len, N_HEADS, HEAD_DIM).transpose(1, 0, 2, 3, 4)
    gs = gates.reshape(b, nb, qb_len, N_HEADS * N_BRANCH).transpose(1, 0, 2, 3)
    out = lax.map(block, (qs, gs, jnp.arange(nb)))
    return out.transpose(1, 0, 2, 3, 4).reshape(b, tq, N_HEADS, HEAD_DIM)


def _nsa_mixer(h, q_pos0, past_cmp, past_sel, win_buf, w_in, w_out, w_ck, pe_ck, w_cv, pe_cv):
    b, t, _ = h.shape
    hq = N_HEADS * HEAD_DIM
    proj = h @ w_in
    q = proj[..., :hq].reshape(b, t, N_HEADS, HEAD_DIM)
    kv = proj[..., hq:hq + 6 * KV_WIDTH].reshape(b, t, 6, KV_HEADS, HEAD_DIM)
    gates = jax.nn.sigmoid(proj[..., hq + 6 * KV_WIDTH:].astype(jnp.float32)).astype(h.dtype).reshape(b, t, N_HEADS, N_BRANCH)
    pos = q_pos0 + jnp.arange(t)
    q = _rope(q, pos)
    new_cmp = jnp.stack([kv[:, :, 0], kv[:, :, 1]], axis=2)
    new_sel = jnp.stack([_rope(kv[:, :, 2], pos), kv[:, :, 3]], axis=2)
    new_win = jnp.stack([_rope(kv[:, :, 4], pos), kv[:, :, 5]], axis=2)
    if past_cmp is None:
        cmp_rows, sel_rows, win_rows, win_pos0 = new_cmp, new_sel, new_win, 0
        win_state = new_win[:, -min(WINDOW, t):]
    else:
        wb = win_buf.shape[1]
        cmp_rows = jnp.concatenate([past_cmp, new_cmp], axis=1)
        sel_rows = jnp.concatenate([past_sel, new_sel], axis=1)
        win_rows = jnp.concatenate([win_buf, new_win], axis=1)
        win_pos0 = q_pos0 - wb
        win_state = win_rows[:, -wb:]
    o = _nsa_attend(q, gates, q_pos0, cmp_rows[:, :, 0], cmp_rows[:, :, 1], sel_rows[:, :, 0], sel_rows[:, :, 1],
                    win_rows[:, :, 0], win_rows[:, :, 1], win_pos0, w_ck, pe_ck, w_cv, pe_cv)
    return o.reshape(b, t, hq) @ w_out, new_cmp, new_sel, win_state


def _conv_mixer(h, state, w_in, w_conv, w_out):
    b, t, _ = h.shape
    bg, cg, v = jnp.split(h @ w_in, 3, axis=-1)
    u = cg * v
    prev = jnp.zeros((b, CONV_WIDTH - 1, CONV_DIM), u.dtype) if state is None else state
    ue = jnp.concatenate([prev, u], axis=1)
    conv = w_conv[0] * ue[:, 0:t]
    for j in range(1, CONV_WIDTH):
        conv = conv + w_conv[j] * ue[:, j:j + t]
    return (bg * conv) @ w_out, ue[:, -(CONV_WIDTH - 1):]


def _swiglu(h, w_gu, w_down):
    g, u = jnp.split(h @ w_gu, 2, axis=-1)
    return (jax.nn.silu(g) * u) @ w_down


def _moe(h, w_router, b_router, w_gu, w_down):
    logits = (h @ w_router).astype(jnp.float32) + b_router.astype(jnp.float32)
    top_v, top_i = lax.top_k(logits, TOP_K)
    top_w = jax.nn.softmax(top_v, axis=-1)
    combine = jnp.sum(jax.nn.one_hot(top_i, N_EXPERTS, dtype=jnp.float32) * top_w[..., None], axis=-2).astype(h.dtype)
    out = jnp.zeros_like(h)
    for e in range(N_EXPERTS):
        out = out + combine[..., e:e + 1] * _swiglu(h, w_gu[e], w_down[e])
    return out


def _trunk(x, c, q_pos0, caches, p):
    new_cmp, new_sel, new_win, new_conv = [], [], [], []
    cs = jax.nn.silu(c)
    for i in range(DEPTH):
        mod = (cs @ p['w_ada'][i] + p['b_ada'][i])[:, None, :]
        sh1, sc1, g1, sh2, sc2, g2 = jnp.split(mod, 6, axis=-1)
        h = _rmsnorm(x, p['g_norm_mix'][i]) * (1 + sc1) + sh1
        j = i // 2
        if i % 2 == 0:
            if caches is None:
                past_cmp = past_sel = win_buf = None
            else:
                pt = caches['page_table']
                nrows = pt.shape[1] * PAGE_SIZE
                past_cmp = caches['cmp'][j][pt].reshape(pt.shape[0], nrows, 2, KV_HEADS, HEAD_DIM)
                past_sel = caches['sel'][j][pt].reshape(pt.shape[0], nrows, 2, KV_HEADS, HEAD_DIM)
                win_buf = caches['win'][j]
            y, nc_, ns_, nw_ = _nsa_mixer(h, q_pos0, past_cmp, past_sel, win_buf, p['w_nsa_in'][j], p['w_nsa_out'][j],
                                          p['w_cmp_k'][j], p['pe_cmp_k'][j], p['w_cmp_v'][j], p['pe_cmp_v'][j])
            new_cmp.append(nc_)
            new_sel.append(ns_)
            new_win.append(nw_)
        else:
            state = None if caches is None else caches['conv'][j]
            y, st = _conv_mixer(h, state, p['w_conv_in'][j], p['w_conv'][j], p['w_conv_out'][j])
            new_conv.append(st)
        x = x + g1 * y
        h = _rmsnorm(x, p['g_norm_ffn'][i]) * (1 + sc2) + sh2
        if i % 2 == 0:
            f = _swiglu(h, p['w_ff_gu'][j], p['w_ff_down'][j])
        else:
            f = _moe(h, p['w_router'][j], p['b_router'][j], p['w_moe_gu'][j], p['w_moe_down'][j])
        x = x + g2 * f
    y = _rmsnorm(x, p['g_norm_final'])
    return y, jnp.stack(new_cmp), jnp.stack(new_sel), jnp.stack(new_win), jnp.stack(new_conv)


def setup_inputs(seed: int = 0) -> dict:
    key = jax.random.key(seed)
    ks = jax.random.split(key, 32)
    f32 = jnp.float32

    def nrm(k, shape, scale):
        return jax.random.normal(k, shape, f32) * scale

    n_pages = PAST_LEN // PAGE_SIZE
    n_used = DEC_BATCH * n_pages
    n_phys = n_used + max(1, n_used // 4)
    win_buf = min(WINDOW, PAST_LEN)
    page_table = jax.random.permutation(ks[0], n_phys)[:n_used].reshape(DEC_BATCH, n_pages).astype(jnp.int32)
    return {
        'x_prompt': nrm(ks[1], (BATCH, SEQ, D_MODEL), 1.0),
        'x_sample': nrm(ks[2], (DEC_BATCH, DEC_SEQ, D_MODEL), 1.0),
        'cache_cmp_kv': nrm(ks[3], (N_ATTN_LAYERS, n_phys, PAGE_SIZE, 2, KV_HEADS, HEAD_DIM), 1.0),
        'cache_sel_kv': nrm(ks[4], (N_ATTN_LAYERS, n_phys, PAGE_SIZE, 2, KV_HEADS, HEAD_DIM), 1.0),
        'cache_win_kv': nrm(ks[5], (N_ATTN_LAYERS, DEC_BATCH, win_buf, 2, KV_HEADS, HEAD_DIM), 1.0),
        'state_conv': nrm(ks[6], (N_CONV_LAYERS, DEC_BATCH, CONV_WIDTH - 1, CONV_DIM), 1.0),
        'page_table': page_table,
        'c_prompt': nrm(ks[7], (BATCH, D_MODEL), 1.0),
        'c_sample': nrm(ks[8], (DEC_BATCH, D_MODEL), 1.0),
        'w_ada': nrm(ks[9], (DEPTH, D_MODEL, 6 * D_MODEL), 0.5 * D_MODEL ** -0.5),
        'b_ada': nrm(ks[10], (DEPTH, 6 * D_MODEL), 0.01),
        'g_norm_mix': 1.0 + nrm(ks[11], (DEPTH, D_MODEL), 0.05),
        'g_norm_ffn': 1.0 + nrm(ks[12], (DEPTH, D_MODEL), 0.05),
        'g_norm_final': 1.0 + nrm(ks[13], (D_MODEL,), 0.05),
        'w_nsa_in': nrm(ks[14], (N_ATTN_LAYERS, D_MODEL, NSA_IN_WIDTH), D_MODEL ** -0.5),
        'w_nsa_out': nrm(ks[15], (N_ATTN_LAYERS, N_HEADS * HEAD_DIM, D_MODEL), (N_HEADS * HEAD_DIM) ** -0.5),
        'w_cmp_k': nrm(ks[16], (N_ATTN_LAYERS, CMP_BLOCK, HEAD_DIM, HEAD_DIM), (CMP_BLOCK * HEAD_DIM) ** -0.5),
        'pe_cmp_k': nrm(ks[17], (N_ATTN_LAYERS, CMP_BLOCK, HEAD_DIM), 0.1),
        'w_cmp_v': nrm(ks[18], (N_ATTN_LAYERS, CMP_BLOCK, HEAD_DIM, HEAD_DIM), (CMP_BLOCK * HEAD_DIM) ** -0.5),
        'pe_cmp_v': nrm(ks[19], (N_ATTN_LAYERS, CMP_BLOCK, HEAD_DIM), 0.1),
        'w_conv_in': nrm(ks[20], (N_CONV_LAYERS, D_MODEL, 3 * CONV_DIM), D_MODEL ** -0.5),
        'w_conv': nrm(ks[21], (N_CONV_LAYERS, CONV_WIDTH, CONV_DIM), CONV_WIDTH ** -0.5),
        'w_conv_out': nrm(ks[22], (N_CONV_LAYERS, CONV_DIM, D_MODEL), CONV_DIM ** -0.5),
        'w_ff_gu': nrm(ks[23], (N_DENSE_FFN, D_MODEL, 2 * D_FF), D_MODEL ** -0.5),
        'w_ff_down': nrm(ks[24], (N_DENSE_FFN, D_FF, D_MODEL), D_FF ** -0.5),
        'w_router': nrm(ks[25], (N_MOE_FFN, D_MODEL, N_EXPERTS), D_MODEL ** -0.5),
        'b_router': nrm(ks[26], (N_MOE_FFN, N_EXPERTS), 0.01),
        'w_moe_gu': nrm(ks[27], (N_MOE_FFN, N_EXPERTS, D_MODEL, 2 * D_FF), D_MODEL ** -0.5),
        'w_moe_down': nrm(ks[28], (N_MOE_FFN, N_EXPERTS, D_FF, D_MODEL), D_FF ** -0.5),
    }


def reference(x_prompt, x_sample, cache_cmp_kv, cache_sel_kv, cache_win_kv, state_conv, page_table, c_prompt, c_sample,
              w_ada, b_ada, g_norm_mix, g_norm_ffn, g_norm_final, w_nsa_in, w_nsa_out, w_cmp_k, pe_cmp_k, w_cmp_v, pe_cmp_v,
              w_conv_in, w_conv, w_conv_out, w_ff_gu, w_ff_down, w_router, b_router, w_moe_gu, w_moe_down):
    p = {'w_ada': w_ada, 'b_ada': b_ada, 'g_norm_mix': g_norm_mix, 'g_norm_ffn': g_norm_ffn, 'g_norm_final': g_norm_final,
         'w_nsa_in': w_nsa_in, 'w_nsa_out': w_nsa_out, 'w_cmp_k': w_cmp_k, 'pe_cmp_k': pe_cmp_k, 'w_cmp_v': w_cmp_v,
         'pe_cmp_v': pe_cmp_v, 'w_conv_in': w_conv_in, 'w_conv': w_conv, 'w_conv_out': w_conv_out, 'w_ff_gu': w_ff_gu,
         'w_ff_down': w_ff_down, 'w_router': w_router, 'b_router': b_router, 'w_moe_gu': w_moe_gu, 'w_moe_down': w_moe_down}
    y_prompt, cmp_p, sel_p, win_p, conv_p = _trunk(x_prompt, c_prompt, 0, None, p)
    past_len = page_table.shape[1] * PAGE_SIZE
    caches = {'cmp': cache_cmp_kv, 'sel': cache_sel_kv, 'win': cache_win_kv, 'conv': state_conv, 'page_table': page_table}
    y_sample, cmp_s, sel_s, win_s, conv_s = _trunk(x_sample, c_sample, past_len, caches, p)
    return (y_prompt, y_sample, cmp_p, cmp_s, sel_p, sel_s, win_p, win_s, conv_p, conv_s)
```

```python
import functools

import jax
import jax.numpy as jnp
from jax import lax
from jax.experimental import pallas as pl
from jax.experimental.pallas import tpu as pltpu

F32 = jnp.float32
BF16 = jnp.bfloat16

N_HEADS = 16
HEAD_DIM = 64
KV_HEADS = 4
HEADS_PER_KV = N_HEADS // KV_HEADS
KV_WIDTH = KV_HEADS * HEAD_DIM
ROT_DIM = HEAD_DIM // 4
ROPE_THETA = 500000.0
CMP_BLOCK = 32
SEL_BLOCK = 64
TOP_N = 8
WINDOW = 512
N_BRANCH = 3
FORCE_BONUS = float(HEADS_PER_KV + 1)
N_EXPERTS = 8
EPS = 1e-6
NEG_INF = float("-inf")

V7X_LANES = 128
V7X_VMEM_LIMIT_BYTES = 56 * 1024 * 1024


def _cparams(*sem):
    return pltpu.CompilerParams(dimension_semantics=sem, vmem_limit_bytes=V7X_VMEM_LIMIT_BYTES)


def _dot(a, b):
    return jnp.dot(a, b, preferred_element_type=F32)


def _dot_nt(a, b):
    return lax.dot_general(a, b, (((1,), (1,)), ((), ())), preferred_element_type=F32)


def _norm_mod(x, g, sc, sh):
    y = x * lax.rsqrt(jnp.mean(x * x, axis=-1, keepdims=True) + EPS)
    return (y * g) * (1.0 + sc) + sh


def _sigmoid(x):
    return 1.0 / (1.0 + jnp.exp(-x))


def _mod_spec(mod3, tiles_per_block):
    _, r, w = mod3.shape
    return pl.BlockSpec((1, r, w), lambda i, *_: (i // tiles_per_block, 0, 0))


def _ada_kernel(c_ref, w_ref, b_ref, o_ref):
    c = c_ref[...]
    cs = (c * _sigmoid(c)).astype(BF16)
    o_ref[0] = _dot(cs, w_ref[0].astype(BF16)) + b_ref[0]


def _ada(c, w_ada, b_ada):
    depth, d, n = w_ada.shape
    nb = c.shape[0]
    tn = 1536
    return pl.pallas_call(
        _ada_kernel,
        out_shape=jax.ShapeDtypeStruct((depth, nb, n), F32),
        grid=(depth, n // tn),
        in_specs=[
            pl.BlockSpec((nb, d), lambda i, j: (0, 0)),
            pl.BlockSpec((1, d, tn), lambda i, j: (i, 0, j)),
            pl.BlockSpec((1, 1, tn), lambda i, j: (i, 0, j)),
        ],
        out_specs=pl.BlockSpec((1, nb, tn), lambda i, j: (i, 0, j)),
        compiler_params=_cparams("arbitrary", "arbitrary"),
        name="ada",
    )(c, w_ada, b_ada.reshape(depth, 1, n))


def _rope_tables(pos, width):
    half = ROT_DIM // 2
    inv = 1.0 / (ROPE_THETA ** (jnp.arange(half, dtype=F32) * (2.0 / ROT_DIM)))
    ang = pos.astype(F32)[:, None] * inv[None, :]
    cos, sin = jnp.cos(ang), jnp.sin(ang)
    n = pos.shape[0]
    z8 = jnp.zeros((n, half), F32)
    rest0 = jnp.zeros((n, HEAD_DIM - ROT_DIM), F32)
    c = jnp.concatenate([cos, cos, rest0 + 1.0], axis=1)
    a = jnp.concatenate([-sin, z8, rest0], axis=1)
    b = jnp.concatenate([z8, sin, rest0], axis=1)
    rep = width // HEAD_DIM
    return tuple(jnp.tile(t, (1, rep)) for t in (c, a, b))


def _rope(v, c, a, b):
    w = v.shape[-1]
    half = ROT_DIM // 2
    return v * c + pltpu.roll(v, w - half, 1) * a + pltpu.roll(v, half, 1) * b


def _widen(t, width):
    return jnp.concatenate([t] * (width // t.shape[-1]), axis=1)


def _nsa_proj_kernel(x_ref, mod_ref, g_ref, w_ref, wg_ref, rc_ref, ra_ref, rb_ref,
                     q_ref, cmp_ref, sel_ref, win_ref, gate_ref):
    d = x_ref.shape[1]
    mod = mod_ref[0]
    h = _norm_mod(x_ref[...], g_ref[...], mod[:, d:2 * d], mod[:, 0:d]).astype(BF16)
    kvw = 2 * KV_WIDTH
    c1, a1, b1 = rc_ref[...], ra_ref[...], rb_ref[...]
    c2, a2, b2 = (_widen(t, kvw) for t in (c1, a1, b1))
    scale = HEAD_DIM ** -0.5
    hq = N_HEADS * HEAD_DIM
    for j in range(hq // kvw):
        p = _dot(h, w_ref[:, j * kvw:(j + 1) * kvw])
        q_ref[:, j * kvw:(j + 1) * kvw] = (_rope(p, c2, a2, b2) * scale).astype(BF16)
    cmp_ref[...] = _dot(h, w_ref[:, hq:hq + kvw])
    one = jnp.ones_like(c1)
    zero = jnp.zeros_like(c1)
    ck = jnp.concatenate([_widen(c1, KV_WIDTH), _widen(one, KV_WIDTH)], axis=1)
    ak = jnp.concatenate([_widen(a1, KV_WIDTH), _widen(zero, KV_WIDTH)], axis=1)
    bk = jnp.concatenate([_widen(b1, KV_WIDTH), _widen(zero, KV_WIDTH)], axis=1)
    sel_ref[...] = _rope(_dot(h, w_ref[:, hq + kvw:hq + 2 * kvw]), ck, ak, bk)
    win_ref[...] = _rope(_dot(h, w_ref[:, hq + 2 * kvw:hq + 3 * kvw]), ck, ak, bk)
    gate_ref[...] = _sigmoid(_dot(h, wg_ref[...]))


def _nsa_proj(x, mod3, tpb, g, w_main, w_gate, rope_t, rope_rows_per_seq, tm):
    t, d = x.shape
    nw = w_main.shape[1]
    kvw = 2 * KV_WIDTH
    nrt = rope_rows_per_seq // tm
    rspec = pl.BlockSpec((tm, V7X_LANES), lambda i: (i % nrt, 0))
    return pl.pallas_call(
        _nsa_proj_kernel,
        out_shape=(
            jax.ShapeDtypeStruct((t, N_HEADS * HEAD_DIM), BF16),
            jax.ShapeDtypeStruct((t, kvw), F32),
            jax.ShapeDtypeStruct((t, kvw), F32),
            jax.ShapeDtypeStruct((t, kvw), F32),
            jax.ShapeDtypeStruct((t, V7X_LANES), F32),
        ),
        grid=(t // tm,),
        in_specs=[
            pl.BlockSpec((tm, d), lambda i: (i, 0)),
            _mod_spec(mod3, tpb),
            pl.BlockSpec((1, d), lambda i: (0, 0)),
            pl.BlockSpec((d, nw), lambda i: (0, 0)),
            pl.BlockSpec((d, V7X_LANES), lambda i: (0, 0)),
            rspec, rspec, rspec,
        ],
        out_specs=(
            pl.BlockSpec((tm, N_HEADS * HEAD_DIM), lambda i: (i, 0)),
            pl.BlockSpec((tm, kvw), lambda i: (i, 0)),
            pl.BlockSpec((tm, kvw), lambda i: (i, 0)),
            pl.BlockSpec((tm, kvw), lambda i: (i, 0)),
            pl.BlockSpec((tm, V7X_LANES), lambda i: (i, 0)),
        ),
        compiler_params=_cparams("arbitrary"),
        name="nsa_proj",
    )(x, mod3, g, w_main, w_gate, *rope_t)


def _compress_rows(load_rows, bdk_ref, bdv_ref, pek_ref, pev_ref, nc):
    half = nc // 2
    acck = jnp.zeros((nc, KV_WIDTH), F32)
    accv = jnp.zeros((nc, KV_WIDTH), F32)
    for l in range(CMP_BLOCK):
        even = load_rows(l, half)
        odd = load_rows(CMP_BLOCK + l, half)
        xb = jnp.concatenate([even, odd], axis=0)
        xk = (xb[:, 0:KV_WIDTH] + pek_ref[l:l + 1, :]).astype(BF16)
        xv = (xb[:, KV_WIDTH:2 * KV_WIDTH] + pev_ref[l:l + 1, :]).astype(BF16)
        acck = acck + _dot(xk, bdk_ref[l])
        accv = accv + _dot(xv, bdv_ref[l])
    return acck, accv


def _compress_kernel(r0_ref, r1_ref, r2_ref, r3_ref, bdk_ref, bdv_ref, pek_ref, pev_ref, rc_ref, ra_ref, rb_ref,
                     ck_ref, cv_ref):
    nc = ck_ref.shape[1]

    def load_rows(start, n):
        idx = pl.ds(start, n, stride=2 * CMP_BLOCK)
        return jnp.concatenate([r[0, idx, :] for r in (r0_ref, r1_ref, r2_ref, r3_ref)], axis=1)

    acck, accv = _compress_rows(load_rows, bdk_ref, bdv_ref, pek_ref, pev_ref, nc)
    ck_ref[0] = _rope(acck, rc_ref[...], ra_ref[...], rb_ref[...]).astype(BF16)
    cv_ref[0] = accv.astype(BF16)


def _cmp_order_end_pos(nc):
    n = jnp.arange(nc)
    half = nc // 2
    blk = jnp.where(n < half, 2 * n, 2 * (n - half) + 1)
    return blk * CMP_BLOCK + (CMP_BLOCK - 1)


def _compress(rows3, cw):
    b, tk, w = rows3.shape
    nc = tk // CMP_BLOCK
    assert nc % 2 == 0 and nc * CMP_BLOCK == tk
    rt = _rope_tables(_cmp_order_end_pos(nc), KV_WIDTH)
    full = lambda shp: pl.BlockSpec(shp, lambda i: (0,) * len(shp))
    return pl.pallas_call(
        _compress_kernel,
        out_shape=(jax.ShapeDtypeStruct((b, nc, KV_WIDTH), BF16),) * 2,
        grid=(b,),
        in_specs=[
            *[pl.BlockSpec((1, tk, V7X_LANES), functools.partial(lambda i, c: (i, 0, c), c=c))
              for c in range(w // V7X_LANES)],
            full(cw["bdk"].shape), full(cw["bdv"].shape), full(cw["pek"].shape), full(cw["pev"].shape),
            full(rt[0].shape), full(rt[1].shape), full(rt[2].shape),
        ],
        out_specs=(pl.BlockSpec((1, nc, KV_WIDTH), lambda i: (i, 0, 0)),) * 2,
        compiler_params=_cparams("arbitrary"),
        name="compress",
    )(rows3, rows3, rows3, rows3, cw["bdk"], cw["bdv"], cw["pek"], cw["pev"], *rt)


def _compress_weights(w_ck, pe_ck, w_cv, pe_cv):
    eye = jnp.eye(KV_HEADS, dtype=F32)

    def bd(w):
        return jnp.einsum("gh,lde->lgdhe", eye, w).reshape(CMP_BLOCK, KV_WIDTH, KV_WIDTH).astype(BF16)

    return {"bdk": bd(w_ck), "bdv": bd(w_cv),
            "pek": jnp.tile(pe_ck, (1, KV_HEADS)), "pev": jnp.tile(pe_cv, (1, KV_HEADS))}


def _masked_softmax(s, mask):
    s = jnp.where(mask, s, NEG_INF)
    m = jnp.max(s, axis=-1, keepdims=True)
    m = jnp.where(m == NEG_INF, 0.0, m)
    e = jnp.exp(s - m)
    return e / jnp.maximum(jnp.sum(e, axis=-1, keepdims=True), 1e-30)


def _select_blocks(imp, t_col, ns):
    blk = lax.broadcasted_iota(jnp.int32, (1, ns), 1)
    cur = t_col // SEL_BLOCK
    forced = jnp.where(blk == 0, 1.0, jnp.where(blk == cur, 1.0, 0.0))
    score = jnp.where(blk <= cur, imp + FORCE_BONUS * forced, NEG_INF)
    rank = jnp.zeros(score.shape, F32)
    for i in range(ns):
        ci = score[:, i:i + 1]
        later = jnp.where(blk > i, 1.0, 0.0)
        rank = rank + jnp.where(ci > score, 1.0, jnp.where(ci == score, later, 0.0))
    return jnp.where(rank < float(min(TOP_N, ns)), 1.0, 0.0)


def _stack_heads(x, g):
    base = g * HEADS_PER_KV * HEAD_DIM
    return jnp.concatenate([x[:, base + r * HEAD_DIM: base + (r + 1) * HEAD_DIM] for r in range(HEADS_PER_KV)],
                           axis=0)


def _gate_cols(gates, g, branch):
    cols = [(g * HEADS_PER_KV + r) * N_BRANCH + branch for r in range(HEADS_PER_KV)]
    return jnp.concatenate([gates[:, c:c + 1] for c in cols], axis=0)


def _attn_prompt_kernel(q_ref, gate_ref, ck_ref, cv_ref, sel_ref, win_ref, o_ref,
                        ks_scr, vs_scr, kw_scr, vw_scr, *, tk):
    tq = q_ref.shape[0]
    s_len = sel_ref.shape[1]
    nc = ck_ref.shape[1]
    half = nc // 2
    ns = s_len // SEL_BLOCK
    rows = HEADS_PER_KV * tq
    qi = pl.program_id(1)

    @pl.when(qi == 0)
    def _():
        ks_scr[...] = sel_ref[0, :, 0:KV_WIDTH].astype(BF16)
        vs_scr[...] = sel_ref[0, :, KV_WIDTH:2 * KV_WIDTH].astype(BF16)
        kw_scr[...] = win_ref[0, :, 0:KV_WIDTH].astype(BF16)
        vw_scr[...] = win_ref[0, :, KV_WIDTH:2 * KV_WIDTH].astype(BF16)

    q0 = qi * tq
    t_tile = q0 + lax.broadcasted_iota(jnp.int32, (tq, 1), 0)
    t_col = jnp.concatenate([t_tile] * HEADS_PER_KV, axis=0)
    q_all = q_ref[...]
    gates = gate_ref[...]

    ccol = lax.broadcasted_iota(jnp.int32, (1, nc), 1)
    c_end = jnp.where(ccol < half, 2 * CMP_BLOCK * ccol + (CMP_BLOCK - 1),
                      2 * CMP_BLOCK * (ccol - half) + (2 * CMP_BLOCK - 1))
    wlen = kw_scr.shape[0] if kw_scr.shape[0] < WINDOW + tq else WINDOW + tq
    w0 = jnp.clip(q0 + tq - wlen, 0, s_len - wlen)
    w0 = pl.multiple_of(w0, tq)
    n_chunks = (q0 + tq + tk - 1) // tk

    for g in range(KV_HEADS):
        lanes = slice(g * HEAD_DIM, (g + 1) * HEAD_DIM)
        q4 = _stack_heads(q_all, g)
        p_c = _masked_softmax(_dot_nt(q4, ck_ref[0, :, lanes]), c_end <= t_col)
        o_c = _dot(p_c.astype(BF16), cv_ref[0, :, lanes])
        psum = p_c[0:tq]
        for r in range(1, HEADS_PER_KV):
            psum = psum + p_c[r * tq:(r + 1) * tq]
        imp = psum[:, 0:half] + psum[:, half:nc]
        selm = _select_blocks(imp, t_tile, ns)
        selm4 = jnp.concatenate([selm] * HEADS_PER_KV, axis=0).astype(BF16)

        def chunk(c, carry):
            m, l, acc = carry
            k0 = pl.multiple_of(c * tk, tk)
            kpos = k0 + lax.broadcasted_iota(jnp.int32, (1, tk), 1)
            brow = lax.broadcasted_iota(jnp.int32, (ns, 1), 0)
            expand = jnp.where(kpos // SEL_BLOCK == brow, 1.0, 0.0).astype(BF16)
            chosen = _dot(selm4, expand)
            s = _dot_nt(q4, ks_scr[pl.ds(k0, tk), lanes])
            s = jnp.where(kpos <= t_col, jnp.where(chosen > 0.5, s, NEG_INF), NEG_INF)
            m_new = jnp.maximum(m, jnp.max(s, axis=-1, keepdims=True))
            m_safe = jnp.where(m_new == NEG_INF, 0.0, m_new)
            alpha = jnp.exp(m - m_safe)
            p = jnp.exp(s - m_safe)
            l = alpha * l + jnp.sum(p, axis=-1, keepdims=True)
            acc = alpha * acc + _dot(p.astype(BF16), vs_scr[pl.ds(k0, tk), lanes])
            return m_new, l, acc

        init = (jnp.full((rows, 1), NEG_INF, F32), jnp.zeros((rows, 1), F32), jnp.zeros((rows, HEAD_DIM), F32))
        _, l_s, acc_s = lax.fori_loop(0, n_chunks, chunk, init)
        o_s = acc_s / jnp.maximum(l_s, 1e-30)

        wpos = w0 + lax.broadcasted_iota(jnp.int32, (1, wlen), 1)
        s = _dot_nt(q4, kw_scr[pl.ds(w0, wlen), lanes])
        wmask = jnp.where(wpos <= t_col, jnp.where(wpos >= t_col - WINDOW, 1.0, 0.0), 0.0)
        p_w = _masked_softmax(s, wmask > 0.5)
        o_w = _dot(p_w.astype(BF16), vw_scr[pl.ds(w0, wlen), lanes])

        o = _gate_cols(gates, g, 0) * o_c + _gate_cols(gates, g, 1) * o_s + _gate_cols(gates, g, 2) * o_w
        for r in range(HEADS_PER_KV):
            col = (g * HEADS_PER_KV + r) * HEAD_DIM
            o_ref[:, col:col + HEAD_DIM] = o[r * tq:(r + 1) * tq].astype(BF16)


def _attn_prompt(q, gates, ck, cv, sel3, win3, tq):
    b, s_len, kvw = sel3.shape
    nc = ck.shape[1]
    hq = N_HEADS * HEAD_DIM
    nq = s_len // tq
    tk = min(512, s_len)
    assert s_len % tk == 0 and s_len % tq == 0 and tk % tq == 0
    return pl.pallas_call(
        functools.partial(_attn_prompt_kernel, tk=tk),
        out_shape=jax.ShapeDtypeStruct((b * s_len, hq), BF16),
        grid=(b, nq),
        in_specs=[
            pl.BlockSpec((tq, hq), lambda i, j: (i * nq + j, 0)),
            pl.BlockSpec((tq, V7X_LANES), lambda i, j: (i * nq + j, 0)),
            pl.BlockSpec((1, nc, KV_WIDTH), lambda i, j: (i, 0, 0)),
            pl.BlockSpec((1, nc, KV_WIDTH), lambda i, j: (i, 0, 0)),
            pl.BlockSpec((1, s_len, kvw), lambda i, j: (i, 0, 0)),
            pl.BlockSpec((1, s_len, kvw), lambda i, j: (i, 0, 0)),
        ],
        out_specs=pl.BlockSpec((tq, hq), lambda i, j: (i * nq + j, 0)),
        scratch_shapes=[pltpu.VMEM((s_len, KV_WIDTH), BF16)] * 4,
        compiler_params=_cparams("arbitrary", "arbitrary"),
        name="attn_prompt",
    )(q, gates, ck, cv, sel3, win3)


PAGES_PER_STEP = 8


def _block_diag_q(q):
    parts = []
    for g in range(KV_HEADS):
        q4 = _stack_heads(q, g)
        z = jnp.zeros_like(q4)
        parts.append(jnp.concatenate([q4 if gg == g else z for gg in range(KV_HEADS)], axis=1))
    return jnp.concatenate(parts, axis=0)


def _attn_decode_kernel(pt_ref, q_ref, gate_ref, nsel_ref, nwin_ref, wbuf_ref, bdk_ref, bdv_ref, pek_ref, pev_ref,
                        rc_ref, ra_ref, rb_ref, *rest, page, n_pages):
    pps = PAGES_PER_STEP
    cpages = rest[:pps]
    spages = rest[pps:2 * pps]
    o_ref = rest[2 * pps]
    cmp_scr, s_scr, vs_scr, qbd_scr = rest[2 * pps + 1:]
    del pt_ref
    j = pl.program_id(1)
    tq = q_ref.shape[1]
    rows = N_HEADS * tq
    past_len = n_pages * page
    n_lane_chunks = 2 * KV_WIDTH // V7X_LANES

    @pl.when(j == 0)
    def _():
        qbd_scr[...] = _block_diag_q(q_ref[0].astype(F32)).astype(BF16)

    qbd = qbd_scr[...]
    for p in range(pps):
        pg = j * pps + p
        r0 = pl.multiple_of(pg * page, page)
        for c in range(n_lane_chunks):
            cmp_scr[c, pl.ds(r0, page), :] = cpages[p][0, 0, :, c * V7X_LANES:(c + 1) * V7X_LANES]
        s_scr[pg] = _dot_nt(qbd, spages[p][0, 0, :, 0:KV_WIDTH].astype(BF16))
        vs_scr[pl.ds(r0, page), :] = spages[p][0, 0, :, KV_WIDTH:2 * KV_WIDTH].astype(BF16)

    @pl.when(j == pl.num_programs(1) - 1)
    def _():
        tok = lax.broadcasted_iota(jnp.int32, (tq, 1), 0)
        t_rows = past_len + jnp.concatenate([tok] * N_HEADS, axis=0)
        nsel = jnp.concatenate([nsel_ref[0], jnp.zeros((page - tq, 2 * KV_WIDTH), F32)], axis=0)
        s_scr[n_pages] = _dot_nt(qbd, nsel[:, 0:KV_WIDTH].astype(BF16))
        vs_scr[pl.ds(past_len, page), :] = nsel[:, KV_WIDTH:].astype(BF16)

        nc = past_len // CMP_BLOCK
        half = nc // 2

        def load_rows(start, n):
            idx = pl.ds(start, n, stride=2 * CMP_BLOCK)
            return jnp.concatenate([cmp_scr[c, idx, :] for c in range(n_lane_chunks)], axis=1)

        acck, accv = _compress_rows(load_rows, bdk_ref, bdv_ref, pek_ref, pev_ref, nc)
        ck = _rope(acck, rc_ref[...], ra_ref[...], rb_ref[...]).astype(BF16)
        ccol = lax.broadcasted_iota(jnp.int32, (1, nc), 1)
        c_end = jnp.where(ccol < half, 2 * CMP_BLOCK * ccol + (CMP_BLOCK - 1),
                          2 * CMP_BLOCK * (ccol - half) + (2 * CMP_BLOCK - 1))
        p_c = _masked_softmax(_dot_nt(qbd, ck), c_end <= t_rows)
        o_c = _dot(p_c.astype(BF16), accv.astype(BF16))

        ns = n_pages * page // SEL_BLOCK + 1
        grp = HEADS_PER_KV * tq
        psum = []
        for g in range(KV_HEADS):
            pg_ = p_c[g * grp:g * grp + tq]
            for r in range(1, HEADS_PER_KV):
                pg_ = pg_ + p_c[g * grp + r * tq:g * grp + (r + 1) * tq]
            psum.append(pg_)
        psum = jnp.concatenate(psum, axis=0)
        imp = jnp.concatenate([psum[:, 0:half] + psum[:, half:nc], jnp.zeros((KV_HEADS * tq, 1), F32)], axis=1)
        selm = _select_blocks(imp, past_len + jnp.concatenate([tok] * KV_HEADS, axis=0), ns)
        selm_rows = jnp.concatenate(
            [selm[g * tq:(g + 1) * tq] for g in range(KV_HEADS) for _ in range(HEADS_PER_KV)], axis=0)
        nkeys = (n_pages + 1) * page
        kpos = lax.broadcasted_iota(jnp.int32, (1, nkeys), 1)
        brow = lax.broadcasted_iota(jnp.int32, (ns, 1), 0)
        expand = jnp.where(kpos // SEL_BLOCK == brow, 1.0, 0.0).astype(BF16)
        chosen = _dot(selm_rows.astype(BF16), expand)
        s_all = jnp.concatenate([s_scr[i] for i in range(n_pages + 1)], axis=1)
        smask = jnp.where(kpos <= t_rows, jnp.where(chosen > 0.5, 1.0, 0.0), 0.0)
        p_s = _masked_softmax(s_all, smask > 0.5)
        o_s = _dot(p_s.astype(BF16), vs_scr[...])

        wb = wbuf_ref.shape[2]
        wrows = jnp.concatenate([wbuf_ref[0, 0], nwin_ref[0]], axis=0)
        widx = lax.broadcasted_iota(jnp.int32, (1, wrows.shape[0]), 1)
        wpos = past_len - wb + widx
        wmask = jnp.where(wpos <= t_rows, jnp.where(wpos >= t_rows - WINDOW, 1.0, 0.0), 0.0)
        p_w = _masked_softmax(_dot_nt(qbd, wrows[:, 0:KV_WIDTH].astype(BF16)), wmask > 0.5)
        o_w = _dot(p_w.astype(BF16), wrows[:, KV_WIDTH:].astype(BF16))

        gates = gate_ref[0]
        gcol = lambda br: jnp.concatenate([_gate_cols(gates, g, br) for g in range(KV_HEADS)], axis=0)
        o = gcol(0) * o_c + gcol(1) * o_s + gcol(2) * o_w
        for g in range(KV_HEADS):
            for r in range(HEADS_PER_KV):
                col = (g * HEADS_PER_KV + r) * HEAD_DIM
                o_ref[0, :, col:col + HEAD_DIM] = o[g * grp + r * tq:g * grp + (r + 1) * tq,
                                                    g * HEAD_DIM:(g + 1) * HEAD_DIM]


def _attn_decode(page_table, q3, gates3, nsel3, nwin3, cache_cmp, cache_sel, cache_win, layer, cw):
    b, tq, hq = q3.shape
    n_pages = page_table.shape[1]
    page = cache_cmp.shape[2]
    kvw = 2 * KV_WIDTH
    pps = PAGES_PER_STEP
    assert n_pages % pps == 0 and (n_pages * page) % (2 * CMP_BLOCK) == 0 and tq < CMP_BLOCK and tq % 8 == 0
    nc = n_pages * page // CMP_BLOCK
    rt = _rope_tables(_cmp_order_end_pos(nc), KV_WIDTH)
    wb = cache_win.shape[2]
    full = lambda shp: pl.BlockSpec(shp, lambda *_: (0,) * len(shp))
    per_seq = lambda shp: pl.BlockSpec((1,) + shp, lambda i, j, pt: (i,) + (0,) * len(shp))
    page_spec = lambda p: pl.BlockSpec((1, 1, page, kvw), lambda i, j, pt: (layer, pt[i, j * pps + p], 0, 0))
    rows = N_HEADS * tq
    grid_spec = pltpu.PrefetchScalarGridSpec(
        num_scalar_prefetch=1,
        grid=(b, n_pages // pps),
        in_specs=[
            per_seq((tq, hq)), per_seq((tq, V7X_LANES)), per_seq((tq, kvw)), per_seq((tq, kvw)),
            pl.BlockSpec((1, 1, wb, kvw), lambda i, j, pt: (layer, i, 0, 0)),
            full(cw["bdk"].shape), full(cw["bdv"].shape), full(cw["pek"].shape), full(cw["pev"].shape),
            full(rt[0].shape), full(rt[1].shape), full(rt[2].shape),
            *[page_spec(p) for p in range(pps)],
            *[page_spec(p) for p in range(pps)],
        ],
        out_specs=per_seq((tq, hq)),
        scratch_shapes=[
            pltpu.VMEM((kvw // V7X_LANES, n_pages * page, V7X_LANES), F32),
            pltpu.VMEM((n_pages + 1, rows, page), F32),
            pltpu.VMEM(((n_pages + 1) * page, KV_WIDTH), BF16),
            pltpu.VMEM((rows, KV_WIDTH), BF16),
        ],
    )
    return pl.pallas_call(
        functools.partial(_attn_decode_kernel, page=page, n_pages=n_pages),
        out_shape=jax.ShapeDtypeStruct((b, tq, hq), F32),
        grid_spec=grid_spec,
        compiler_params=_cparams("arbitrary", "arbitrary"),
        name="attn_decode",
    )(page_table, q3, gates3, nsel3, nwin3, cache_win, cw["bdk"], cw["bdv"], cw["pek"], cw["pev"], *rt,
      *([cache_cmp] * pps), *([cache_sel] * pps))


def _out_proj_kernel(o_ref, w_ref, x_ref, mod_ref, xo_ref):
    d = x_ref.shape[1]
    xo_ref[...] = x_ref[...] + mod_ref[0][:, 2 * d:3 * d] * _dot(o_ref[...], w_ref[...])


def _out_proj(o, w_bf16, x, mod3, tpb, tm):
    t, d = x.shape
    k = o.shape[1]
    return pl.pallas_call(
        _out_proj_kernel,
        out_shape=jax.ShapeDtypeStruct((t, d), F32),
        grid=(t // tm,),
        in_specs=[
            pl.BlockSpec((tm, k), lambda i: (i, 0)),
            pl.BlockSpec((k, d), lambda i: (0, 0)),
            pl.BlockSpec((tm, d), lambda i: (i, 0)),
            _mod_spec(mod3, tpb),
        ],
        out_specs=pl.BlockSpec((tm, d), lambda i: (i, 0)),
        compiler_params=_cparams("arbitrary"),
        name="out_proj",
    )(o, w_bf16, x, mod3)


def _ffn_kernel(x_ref, mod_ref, g_ref, wg_ref, wu_ref, wd_ref, o_ref, h_scr):
    d = x_ref.shape[1]
    f = pl.program_id(1)

    @pl.when(f == 0)
    def _():
        mod = mod_ref[0]
        h_scr[...] = _norm_mod(x_ref[...], g_ref[...], mod[:, 4 * d:5 * d], mod[:, 3 * d:4 * d]).astype(BF16)
        o_ref[...] = jnp.zeros(o_ref.shape, F32)

    h = h_scr[...]
    a = _dot(h, wg_ref[...].astype(BF16))
    u = _dot(h, wu_ref[...].astype(BF16))
    act = (a * _sigmoid(a) * u).astype(BF16)
    o_ref[...] += _dot(act, wd_ref[...].astype(BF16))

    @pl.when(f == pl.num_programs(1) - 1)
    def _():
        o_ref[...] = x_ref[...] + mod_ref[0][:, 5 * d:6 * d] * o_ref[...]


def _ffn(x, mod3, tpb, g, w_gu, w_down, tm, tf):
    t, d = x.shape
    ff = w_down.shape[0]
    nf = ff // tf
    return pl.pallas_call(
        _ffn_kernel,
        out_shape=jax.ShapeDtypeStruct((t, d), F32),
        grid=(t // tm, nf),
        in_specs=[
            pl.BlockSpec((tm, d), lambda i, f: (i, 0)),
            _mod_spec(mod3, tpb),
            pl.BlockSpec((1, d), lambda i, f: (0, 0)),
            pl.BlockSpec((d, tf), lambda i, f: (0, f)),
            pl.BlockSpec((d, tf), lambda i, f: (0, nf + f)),
            pl.BlockSpec((tf, d), lambda i, f: (f, 0)),
        ],
        out_specs=pl.BlockSpec((tm, d), lambda i, f: (i, 0)),
        scratch_shapes=[pltpu.VMEM((tm, d), BF16)],
        compiler_params=_cparams("arbitrary", "arbitrary"),
        name="ffn",
    )(x, mod3, g, w_gu, w_gu, w_down)


def _conv_in(x_ref, mod_ref, g_ref, win_ref):
    d = x_ref.shape[1]
    mod = mod_ref[0]
    h = _norm_mod(x_ref[...], g_ref[...], mod[:, d:2 * d], mod[:, 0:d]).astype(BF16)
    bg = _dot(h, win_ref[:, 0:d])
    u = _dot(h, win_ref[:, d:2 * d]) * _dot(h, win_ref[:, 2 * d:3 * d])
    return bg, u


def _conv_out(bg, conv, x_ref, mod_ref, wout_ref, xo_ref):
    d = x_ref.shape[1]
    y = _dot((bg * conv).astype(BF16), wout_ref[...])
    xo_ref[...] = x_ref[...] + mod_ref[0][:, 2 * d:3 * d] * y


def _conv_seq_kernel(x_ref, mod_ref, g_ref, win_ref, wc_ref, wout_ref, st_ref, xo_ref, nst_ref, carry_scr):
    tm = x_ref.shape[0]
    bg, u = _conv_in(x_ref, mod_ref, g_ref, win_ref)

    @pl.when(pl.program_id(1) == 0)
    def _():
        carry_scr[6:8, :] = st_ref[0]

    ue = jnp.concatenate([carry_scr[...], u], axis=0)
    conv = wc_ref[0:1, :] * ue[6:6 + tm] + wc_ref[1:2, :] * ue[7:7 + tm] + wc_ref[2:3, :] * u
    carry_scr[...] = u[tm - 8:tm]
    nst_ref[0] = u[tm - 2:tm]
    _conv_out(bg, conv, x_ref, mod_ref, wout_ref, xo_ref)


def _conv_tok_kernel(x_ref, mod_ref, g_ref, win_ref, wc_ref, wout_ref, p1_ref, p2_ref, xo_ref, u_ref, *, t_seq):
    tm = x_ref.shape[0]
    bg, u = _conv_in(x_ref, mod_ref, g_ref, win_ref)
    row = lax.broadcasted_iota(jnp.int32, (tm, 1), 0) % t_seq
    prev1 = jnp.where(row >= 1, pltpu.roll(u, 1, 0), p1_ref[...])
    prev2 = jnp.where(row >= 2, pltpu.roll(u, 2, 0), p2_ref[...])
    conv = wc_ref[0:1, :] * prev2 + wc_ref[1:2, :] * prev1 + wc_ref[2:3, :] * u
    u_ref[...] = u
    _conv_out(bg, conv, x_ref, mod_ref, wout_ref, xo_ref)


def _conv_mixer(x, mod3, tpb, g, w_in, w_conv, w_out, state, b, t_seq, tm):
    t, d = x.shape
    full = lambda shp: pl.BlockSpec(shp, lambda *_: (0,) * len(shp))
    weights = [full((1, d)), full(w_in.shape), full(w_conv.shape), full(w_out.shape)]
    if state is None:
        state = jnp.zeros((b, 2, d), F32)
    if t_seq >= tm:
        nt = t_seq // tm
        xo, nst = pl.pallas_call(
            _conv_seq_kernel,
            out_shape=(jax.ShapeDtypeStruct((t, d), F32), jax.ShapeDtypeStruct((b, 2, d), F32)),
            grid=(b, nt),
            in_specs=[pl.BlockSpec((tm, d), lambda i, j: (i * nt + j, 0)),
                      pl.BlockSpec((1, 1, mod3.shape[-1]), lambda i, j: (i, 0, 0)),
                      *weights,
                      pl.BlockSpec((1, 2, d), lambda i, j: (i, 0, 0))],
            out_specs=(pl.BlockSpec((tm, d), lambda i, j: (i * nt + j, 0)),
                       pl.BlockSpec((1, 2, d), lambda i, j: (i, 0, 0))),
            scratch_shapes=[pltpu.VMEM((8, d), F32)],
            compiler_params=_cparams("arbitrary", "arbitrary"),
            name="conv_seq",
        )(x, mod3, g, w_in, w_conv, w_out, state)
        return xo, nst
    assert t == tm and t_seq >= 2
    p1 = jnp.zeros((b, t_seq, d), F32).at[:, 0].set(state[:, 1]).reshape(t, d)
    p2 = jnp.zeros((b, t_seq, d), F32).at[:, 0].set(state[:, 0]).at[:, 1].set(state[:, 1]).reshape(t, d)
    xo, u = pl.pallas_call(
        functools.partial(_conv_tok_kernel, t_seq=t_seq),
        out_shape=(jax.ShapeDtypeStruct((t, d), F32), jax.ShapeDtypeStruct((t, d), F32)),
        grid=(1,),
        in_specs=[full((t, d)), full(mod3.shape), *weights, full((t, d)), full((t, d))],
        out_specs=(full((t, d)), full((t, d))),
        compiler_params=_cparams("arbitrary"),
        name="conv_tok",
    )(x, mod3, g, w_in, w_conv, w_out, p1, p2)
    return xo, u.reshape(b, t_seq, d)[:, t_seq - 2:]


def _split_bf16(x):
    hi = x.astype(BF16)
    return hi, (x - hi.astype(F32)).astype(BF16)


def _moe_route(h32, wr_ref, br_ref, tri_blk):
    tm = h32.shape[0]
    lane = lax.broadcasted_iota(jnp.int32, (1, V7X_LANES), 1)
    h_hi, h_lo = _split_bf16(h32)
    w_hi, w_lo = _split_bf16(wr_ref[...])
    logits = _dot(h_hi, w_hi) + (_dot(h_hi, w_lo) + _dot(h_lo, w_hi)) + br_ref[...]
    logits = jnp.where(lane < N_EXPERTS, logits, NEG_INF)
    m1 = jnp.max(logits, axis=-1, keepdims=True)
    i1 = jnp.min(jnp.where(logits == m1, lane, V7X_LANES), axis=-1, keepdims=True)
    rest = jnp.where(lane == i1, NEG_INF, logits)
    m2 = jnp.max(rest, axis=-1, keepdims=True)
    i2 = jnp.min(jnp.where(rest == m2, lane, V7X_LANES), axis=-1, keepdims=True)
    e2 = jnp.exp(m2 - m1)
    den = 1.0 + e2
    comb = jnp.where(lane == i1, 1.0 / den, jnp.where(lane == i2, e2 / den, 0.0))
    mask = jnp.where(lane == i1, 1.0, jnp.where(lane == i2, 1.0, 0.0))
    r = lax.broadcasted_iota(jnp.int32, (tri_blk, tri_blk), 0)
    c = lax.broadcasted_iota(jnp.int32, (tri_blk, tri_blk), 1)
    lower = jnp.where(c < r, 1.0, 0.0).astype(BF16)
    carry = jnp.zeros((1, V7X_LANES), F32)
    pos = []
    for blk in range(tm // tri_blk):
        mb = mask[blk * tri_blk:(blk + 1) * tri_blk]
        pos.append(_dot(lower, mb.astype(BF16)) + carry)
        carry = carry + jnp.sum(mb, axis=0, keepdims=True)
    return mask, comb, jnp.concatenate(pos, axis=0), carry


def _moe_kernel(x_ref, mod_ref, g_ref, wr_ref, br_ref, wg_ref, wu_ref, wd_ref, o_ref,
                h_scr, pos_scr, comb_scr, post_scr, hc_scr, acc_scr, cnt_smem, *, ch, tri_blk):
    tm, d = x_ref.shape
    e = pl.program_id(1)
    f = pl.program_id(2)
    nf = pl.num_programs(2)
    lane = lax.broadcasted_iota(jnp.int32, (1, V7X_LANES), 1)

    @pl.when((e == 0) & (f == 0))
    def _():
        mod = mod_ref[0]
        h32 = _norm_mod(x_ref[...], g_ref[...], mod[:, 4 * d:5 * d], mod[:, 3 * d:4 * d])
        h_scr[...] = h32.astype(BF16)
        mask, comb, pos, total = _moe_route(h32, wr_ref, br_ref, tri_blk)
        comb_scr[...] = comb
        pos_scr[...] = jnp.where(mask > 0.5, pos, -1.0)
        eye = jnp.where(lax.broadcasted_iota(jnp.int32, (V7X_LANES, V7X_LANES), 0)
                        == lax.broadcasted_iota(jnp.int32, (V7X_LANES, V7X_LANES), 1), 1.0, 0.0).astype(BF16)
        mask_t = _dot_nt(eye, mask.astype(BF16))[0:N_EXPERTS]
        r = lax.broadcasted_iota(jnp.int32, (tri_blk, tri_blk), 0)
        c = lax.broadcasted_iota(jnp.int32, (tri_blk, tri_blk), 1)
        upper = jnp.where(r < c, 1.0, 0.0).astype(BF16)
        carry = jnp.zeros((N_EXPERTS, 1), F32)
        for blk in range(tm // tri_blk):
            mb = mask_t[:, blk * tri_blk:(blk + 1) * tri_blk]
            pt = _dot(mb.astype(BF16), upper) + carry
            post_scr[:, blk * tri_blk:(blk + 1) * tri_blk] = jnp.where(mb > 0.5, pt, -1.0)
            carry = carry + jnp.sum(mb, axis=1, keepdims=True)
        for k in range(N_EXPERTS):
            cnt_smem[k] = total[0, k].astype(jnp.int32)
        o_ref[...] = jnp.zeros(o_ref.shape, F32)

    n_chunks = (cnt_smem[e] + ch - 1) // ch

    @pl.when(f == 0)
    def _():
        pos_t = post_scr[pl.ds(e, 1), :]

        def gather(j, _):
            r0 = pl.multiple_of(j * ch, ch)
            want = (r0 + lax.broadcasted_iota(jnp.int32, (ch, 1), 0)).astype(F32)
            onehot = jnp.where(pos_t == want, 1.0, 0.0).astype(BF16)
            hc_scr[pl.ds(r0, ch), :] = _dot(onehot, h_scr[...]).astype(BF16)
            acc_scr[pl.ds(r0, ch), :] = jnp.zeros((ch, d), F32)
            return 0

        lax.fori_loop(0, n_chunks, gather, 0)

    wg = wg_ref[0].astype(BF16)
    wu = wu_ref[0].astype(BF16)
    wd = wd_ref[0].astype(BF16)

    def expert(j, _):
        r0 = pl.multiple_of(j * ch, ch)
        hc = hc_scr[pl.ds(r0, ch), :]
        a = _dot(hc, wg)
        act = (a * _sigmoid(a) * _dot(hc, wu)).astype(BF16)
        acc_scr[pl.ds(r0, ch), :] += _dot(act, wd)
        return 0

    lax.fori_loop(0, n_chunks, expert, 0)

    @pl.when(f == nf - 1)
    def _():
        pos_c = jnp.sum(jnp.where(lane == e, pos_scr[...], 0.0), axis=-1, keepdims=True)
        comb_c = jnp.sum(jnp.where(lane == e, comb_scr[...], 0.0), axis=-1, keepdims=True)

        def scatter(j, _):
            r0 = pl.multiple_of(j * ch, ch)
            want = (r0 + lax.broadcasted_iota(jnp.int32, (1, ch), 1)).astype(F32)
            onehot = jnp.where(pos_c == want, 1.0, 0.0).astype(BF16)
            o_ref[...] += comb_c * _dot(onehot, acc_scr[pl.ds(r0, ch), :].astype(BF16))
            return 0

        lax.fori_loop(0, n_chunks, scatter, 0)

    @pl.when((e == N_EXPERTS - 1) & (f == nf - 1))
    def _():
        o_ref[...] = x_ref[...] + mod_ref[0][:, 5 * d:6 * d] * o_ref[...]


def _moe(x, mod3, tpb, g, w_router, b_router, w_gu, w_down, tm, tf, ch):
    t, d = x.shape
    ne, ff, _ = w_down.shape
    nf = ff // tf
    tri_blk = min(256, tm)
    wr = jnp.pad(w_router, ((0, 0), (0, V7X_LANES - ne)))
    br = jnp.pad(b_router, (0, V7X_LANES - ne)).reshape(1, V7X_LANES)
    return pl.pallas_call(
        functools.partial(_moe_kernel, ch=ch, tri_blk=tri_blk),
        out_shape=jax.ShapeDtypeStruct((t, d), F32),
        grid=(t // tm, ne, nf),
        in_specs=[
            pl.BlockSpec((tm, d), lambda i, e, f: (i, 0)),
            _mod_spec(mod3, tpb),
            pl.BlockSpec((1, d), lambda i, e, f: (0, 0)),
            pl.BlockSpec((d, V7X_LANES), lambda i, e, f: (0, 0)),
            pl.BlockSpec((1, V7X_LANES), lambda i, e, f: (0, 0)),
            pl.BlockSpec((1, d, tf), lambda i, e, f: (e, 0, f)),
            pl.BlockSpec((1, d, tf), lambda i, e, f: (e, 0, nf + f)),
            pl.BlockSpec((1, tf, d), lambda i, e, f: (e, f, 0)),
        ],
        out_specs=pl.BlockSpec((tm, d), lambda i, e, f: (i, 0)),
        scratch_shapes=[
            pltpu.VMEM((tm, d), BF16),
            pltpu.VMEM((tm, V7X_LANES), F32),
            pltpu.VMEM((tm, V7X_LANES), F32),
            pltpu.VMEM((N_EXPERTS, tm), F32),
            pltpu.VMEM((tm, d), BF16),
            pltpu.VMEM((tm, d), F32),
            pltpu.SMEM((N_EXPERTS,), jnp.int32),
        ],
        compiler_params=_cparams("arbitrary", "arbitrary", "arbitrary"),
        name="moe",
    )(x, mod3, g, wr, br, w_gu, w_gu, w_down)


def _final_norm_kernel(x_ref, g_ref, o_ref):
    x = x_ref[...]
    o_ref[...] = (x * lax.rsqrt(jnp.mean(x * x, axis=-1, keepdims=True) + EPS)) * g_ref[...]


def _final_norm(x, g, tm):
    t, d = x.shape
    return pl.pallas_call(
        _final_norm_kernel,
        out_shape=jax.ShapeDtypeStruct((t, d), F32),
        grid=(t // tm,),
        in_specs=[pl.BlockSpec((tm, d), lambda i: (i, 0)), pl.BlockSpec((1, d), lambda i: (0, 0))],
        out_specs=pl.BlockSpec((tm, d), lambda i: (i, 0)),
        compiler_params=_cparams("arbitrary"),
        name="final_norm",
    )(x, g)


def _nsa_layer_weights(w_in, w_out, w_ck, pe_ck, w_cv, pe_cv):
    n_main = N_HEADS * HEAD_DIM + 6 * KV_WIDTH
    w_gate = jnp.pad(w_in[:, n_main:], ((0, 0), (0, V7X_LANES - N_HEADS * N_BRANCH)))
    return {"w_main": w_in[:, :n_main].astype(BF16), "w_gate": w_gate.astype(BF16),
            "w_out": w_out.astype(BF16), "cw": _compress_weights(w_ck, pe_ck, w_cv, pe_cv)}


def _nsa_prompt_layer(x, mod, g, lw, b, s_len):
    tm = min(512, s_len)
    mod3 = mod.reshape(b, 1, mod.shape[-1])
    rope_t = _rope_tables(jnp.arange(s_len), V7X_LANES)
    q, cmp_rows, sel_rows, win_rows, gates = _nsa_proj(x, mod3, s_len // tm, g, lw["w_main"], lw["w_gate"],
                                                        rope_t, s_len, tm)
    kvw = 2 * KV_WIDTH
    ck, cv = _compress(cmp_rows.reshape(b, s_len, kvw), lw["cw"])
    o = _attn_prompt(q, gates, ck, cv, sel_rows.reshape(b, s_len, kvw), win_rows.reshape(b, s_len, kvw), tq=128)
    x = _out_proj(o, lw["w_out"], x, mod3, s_len // tm, tm)
    return x, cmp_rows, sel_rows, win_rows


DECODE_TOK_PAD = 8


def _nsa_decode_layer(x, mod3, g, lw, b, t_seq, caches, layer):
    t = x.shape[0]
    page_table = caches["page_table"]
    past_len = page_table.shape[1] * caches["cmp"].shape[2]
    rope_t = _rope_tables(past_len + jnp.arange(t) % t_seq, V7X_LANES)
    q, cmp_rows, sel_rows, win_rows, gates = _nsa_proj(x, mod3, 1, g, lw["w_main"], lw["w_gate"], rope_t, t, t)
    pad3 = lambda a: jnp.pad(a.reshape(b, t_seq, a.shape[-1]), ((0, 0), (0, DECODE_TOK_PAD - t_seq), (0, 0)))
    o = _attn_decode(page_table, pad3(q), pad3(gates), pad3(sel_rows), pad3(win_rows),
                     caches["cmp"], caches["sel"], caches["win"], layer, lw["cw"])
    o = o[:, :t_seq].reshape(t, o.shape[-1]).astype(BF16)
    x = _out_proj(o, lw["w_out"], x, mod3, 1, t)
    return x, cmp_rows, sel_rows, win_rows


def _trunk(x3, mod, caches, p):
    b, t_seq, d = x3.shape
    t = b * t_seq
    kvw = 2 * KV_WIDTH
    prompt = caches is None
    x = x3.reshape(t, d)
    new_cmp, new_sel, new_win, new_conv = [], [], [], []
    depth = mod.shape[0]
    for i in range(depth):
        j = i // 2
        if prompt:
            mod3 = mod[i].reshape(b, 1, 6 * d)
            tiles = lambda tm: t_seq // tm
            tm_small, tm_big = min(512, t_seq), min(1024, t_seq)
        else:
            mod3 = jnp.repeat(mod[i], t_seq, axis=0).reshape(1, t, 6 * d)
            tiles = lambda tm: 1
            tm_small = tm_big = t
        g_mix = p["g_norm_mix"][i].reshape(1, d)
        g_ffn = p["g_norm_ffn"][i].reshape(1, d)
        if i % 2 == 0:
            lw = _nsa_layer_weights(p["w_nsa_in"][j], p["w_nsa_out"][j], p["w_cmp_k"][j], p["pe_cmp_k"][j],
                                    p["w_cmp_v"][j], p["pe_cmp_v"][j])
            if prompt:
                x, c_rows, s_rows, w_rows = _nsa_prompt_layer(x, mod[i], g_mix, lw, b, t_seq)
                w_state = w_rows.reshape(b, t_seq, kvw)[:, t_seq - min(WINDOW, t_seq):]
            else:
                x, c_rows, s_rows, w_rows = _nsa_decode_layer(x, mod3, g_mix, lw, b, t_seq, caches, j)
                wb = caches["win"].shape[2]
                w_state = jnp.concatenate([caches["win"][j], w_rows.reshape(b, t_seq, kvw)], axis=1)[:, t_seq:]
                assert w_state.shape[1] == wb
            new_cmp.append(c_rows.reshape(b, t_seq, 2, KV_HEADS, HEAD_DIM))
            new_sel.append(s_rows.reshape(b, t_seq, 2, KV_HEADS, HEAD_DIM))
            new_win.append(w_state.reshape(b, -1, 2, KV_HEADS, HEAD_DIM))
            x = _ffn(x, mod3, tiles(tm_big), g_ffn, p["w_ff_gu"][j], p["w_ff_down"][j], tm_big, 512)
        else:
            state = None if prompt else caches["conv"][j]
            x, st = _conv_mixer(x, mod3, tiles(tm_small), g_mix, p["w_conv_in"][j].astype(BF16), p["w_conv"][j],
                                p["w_conv_out"][j].astype(BF16), state, b, t_seq, tm_small)
            new_conv.append(st)
            x = _moe(x, mod3, tiles(tm_big), g_ffn, p["w_router"][j], p["b_router"][j], p["w_moe_gu"][j],
                     p["w_moe_down"][j], tm_big, 512, 128)
    y = _final_norm(x, p["g_norm_final"].reshape(1, d), tm_big)
    return (y.reshape(b, t_seq, d), jnp.stack(new_cmp), jnp.stack(new_sel), jnp.stack(new_win), jnp.stack(new_conv))


def kernel(x_prompt, x_sample, cache_cmp_kv, cache_sel_kv, cache_win_kv, state_conv, page_table, c_prompt, c_sample,
           w_ada, b_ada, g_norm_mix, g_norm_ffn, g_norm_final, w_nsa_in, w_nsa_out, w_cmp_k, pe_cmp_k, w_cmp_v,
           pe_cmp_v, w_conv_in, w_conv, w_conv_out, w_ff_gu, w_ff_down, w_router, b_router, w_moe_gu, w_moe_down):
    p = {"g_norm_mix": g_norm_mix, "g_norm_ffn": g_norm_ffn, "g_norm_final": g_norm_final,
         "w_nsa_in": w_nsa_in, "w_nsa_out": w_nsa_out, "w_cmp_k": w_cmp_k, "pe_cmp_k": pe_cmp_k, "w_cmp_v": w_cmp_v,
         "pe_cmp_v": pe_cmp_v, "w_conv_in": w_conv_in, "w_conv": w_conv, "w_conv_out": w_conv_out,
         "w_ff_gu": w_ff_gu, "w_ff_down": w_ff_down, "w_router": w_router, "b_router": b_router,
         "w_moe_gu": w_moe_gu, "w_moe_down": w_moe_down}
    nb = x_prompt.shape[0]
    mod = _ada(jnp.concatenate([c_prompt, c_sample], axis=0), w_ada, b_ada)
    y_p, cmp_p, sel_p, win_p, conv_p = _trunk(x_prompt, mod[:, :nb], None, p)
    kvw = 2 * KV_WIDTH
    flat = lambda c: c.reshape(c.shape[:3] + (kvw,))
    caches = {"cmp": flat(cache_cmp_kv), "sel": flat(cache_sel_kv), "win": flat(cache_win_kv), "conv": state_conv,
              "page_table": page_table}
    y_s, cmp_s, sel_s, win_s, conv_s = _trunk(x_sample, mod[:, nb:], caches, p)
    return (y_p, y_s, cmp_p, cmp_s, sel_p, sel_s, win_p, win_s, conv_p, conv_s)
```

```python
import functools

import jax
import jax.numpy as jnp
from jax import lax
from jax.experimental import pallas as pl
from jax.experimental.pallas import tpu as pltpu

F32 = jnp.float32
BF16 = jnp.bfloat16

N_HEADS = 16
HEAD_DIM = 64
KV_HEADS = 4
HEADS_PER_KV = N_HEADS // KV_HEADS
KV_WIDTH = KV_HEADS * HEAD_DIM
ROT_DIM = HEAD_DIM // 4
ROPE_THETA = 500000.0
CMP_BLOCK = 32
SEL_BLOCK = 64
TOP_N = 8
WINDOW = 512
N_BRANCH = 3
FORCE_BONUS = float(HEADS_PER_KV + 1)
N_EXPERTS = 8
EPS = 1e-6
NEG_INF = float("-inf")

V7X_LANES = 128
V7X_SUBLANES = 8
V7X_VMEM_LIMIT_BYTES = 58 * 1024 * 1024


def _cparams(*sem):
    return pltpu.CompilerParams(dimension_semantics=sem, vmem_limit_bytes=V7X_VMEM_LIMIT_BYTES)


def _dot(a, b):
    return jnp.dot(a, b, preferred_element_type=F32)


def _dot_nt(a, b):
    return lax.dot_general(a, b, (((1,), (1,)), ((), ())), preferred_element_type=F32)


def _norm_mod(x, g, sc, sh):
    y = x * lax.rsqrt(jnp.mean(x * x, axis=-1, keepdims=True) + EPS)
    return (y * g) * (1.0 + sc) + sh


def _sigmoid(x):
    return 1.0 / (1.0 + jnp.exp(-x))


def _mod_spec(mod3, tiles_per_block):
    _, r, w = mod3.shape
    return pl.BlockSpec((1, r, w), lambda i, *_: (i // tiles_per_block, 0, 0))


def _full(shape):
    return pl.BlockSpec(shape, lambda *_: (0,) * len(shape))


def _ada_kernel(c_ref, w_ref, b_ref, o_ref):
    c = c_ref[...]
    cs = (c * _sigmoid(c)).astype(BF16)
    o_ref[0] = _dot(cs, w_ref[0].astype(BF16)) + b_ref[0]


def _ada(c, w_ada, b_ada):
    depth, d, n = w_ada.shape
    nb = c.shape[0]
    tn = 1536
    return pl.pallas_call(
        _ada_kernel,
        out_shape=jax.ShapeDtypeStruct((depth, nb, n), F32),
        grid=(depth, n // tn),
        in_specs=[
            pl.BlockSpec((nb, d), lambda i, j: (0, 0)),
            pl.BlockSpec((1, d, tn), lambda i, j: (i, 0, j)),
            pl.BlockSpec((1, 1, tn), lambda i, j: (i, 0, j)),
        ],
        out_specs=pl.BlockSpec((1, nb, tn), lambda i, j: (i, 0, j)),
        compiler_params=_cparams("arbitrary", "arbitrary"),
        name="ada",
    )(c, w_ada, b_ada.reshape(depth, 1, n))


def _rope_tables(pos, width):
    half = ROT_DIM // 2
    inv = 1.0 / (ROPE_THETA ** (jnp.arange(half, dtype=F32) * (2.0 / ROT_DIM)))
    ang = pos.astype(F32)[:, None] * inv[None, :]
    cos, sin = jnp.cos(ang), jnp.sin(ang)
    n = pos.shape[0]
    z8 = jnp.zeros((n, half), F32)
    rest0 = jnp.zeros((n, HEAD_DIM - ROT_DIM), F32)
    c = jnp.concatenate([cos, cos, rest0 + 1.0], axis=1)
    a = jnp.concatenate([-sin, z8, rest0], axis=1)
    b = jnp.concatenate([z8, sin, rest0], axis=1)
    rep = width // HEAD_DIM
    return tuple(jnp.tile(t, (1, rep)) for t in (c, a, b))


def _rope(v, c, a, b):
    w = v.shape[-1]
    half = ROT_DIM // 2
    return v * c + pltpu.roll(v, w - half, 1) * a + pltpu.roll(v, half, 1) * b


def _widen(t, width):
    return jnp.concatenate([t] * (width // t.shape[-1]), axis=1)


def _nsa_proj_kernel(x_ref, mod_ref, g_ref, w_ref, wg_ref, rc_ref, ra_ref, rb_ref, q_ref, gate_ref, *kv_refs,
                     feature_major):
    d = x_ref.shape[1]
    mod = mod_ref[0]
    h = _norm_mod(x_ref[...], g_ref[...], mod[:, d:2 * d], mod[:, 0:d]).astype(BF16)
    kvw = 2 * KV_WIDTH
    c1, a1, b1 = rc_ref[...], ra_ref[...], rb_ref[...]
    c2, a2, b2 = (_widen(t, kvw) for t in (c1, a1, b1))
    scale = HEAD_DIM ** -0.5
    hq = N_HEADS * HEAD_DIM
    for j in range(hq // kvw):
        p = _dot(h, w_ref[:, j * kvw:(j + 1) * kvw])
        q_ref[:, j * kvw:(j + 1) * kvw] = (_rope(p, c2, a2, b2) * scale).astype(BF16)
    gate_ref[...] = _sigmoid(_dot(h, wg_ref[...]))
    one = jnp.ones_like(c1)
    zero = jnp.zeros_like(c1)
    ck = jnp.concatenate([_widen(c1, KV_WIDTH), _widen(one, KV_WIDTH)], axis=1)
    ak = jnp.concatenate([_widen(a1, KV_WIDTH), _widen(zero, KV_WIDTH)], axis=1)
    bk = jnp.concatenate([_widen(b1, KV_WIDTH), _widen(zero, KV_WIDTH)], axis=1)
    cmp_rows = _dot(h, w_ref[:, hq:hq + kvw])
    sel_rows = _rope(_dot(h, w_ref[:, hq + kvw:hq + 2 * kvw]), ck, ak, bk)
    win_rows = _rope(_dot(h, w_ref[:, hq + 2 * kvw:hq + 3 * kvw]), ck, ak, bk)
    if feature_major:
        cmp_ref, cmpt_ref, selt_ref, wint_ref = kv_refs
        cmp_ref[...] = cmp_rows
        cmpt_ref[0] = cmp_rows.T
        selt_ref[0] = sel_rows.T
        wint_ref[0] = win_rows.T
    else:
        cmp_ref, sel_ref, win_ref = kv_refs
        cmp_ref[...] = cmp_rows
        sel_ref[...] = sel_rows
        win_ref[...] = win_rows


def _nsa_proj(x, mod3, tpb, g, w_main, w_gate, rope_t, rows_per_seq, tm, feature_major):
    t, d = x.shape
    nw = w_main.shape[1]
    kvw = 2 * KV_WIDTH
    hq = N_HEADS * HEAD_DIM
    nrt = rows_per_seq // tm
    rspec = pl.BlockSpec((tm, V7X_LANES), lambda i: (i % nrt, 0))
    tok = lambda w: pl.BlockSpec((tm, w), lambda i: (i, 0))
    out_shape = [jax.ShapeDtypeStruct((t, hq), BF16), jax.ShapeDtypeStruct((t, V7X_LANES), F32)]
    out_specs = [tok(hq), tok(V7X_LANES)]
    if feature_major:
        b = t // rows_per_seq
        fm = pl.BlockSpec((1, kvw, tm), lambda i: (i // nrt, 0, i % nrt))
        out_shape += [jax.ShapeDtypeStruct((t, kvw), F32)] + [jax.ShapeDtypeStruct((b, kvw, rows_per_seq), F32)] * 3
        out_specs += [tok(kvw), fm, fm, fm]
    else:
        out_shape += [jax.ShapeDtypeStruct((t, kvw), F32)] * 3
        out_specs += [tok(kvw)] * 3
    return pl.pallas_call(
        functools.partial(_nsa_proj_kernel, feature_major=feature_major),
        out_shape=tuple(out_shape),
        grid=(t // tm,),
        in_specs=[tok(d), _mod_spec(mod3, tpb), _full((1, d)), _full((d, nw)), _full((d, V7X_LANES)),
                  rspec, rspec, rspec],
        out_specs=tuple(out_specs),
        compiler_params=_cparams("arbitrary"),
        name="nsa_proj",
    )(x, mod3, g, w_main, w_gate, *rope_t)


def _compress_rows(load_rows, bdk_ref, bdv_ref, pek_ref, pev_ref, nc):
    half = nc // 2
    acck = jnp.zeros((nc, KV_WIDTH), F32)
    accv = jnp.zeros((nc, KV_WIDTH), F32)
    for l in range(CMP_BLOCK):
        even = load_rows(l, half)
        odd = load_rows(CMP_BLOCK + l, half)
        xb = jnp.concatenate([even, odd], axis=0)
        xk = (xb[:, 0:KV_WIDTH] + pek_ref[l:l + 1, :]).astype(BF16)
        xv = (xb[:, KV_WIDTH:2 * KV_WIDTH] + pev_ref[l:l + 1, :]).astype(BF16)
        acck = acck + _dot(xk, bdk_ref[l])
        accv = accv + _dot(xv, bdv_ref[l])
    return acck, accv


def _compress_kernel(r0_ref, r1_ref, r2_ref, r3_ref, bdk_ref, bdv_ref, pek_ref, pev_ref, rc_ref, ra_ref, rb_ref,
                     ck_ref, cv_ref):
    nc = ck_ref.shape[1]

    def load_rows(start, n):
        idx = pl.ds(start, n, stride=2 * CMP_BLOCK)
        return jnp.concatenate([r[0, idx, :] for r in (r0_ref, r1_ref, r2_ref, r3_ref)], axis=1)

    acck, accv = _compress_rows(load_rows, bdk_ref, bdv_ref, pek_ref, pev_ref, nc)
    ck_ref[0] = _rope(acck, rc_ref[...], ra_ref[...], rb_ref[...]).astype(BF16)
    cv_ref[0] = accv.astype(BF16)


def _cmp_order_end_pos(nc):
    n = jnp.arange(nc)
    half = nc // 2
    blk = jnp.where(n < half, 2 * n, 2 * (n - half) + 1)
    return blk * CMP_BLOCK + (CMP_BLOCK - 1)


def _cmp_end_row(nc):
    half = nc // 2
    col = lax.broadcasted_iota(jnp.int32, (1, nc), 1)
    return jnp.where(col < half, 2 * CMP_BLOCK * col + (CMP_BLOCK - 1),
                     2 * CMP_BLOCK * (col - half) + (2 * CMP_BLOCK - 1))


def _compress(rows3, cw):
    b, tk, w = rows3.shape
    nc = tk // CMP_BLOCK
    assert nc % 2 == 0 and nc * CMP_BLOCK == tk
    rt = _rope_tables(_cmp_order_end_pos(nc), KV_WIDTH)
    return pl.pallas_call(
        _compress_kernel,
        out_shape=(jax.ShapeDtypeStruct((b, nc, KV_WIDTH), BF16),) * 2,
        grid=(b,),
        in_specs=[
            *[pl.BlockSpec((1, tk, V7X_LANES), functools.partial(lambda i, c: (i, 0, c), c=c))
              for c in range(w // V7X_LANES)],
            _full(cw["bdk"].shape), _full(cw["bdv"].shape), _full(cw["pek"].shape), _full(cw["pev"].shape),
            _full(rt[0].shape), _full(rt[1].shape), _full(rt[2].shape),
        ],
        out_specs=(pl.BlockSpec((1, nc, KV_WIDTH), lambda i: (i, 0, 0)),) * 2,
        compiler_params=_cparams("arbitrary"),
        name="compress",
    )(rows3, rows3, rows3, rows3, cw["bdk"], cw["bdv"], cw["pek"], cw["pev"], *rt)


def _compress_weights(w_ck, pe_ck, w_cv, pe_cv):
    eye = jnp.eye(KV_HEADS, dtype=F32)

    def bd(w):
        return jnp.einsum("gh,lde->lgdhe", eye, w).reshape(CMP_BLOCK, KV_WIDTH, KV_WIDTH).astype(BF16)

    return {"bdk": bd(w_ck), "bdv": bd(w_cv),
            "pek": jnp.tile(pe_ck, (1, KV_HEADS)), "pev": jnp.tile(pe_cv, (1, KV_HEADS))}


def _softmax_rows(s):
    m = jnp.max(s, axis=-1, keepdims=True)
    m = jnp.where(m == NEG_INF, 0.0, m)
    e = jnp.exp(s - m)
    return e / jnp.maximum(jnp.sum(e, axis=-1, keepdims=True), 1e-30)


def _masked_softmax(s, mask):
    return _softmax_rows(jnp.where(mask, s, NEG_INF))


def _select_blocks(imp, t_col, ns):
    blk = lax.broadcasted_iota(jnp.int32, (1, ns), 1)
    cur = t_col // SEL_BLOCK
    forced = jnp.where(blk == 0, 1.0, jnp.where(blk == cur, 1.0, 0.0))
    score = jnp.where(blk <= cur, imp + FORCE_BONUS * forced, NEG_INF)
    rank = jnp.zeros(score.shape, F32)
    for i in range(ns):
        ci = score[:, i:i + 1]
        later = jnp.where(blk > i, 1.0, 0.0)
        rank = rank + jnp.where(ci > score, 1.0, jnp.where(ci == score, later, 0.0))
    return jnp.where(rank < float(min(TOP_N, ns)), 1.0, 0.0)


def _select_blocks_t(imp_t, t_row, ns):
    blk = lax.broadcasted_iota(jnp.int32, (ns, 1), 0)
    cur = t_row // SEL_BLOCK
    forced = jnp.where(blk == 0, 1.0, jnp.where(blk == cur, 1.0, 0.0))
    score = jnp.where(blk <= cur, imp_t + FORCE_BONUS * forced, NEG_INF)
    rank = jnp.zeros(score.shape, F32)
    for i in range(ns):
        ci = score[i:i + 1, :]
        later = jnp.where(blk > i, 1.0, 0.0)
        rank = rank + jnp.where(ci > score, 1.0, jnp.where(ci == score, later, 0.0))
    return jnp.where(rank < float(min(TOP_N, ns)), 1.0, 0.0)


def _stack_heads(x, g):
    base = g * HEADS_PER_KV * HEAD_DIM
    return jnp.concatenate([x[:, base + r * HEAD_DIM: base + (r + 1) * HEAD_DIM] for r in range(HEADS_PER_KV)],
                           axis=0)


def _gate_cols(gates, g, branch):
    cols = [(g * HEADS_PER_KV + r) * N_BRANCH + branch for r in range(HEADS_PER_KV)]
    return jnp.concatenate([gates[:, c:c + 1] for c in cols], axis=0)


def _pad_to(x, rows, cols):
    r, c = x.shape
    if c < cols:
        x = jnp.concatenate([x, jnp.zeros((r, cols - c), x.dtype)], axis=1)
    if r < rows:
        x = jnp.concatenate([x, jnp.zeros((rows - r, x.shape[1]), x.dtype)], axis=0)
    return x


WIN_CHUNK = V7X_LANES


def _attn_prompt_kernel(q_ref, gate_ref, ck_ref, cv_ref, selt_ref, wint_ref, o_ref,
                        ks_scr, vs_scr, kw_scr, vw_scr, bias_scr, *, tk):
    tq = q_ref.shape[0]
    s_len = selt_ref.shape[2]
    nc = ck_ref.shape[1]
    half = nc // 2
    ns = s_len // SEL_BLOCK
    qi = pl.program_id(1)

    @pl.when(qi == 0)
    def _():
        for c in range(s_len // tk):
            ks_scr[c] = selt_ref[0, 0:KV_WIDTH, c * tk:(c + 1) * tk].astype(BF16)
            vs_scr[c] = selt_ref[0, KV_WIDTH:2 * KV_WIDTH, c * tk:(c + 1) * tk].astype(BF16)
        for c in range(s_len // WIN_CHUNK):
            kw_scr[c] = wint_ref[0, 0:KV_WIDTH, c * WIN_CHUNK:(c + 1) * WIN_CHUNK].astype(BF16)
            vw_scr[c] = wint_ref[0, KV_WIDTH:2 * KV_WIDTH, c * WIN_CHUNK:(c + 1) * WIN_CHUNK].astype(BF16)

    q0 = qi * tq
    t_tile = q0 + lax.broadcasted_iota(jnp.int32, (tq, 1), 0)
    t_row = q0 + lax.broadcasted_iota(jnp.int32, (1, tq), 1)
    t_col4 = jnp.concatenate([t_tile] * HEADS_PER_KV, axis=0)
    q_all = q_ref[...]
    gates = gate_ref[...]
    c_end = _cmp_end_row(nc)
    n_chunks = (q0 + tq + tk - 1) // tk

    wlen = min(WINDOW + tq, s_len)
    n_wc = wlen // WIN_CHUNK
    w0 = jnp.clip(q0 + tq - wlen, 0, s_len - wlen)
    wc0 = w0 // WIN_CHUNK
    wpos = w0 + lax.broadcasted_iota(jnp.int32, (1, wlen), 1)
    wbias = jnp.where(wpos <= t_tile, jnp.where(wpos >= t_tile - WINDOW, 0.0, NEG_INF), NEG_INF)

    for g in range(KV_HEADS):
        rows_g = slice(g * HEAD_DIM, (g + 1) * HEAD_DIM)
        q4 = _stack_heads(q_all, g)
        p_c = _masked_softmax(_dot_nt(q4, ck_ref[0, :, rows_g]), c_end <= t_col4)
        o_c = _dot(p_c.astype(BF16), cv_ref[0, :, rows_g])
        psum = p_c[0:tq]
        for r in range(1, HEADS_PER_KV):
            psum = psum + p_c[r * tq:(r + 1) * tq]
        psum_t = _pad_to(psum, tq, V7X_LANES).T
        imp_t = psum_t[0:half] + psum_t[half:nc]
        sel_t = _select_blocks_t(imp_t, t_row, ns)
        sel = _pad_to(sel_t, V7X_LANES, tq).T[:, 0:ns].astype(BF16)

        def make_bias(c, _):
            kpos = c * tk + lax.broadcasted_iota(jnp.int32, (1, tk), 1)
            brow = lax.broadcasted_iota(jnp.int32, (ns, 1), 0)
            expand = jnp.where(kpos // SEL_BLOCK == brow, 1.0, 0.0).astype(BF16)
            chosen = _dot(sel, expand)
            bias_scr[c] = jnp.where(kpos <= t_tile, jnp.where(chosen > 0.5, 0.0, NEG_INF), NEG_INF)
            return 0

        lax.fori_loop(0, n_chunks, make_bias, 0)
        kw = jnp.concatenate([kw_scr[wc0 + i, rows_g, :] for i in range(n_wc)], axis=1)
        vw = jnp.concatenate([vw_scr[wc0 + i, rows_g, :] for i in range(n_wc)], axis=1)

        for r in range(HEADS_PER_KV):
            head = g * HEADS_PER_KV + r
            q_r = q_all[:, head * HEAD_DIM:(head + 1) * HEAD_DIM]

            def chunk(c, carry):
                m, l, acc = carry
                s = _dot(q_r, ks_scr[c, rows_g, :]) + bias_scr[c]
                m_new = jnp.maximum(m, jnp.max(s, axis=-1, keepdims=True))
                m_safe = jnp.where(m_new == NEG_INF, 0.0, m_new)
                alpha = jnp.exp(m - m_safe)
                p = jnp.exp(s - m_safe)
                l = alpha * l + jnp.sum(p, axis=-1, keepdims=True)
                acc = alpha * acc + _dot_nt(p.astype(BF16), vs_scr[c, rows_g, :])
                return m_new, l, acc

            init = (jnp.full((tq, 1), NEG_INF, F32), jnp.zeros((tq, 1), F32), jnp.zeros((tq, HEAD_DIM), F32))
            _, l_s, acc_s = lax.fori_loop(0, n_chunks, chunk, init)
            o_s = acc_s / jnp.maximum(l_s, 1e-30)

            p_w = _softmax_rows(_dot(q_r, kw) + wbias)
            o_w = _dot_nt(p_w.astype(BF16), vw)

            gcol = lambda br: gates[:, head * N_BRANCH + br:head * N_BRANCH + br + 1]
            o = gcol(0) * o_c[r * tq:(r + 1) * tq] + gcol(1) * o_s + gcol(2) * o_w
            o_ref[:, head * HEAD_DIM:(head + 1) * HEAD_DIM] = o.astype(BF16)


def _attn_prompt(q, gates, ck, cv, selt, wint, tq):
    b, kvw, s_len = selt.shape
    nc = ck.shape[1]
    hq = N_HEADS * HEAD_DIM
    nq = s_len // tq
    tk = min(512, s_len)
    assert s_len % tk == 0 and s_len % tq == 0 and tk % tq == 0 and tq == V7X_LANES
    assert min(WINDOW + tq, s_len) % WIN_CHUNK == 0
    return pl.pallas_call(
        functools.partial(_attn_prompt_kernel, tk=tk),
        out_shape=jax.ShapeDtypeStruct((b * s_len, hq), BF16),
        grid=(b, nq),
        in_specs=[
            pl.BlockSpec((tq, hq), lambda i, j: (i * nq + j, 0)),
            pl.BlockSpec((tq, V7X_LANES), lambda i, j: (i * nq + j, 0)),
            pl.BlockSpec((1, nc, KV_WIDTH), lambda i, j: (i, 0, 0)),
            pl.BlockSpec((1, nc, KV_WIDTH), lambda i, j: (i, 0, 0)),
            pl.BlockSpec((1, kvw, s_len), lambda i, j: (i, 0, 0)),
            pl.BlockSpec((1, kvw, s_len), lambda i, j: (i, 0, 0)),
        ],
        out_specs=pl.BlockSpec((tq, hq), lambda i, j: (i * nq + j, 0)),
        scratch_shapes=[
            pltpu.VMEM((s_len // tk, KV_WIDTH, tk), BF16),
            pltpu.VMEM((s_len // tk, KV_WIDTH, tk), BF16),
            pltpu.VMEM((s_len // WIN_CHUNK, KV_WIDTH, WIN_CHUNK), BF16),
            pltpu.VMEM((s_len // WIN_CHUNK, KV_WIDTH, WIN_CHUNK), BF16),
            pltpu.VMEM((s_len // tk, tq, tk), F32),
        ],
        compiler_params=_cparams("arbitrary", "arbitrary"),
        name="attn_prompt",
    )(q, gates, ck, cv, selt, wint)


PAGES_PER_STEP = 8
DECODE_TOK_PAD = V7X_SUBLANES


def _block_diag_q(q):
    parts = []
    for g in range(KV_HEADS):
        q4 = _stack_heads(q, g)
        z = jnp.zeros_like(q4)
        parts.append(jnp.concatenate([q4 if gg == g else z for gg in range(KV_HEADS)], axis=1))
    return jnp.concatenate(parts, axis=0)


def _attn_decode_kernel(pt_ref, q_ref, gate_ref, nsel_ref, nwin_ref, wbuf_ref, bdk_ref, bdv_ref, pek_ref, pev_ref,
                        rc_ref, ra_ref, rb_ref, *rest, page, n_pages):
    pps = PAGES_PER_STEP
    cpages = rest[:pps]
    spages = rest[pps:2 * pps]
    o_ref = rest[2 * pps]
    cmp_scr, s_scr, vs_scr, qbd_scr = rest[2 * pps + 1:]
    del pt_ref
    j = pl.program_id(1)
    tq = q_ref.shape[1]
    past_len = n_pages * page
    n_lane_chunks = 2 * KV_WIDTH // V7X_LANES

    @pl.when(j == 0)
    def _():
        qbd_scr[...] = _block_diag_q(q_ref[0].astype(F32)).astype(BF16)

    qbd = qbd_scr[...]
    for p in range(pps):
        pg = j * pps + p
        r0 = pl.multiple_of(pg * page, page)
        for c in range(n_lane_chunks):
            cmp_scr[c, pl.ds(r0, page), :] = cpages[p][0, 0, c * V7X_LANES:(c + 1) * V7X_LANES, :].T
        s_scr[pg] = _dot(qbd, spages[p][0, 0, 0:KV_WIDTH, :].astype(BF16))
        vs_scr[pg] = spages[p][0, 0, KV_WIDTH:2 * KV_WIDTH, :].astype(BF16)

    @pl.when(j == pl.num_programs(1) - 1)
    def _():
        tok = lax.broadcasted_iota(jnp.int32, (tq, 1), 0)
        t_rows = past_len + jnp.concatenate([tok] * N_HEADS, axis=0)
        nsel = _pad_to(nsel_ref[0], page, 2 * KV_WIDTH)
        s_scr[n_pages] = _dot_nt(qbd, nsel[:, 0:KV_WIDTH].astype(BF16))

        nc = past_len // CMP_BLOCK
        half = nc // 2

        def load_rows(start, n):
            idx = pl.ds(start, n, stride=2 * CMP_BLOCK)
            return jnp.concatenate([cmp_scr[c, idx, :] for c in range(n_lane_chunks)], axis=1)

        acck, accv = _compress_rows(load_rows, bdk_ref, bdv_ref, pek_ref, pev_ref, nc)
        ck = _rope(acck, rc_ref[...], ra_ref[...], rb_ref[...]).astype(BF16)
        p_c = _masked_softmax(_dot_nt(qbd, ck), _cmp_end_row(nc) <= t_rows)
        o_c = _dot(p_c.astype(BF16), accv.astype(BF16))

        ns = n_pages * page // SEL_BLOCK + 1
        grp = HEADS_PER_KV * tq
        psum = []
        for g in range(KV_HEADS):
            pg_ = p_c[g * grp:g * grp + tq]
            for r in range(1, HEADS_PER_KV):
                pg_ = pg_ + p_c[g * grp + r * tq:g * grp + (r + 1) * tq]
            psum.append(pg_)
        psum = jnp.concatenate(psum, axis=0)
        imp = jnp.concatenate([psum[:, 0:half] + psum[:, half:nc], jnp.zeros((KV_HEADS * tq, 1), F32)], axis=1)
        selm = _select_blocks(imp, past_len + jnp.concatenate([tok] * KV_HEADS, axis=0), ns)
        selm_rows = jnp.concatenate(
            [selm[g * tq:(g + 1) * tq] for g in range(KV_HEADS) for _ in range(HEADS_PER_KV)], axis=0)
        nkeys = (n_pages + 1) * page
        kpos = lax.broadcasted_iota(jnp.int32, (1, nkeys), 1)
        brow = lax.broadcasted_iota(jnp.int32, (ns, 1), 0)
        expand = jnp.where(kpos // SEL_BLOCK == brow, 1.0, 0.0).astype(BF16)
        chosen = _dot(selm_rows.astype(BF16), expand)
        s_all = jnp.concatenate([s_scr[i] for i in range(n_pages + 1)], axis=1)
        s_all = jnp.where(kpos <= t_rows, jnp.where(chosen > 0.5, s_all, NEG_INF), NEG_INF)
        p_s = _softmax_rows(s_all).astype(BF16)
        vt_all = jnp.concatenate([vs_scr[i] for i in range(n_pages)], axis=1)
        o_s = _dot_nt(p_s[:, 0:past_len], vt_all) + _dot(p_s[:, past_len:nkeys], nsel[:, KV_WIDTH:].astype(BF16))

        wb = wbuf_ref.shape[3]
        nwin = _pad_to(nwin_ref[0], page, 2 * KV_WIDTH)
        s_w = jnp.concatenate([_dot(qbd, wbuf_ref[0, 0, 0:KV_WIDTH, :].astype(BF16)),
                               _dot_nt(qbd, nwin[:, 0:KV_WIDTH].astype(BF16))], axis=1)
        wpos = past_len - wb + lax.broadcasted_iota(jnp.int32, (1, wb + page), 1)
        s_w = jnp.where(wpos <= t_rows, jnp.where(wpos >= t_rows - WINDOW, s_w, NEG_INF), NEG_INF)
        p_w = _softmax_rows(s_w).astype(BF16)
        o_w = (_dot_nt(p_w[:, 0:wb], wbuf_ref[0, 0, KV_WIDTH:2 * KV_WIDTH, :].astype(BF16))
               + _dot(p_w[:, wb:wb + page], nwin[:, KV_WIDTH:].astype(BF16)))

        gates = gate_ref[0]
        gcol = lambda br: jnp.concatenate([_gate_cols(gates, g, br) for g in range(KV_HEADS)], axis=0)
        o = gcol(0) * o_c + gcol(1) * o_s + gcol(2) * o_w
        for g in range(KV_HEADS):
            for r in range(HEADS_PER_KV):
                col = (g * HEADS_PER_KV + r) * HEAD_DIM
                o_ref[0, :, col:col + HEAD_DIM] = o[g * grp + r * tq:g * grp + (r + 1) * tq,
                                                    g * HEAD_DIM:(g + 1) * HEAD_DIM]


def _attn_decode(page_table, q3, gates3, nsel3, nwin3, cache_cmp_t, cache_sel_t, cache_win_t, layer, cw):
    b, tq, hq = q3.shape
    n_pages = page_table.shape[1]
    page = cache_cmp_t.shape[3]
    kvw = 2 * KV_WIDTH
    pps = PAGES_PER_STEP
    assert n_pages % pps == 0 and (n_pages * page) % (2 * CMP_BLOCK) == 0 and tq < CMP_BLOCK
    assert tq % V7X_SUBLANES == 0 and page == V7X_LANES
    nc = n_pages * page // CMP_BLOCK
    rt = _rope_tables(_cmp_order_end_pos(nc), KV_WIDTH)
    wb = cache_win_t.shape[3]
    full = lambda shp: pl.BlockSpec(shp, lambda *_: (0,) * len(shp))
    per_seq = lambda shp: pl.BlockSpec((1,) + shp, lambda i, j, pt: (i,) + (0,) * len(shp))
    page_spec = lambda p: pl.BlockSpec((1, 1, kvw, page), lambda i, j, pt: (layer, pt[i, j * pps + p], 0, 0))
    rows = N_HEADS * tq
    grid_spec = pltpu.PrefetchScalarGridSpec(
        num_scalar_prefetch=1,
        grid=(b, n_pages // pps),
        in_specs=[
            per_seq((tq, hq)), per_seq((tq, V7X_LANES)), per_seq((tq, kvw)), per_seq((tq, kvw)),
            pl.BlockSpec((1, 1, kvw, wb), lambda i, j, pt: (layer, i, 0, 0)),
            full(cw["bdk"].shape), full(cw["bdv"].shape), full(cw["pek"].shape), full(cw["pev"].shape),
            full(rt[0].shape), full(rt[1].shape), full(rt[2].shape),
            *[page_spec(p) for p in range(pps)],
            *[page_spec(p) for p in range(pps)],
        ],
        out_specs=per_seq((tq, hq)),
        scratch_shapes=[
            pltpu.VMEM((kvw // V7X_LANES, n_pages * page, V7X_LANES), F32),
            pltpu.VMEM((n_pages + 1, rows, page), F32),
            pltpu.VMEM((n_pages, KV_WIDTH, page), BF16),
            pltpu.VMEM((rows, KV_WIDTH), BF16),
        ],
    )
    return pl.pallas_call(
        functools.partial(_attn_decode_kernel, page=page, n_pages=n_pages),
        out_shape=jax.ShapeDtypeStruct((b, tq, hq), F32),
        grid_spec=grid_spec,
        compiler_params=_cparams("arbitrary", "arbitrary"),
        name="attn_decode",
    )(page_table, q3, gates3, nsel3, nwin3, cache_win_t, cw["bdk"], cw["bdv"], cw["pek"], cw["pev"], *rt,
      *([cache_cmp_t] * pps), *([cache_sel_t] * pps))


def _out_proj_kernel(o_ref, w_ref, x_ref, mod_ref, xo_ref):
    d = x_ref.shape[1]
    xo_ref[...] = x_ref[...] + mod_ref[0][:, 2 * d:3 * d] * _dot(o_ref[...], w_ref[...])


def _out_proj(o, w_bf16, x, mod3, tpb, tm):
    t, d = x.shape
    k = o.shape[1]
    return pl.pallas_call(
        _out_proj_kernel,
        out_shape=jax.ShapeDtypeStruct((t, d), F32),
        grid=(t // tm,),
        in_specs=[
            pl.BlockSpec((tm, k), lambda i: (i, 0)),
            pl.BlockSpec((k, d), lambda i: (0, 0)),
            pl.BlockSpec((tm, d), lambda i: (i, 0)),
            _mod_spec(mod3, tpb),
        ],
        out_specs=pl.BlockSpec((tm, d), lambda i: (i, 0)),
        compiler_params=_cparams("arbitrary"),
        name="out_proj",
    )(o, w_bf16, x, mod3)


def _ffn_kernel(x_ref, mod_ref, g_ref, wg_ref, wu_ref, wd_ref, o_ref, h_scr):
    d = x_ref.shape[1]
    f = pl.program_id(1)

    @pl.when(f == 0)
    def _():
        mod = mod_ref[0]
        h_scr[...] = _norm_mod(x_ref[...], g_ref[...], mod[:, 4 * d:5 * d], mod[:, 3 * d:4 * d]).astype(BF16)
        o_ref[...] = jnp.zeros(o_ref.shape, F32)

    h = h_scr[...]
    a = _dot(h, wg_ref[0].astype(BF16))
    u = _dot(h, wu_ref[0].astype(BF16))
    act = (a * _sigmoid(a) * u).astype(BF16)
    o_ref[...] += _dot(act, wd_ref[0].astype(BF16))

    @pl.when(f == pl.num_programs(1) - 1)
    def _():
        o_ref[...] = x_ref[...] + mod_ref[0][:, 5 * d:6 * d] * o_ref[...]


def _ffn(x, mod3, tpb, g, w_gu_all, w_down_all, layer, tm, tf):
    t, d = x.shape
    ff = w_down_all.shape[1]
    nf = ff // tf
    return pl.pallas_call(
        _ffn_kernel,
        out_shape=jax.ShapeDtypeStruct((t, d), F32),
        grid=(t // tm, nf),
        in_specs=[
            pl.BlockSpec((tm, d), lambda i, f: (i, 0)),
            _mod_spec(mod3, tpb),
            pl.BlockSpec((1, d), lambda i, f: (0, 0)),
            pl.BlockSpec((1, d, tf), lambda i, f: (layer, 0, f)),
            pl.BlockSpec((1, d, tf), lambda i, f: (layer, 0, nf + f)),
            pl.BlockSpec((1, tf, d), lambda i, f: (layer, f, 0)),
        ],
        out_specs=pl.BlockSpec((tm, d), lambda i, f: (i, 0)),
        scratch_shapes=[pltpu.VMEM((tm, d), BF16)],
        compiler_params=_cparams("arbitrary", "arbitrary"),
        name="ffn",
    )(x, mod3, g, w_gu_all, w_gu_all, w_down_all)


def _conv_in(x_ref, mod_ref, g_ref, win_ref):
    d = x_ref.shape[1]
    mod = mod_ref[0]
    h = _norm_mod(x_ref[...], g_ref[...], mod[:, d:2 * d], mod[:, 0:d]).astype(BF16)
    bg = _dot(h, win_ref[:, 0:d])
    u = _dot(h, win_ref[:, d:2 * d]) * _dot(h, win_ref[:, 2 * d:3 * d])
    return bg, u


def _conv_out(bg, conv, x_ref, mod_ref, wout_ref, xo_ref):
    d = x_ref.shape[1]
    y = _dot((bg * conv).astype(BF16), wout_ref[...])
    xo_ref[...] = x_ref[...] + mod_ref[0][:, 2 * d:3 * d] * y


def _conv_seq_kernel(x_ref, mod_ref, g_ref, win_ref, wc_ref, wout_ref, st_ref, xo_ref, nst_ref, carry_scr):
    tm = x_ref.shape[0]
    bg, u = _conv_in(x_ref, mod_ref, g_ref, win_ref)

    @pl.when(pl.program_id(1) == 0)
    def _():
        carry_scr[6:8, :] = st_ref[0]

    ue = jnp.concatenate([carry_scr[...], u], axis=0)
    conv = wc_ref[0:1, :] * ue[6:6 + tm] + wc_ref[1:2, :] * ue[7:7 + tm] + wc_ref[2:3, :] * u
    carry_scr[...] = u[tm - 8:tm]
    nst_ref[0] = u[tm - 2:tm]
    _conv_out(bg, conv, x_ref, mod_ref, wout_ref, xo_ref)


def _conv_tok_kernel(x_ref, mod_ref, g_ref, win_ref, wc_ref, wout_ref, p1_ref, p2_ref, xo_ref, u_ref, *, t_seq):
    tm = x_ref.shape[0]
    bg, u = _conv_in(x_ref, mod_ref, g_ref, win_ref)
    row = lax.broadcasted_iota(jnp.int32, (tm, 1), 0) % t_seq
    prev1 = jnp.where(row >= 1, pltpu.roll(u, 1, 0), p1_ref[...])
    prev2 = jnp.where(row >= 2, pltpu.roll(u, 2, 0), p2_ref[...])
    conv = wc_ref[0:1, :] * prev2 + wc_ref[1:2, :] * prev1 + wc_ref[2:3, :] * u
    u_ref[...] = u
    _conv_out(bg, conv, x_ref, mod_ref, wout_ref, xo_ref)


def _conv_mixer(x, mod3, tpb, g, w_in, w_conv, w_out, state, b, t_seq, tm):
    t, d = x.shape
    weights = [_full((1, d)), _full(w_in.shape), _full(w_conv.shape), _full(w_out.shape)]
    if state is None:
        state = jnp.zeros((b, 2, d), F32)
    if t_seq >= tm:
        nt = t_seq // tm
        xo, nst = pl.pallas_call(
            _conv_seq_kernel,
            out_shape=(jax.ShapeDtypeStruct((t, d), F32), jax.ShapeDtypeStruct((b, 2, d), F32)),
            grid=(b, nt),
            in_specs=[pl.BlockSpec((tm, d), lambda i, j: (i * nt + j, 0)),
                      pl.BlockSpec((1, 1, mod3.shape[-1]), lambda i, j: (i, 0, 0)),
                      *weights,
                      pl.BlockSpec((1, 2, d), lambda i, j: (i, 0, 0))],
            out_specs=(pl.BlockSpec((tm, d), lambda i, j: (i * nt + j, 0)),
                       pl.BlockSpec((1, 2, d), lambda i, j: (i, 0, 0))),
            scratch_shapes=[pltpu.VMEM((8, d), F32)],
            compiler_params=_cparams("arbitrary", "arbitrary"),
            name="conv_seq",
        )(x, mod3, g, w_in, w_conv, w_out, state)
        return xo, nst
    assert t == tm and t_seq >= 2
    p1 = jnp.zeros((b, t_seq, d), F32).at[:, 0].set(state[:, 1]).reshape(t, d)
    p2 = jnp.zeros((b, t_seq, d), F32).at[:, 0].set(state[:, 0]).at[:, 1].set(state[:, 1]).reshape(t, d)
    xo, u = pl.pallas_call(
        functools.partial(_conv_tok_kernel, t_seq=t_seq),
        out_shape=(jax.ShapeDtypeStruct((t, d), F32), jax.ShapeDtypeStruct((t, d), F32)),
        grid=(1,),
        in_specs=[_full((t, d)), _full(mod3.shape), *weights, _full((t, d)), _full((t, d))],
        out_specs=(_full((t, d)), _full((t, d))),
        compiler_params=_cparams("arbitrary"),
        name="conv_tok",
    )(x, mod3, g, w_in, w_conv, w_out, p1, p2)
    return xo, u.reshape(b, t_seq, d)[:, t_seq - 2:]


MOE_ROW_UNIT = 128
MOE_BOUNDS_ROWS = 16


def _split_bf16(x):
    hi = x.astype(BF16)
    return hi, (x - hi.astype(F32)).astype(BF16)


def _moe_route_kernel(x_ref, mod_ref, g_ref, wr_ref, br_ref, h_ref, pos_ref, comb_ref, post_ref, bnd_ref, *, blk):
    tm, d = x_ref.shape
    nb = tm // blk
    mod = mod_ref[0]
    h32 = _norm_mod(x_ref[...], g_ref[...], mod[:, 4 * d:5 * d], mod[:, 3 * d:4 * d])
    h_ref[...] = h32.astype(BF16)
    lane = lax.broadcasted_iota(jnp.int32, (1, V7X_LANES), 1)
    h_hi, h_lo = _split_bf16(h32)
    w_hi, w_lo = _split_bf16(wr_ref[...])
    logits = _dot(h_hi, w_hi) + (_dot(h_hi, w_lo) + _dot(h_lo, w_hi)) + br_ref[...]
    logits = jnp.where(lane < N_EXPERTS, logits, NEG_INF)
    m1 = jnp.max(logits, axis=-1, keepdims=True)
    i1 = jnp.min(jnp.where(logits == m1, lane, V7X_LANES), axis=-1, keepdims=True)
    rest = jnp.where(lane == i1, NEG_INF, logits)
    m2 = jnp.max(rest, axis=-1, keepdims=True)
    i2 = jnp.min(jnp.where(rest == m2, lane, V7X_LANES), axis=-1, keepdims=True)
    e2 = jnp.exp(m2 - m1)
    den = 1.0 + e2
    comb_ref[...] = jnp.where(lane == i1, 1.0 / den, jnp.where(lane == i2, e2 / den, 0.0))
    mask = jnp.where(lane == i1, 1.0, jnp.where(lane == i2, 1.0, 0.0))
    r = lax.broadcasted_iota(jnp.int32, (blk, blk), 0)
    c = lax.broadcasted_iota(jnp.int32, (blk, blk), 1)
    lower = jnp.where(c < r, 1.0, 0.0).astype(BF16)
    upper = jnp.where(r < c, 1.0, 0.0).astype(BF16)
    carry = jnp.zeros((1, V7X_LANES), F32)
    bnd_ref[0] = jnp.zeros((MOE_BOUNDS_ROWS, V7X_LANES), F32)
    for k in range(nb):
        mb = mask[k * blk:(k + 1) * blk]
        pos_ref[k * blk:(k + 1) * blk, :] = jnp.where(mb > 0.5, _dot(lower, mb.astype(BF16)) + carry, -1.0)
        carry = carry + jnp.sum(mb, axis=0, keepdims=True)
        bnd_ref[0, k + 1:k + 2, :] = carry
    eye = jnp.where(lax.broadcasted_iota(jnp.int32, (V7X_LANES, V7X_LANES), 0)
                    == lax.broadcasted_iota(jnp.int32, (V7X_LANES, V7X_LANES), 1), 1.0, 0.0).astype(BF16)
    mask_t = _dot_nt(eye, mask.astype(BF16))[0:N_EXPERTS]
    carry_t = jnp.zeros((N_EXPERTS, 1), F32)
    for k in range(nb):
        mb = mask_t[:, k * blk:(k + 1) * blk]
        post_ref[0, k] = jnp.where(mb > 0.5, _dot(mb.astype(BF16), upper) + carry_t, -1.0)
        carry_t = carry_t + jnp.sum(mb, axis=1, keepdims=True)


def _moe_route(x, mod3, tpb, g, w_router, b_router, tm, blk):
    t, d = x.shape
    ne = w_router.shape[1]
    nt = t // tm
    nb = tm // blk
    assert nb + 1 <= MOE_BOUNDS_ROWS
    wr = jnp.pad(w_router, ((0, 0), (0, V7X_LANES - ne)))
    br = jnp.pad(b_router, (0, V7X_LANES - ne)).reshape(1, V7X_LANES)
    tok = lambda w: pl.BlockSpec((tm, w), lambda i: (i, 0))
    return pl.pallas_call(
        functools.partial(_moe_route_kernel, blk=blk),
        out_shape=(
            jax.ShapeDtypeStruct((t, d), BF16),
            jax.ShapeDtypeStruct((t, V7X_LANES), F32),
            jax.ShapeDtypeStruct((t, V7X_LANES), F32),
            jax.ShapeDtypeStruct((nt, nb, N_EXPERTS, blk), F32),
            jax.ShapeDtypeStruct((nt, MOE_BOUNDS_ROWS, V7X_LANES), F32),
        ),
        grid=(nt,),
        in_specs=[tok(d), _mod_spec(mod3, tpb), _full((1, d)), _full((d, V7X_LANES)), _full((1, V7X_LANES))],
        out_specs=(tok(d), tok(V7X_LANES), tok(V7X_LANES),
                   pl.BlockSpec((1, nb, N_EXPERTS, blk), lambda i: (i, 0, 0, 0)),
                   pl.BlockSpec((1, MOE_BOUNDS_ROWS, V7X_LANES), lambda i: (i, 0, 0))),
        compiler_params=_cparams("arbitrary"),
        name="moe_route",
    )(x, mod3, g, wr, br)


def _moe_expert_kernel(bnd_ref, h_ref, pos_ref, comb_ref, post_ref, wg_ref, wu_ref, wd_ref, o_ref,
                       hc_scr, acc_scr, posc_scr, combc_scr, *, blk):
    tm, d = h_ref.shape
    nb = tm // blk
    unit = MOE_ROW_UNIT
    i = pl.program_id(0)
    e = pl.program_id(1)
    f = pl.program_id(2)
    nf = pl.num_programs(2)
    base = (i * N_EXPERTS + e) * MOE_BOUNDS_ROWS
    cnt = bnd_ref[base + nb]
    n_units = (cnt + unit - 1) // unit

    def span(u):
        start = u * unit
        stop = jnp.minimum(start + unit, cnt)
        lo = jnp.int32(0)
        hi = jnp.int32(0)
        for k in range(1, nb):
            bk = bnd_ref[base + k]
            lo = lo + (bk <= start).astype(jnp.int32)
            hi = hi + (bk < stop).astype(jnp.int32)
        return lo, hi + 1

    @pl.when((e == 0) & (f == 0))
    def _():
        o_ref[...] = jnp.zeros(o_ref.shape, F32)

    @pl.when(f == 0)
    def _():
        def per_unit(u, _):
            lo, hi = span(u)
            r0 = pl.multiple_of(u * unit, unit)
            want = (r0 + lax.broadcasted_iota(jnp.int32, (unit, 1), 0)).astype(F32)
            acc_scr[pl.ds(r0, unit), :] = jnp.zeros((unit, d), F32)

            def per_blk(k, _):
                t0 = pl.multiple_of(k * blk, blk)
                onehot = jnp.where(post_ref[0, k, pl.ds(e, 1), :] == want, 1.0, 0.0).astype(BF16)
                acc_scr[pl.ds(r0, unit), :] += _dot(onehot, h_ref[pl.ds(t0, blk), :])
                return 0

            lax.fori_loop(lo, hi, per_blk, 0)
            hc_scr[pl.ds(r0, unit), :] = acc_scr[pl.ds(r0, unit), :].astype(BF16)
            acc_scr[pl.ds(r0, unit), :] = jnp.zeros((unit, d), F32)
            return 0

        lax.fori_loop(0, n_units, per_unit, 0)

    def ffn_rows(r0, m):
        hc = hc_scr[pl.ds(r0, m), :]
        a = _dot(hc, wg_ref[0, 0].astype(BF16))
        act = (a * _sigmoid(a) * _dot(hc, wu_ref[0, 0].astype(BF16))).astype(BF16)
        acc_scr[pl.ds(r0, m), :] += _dot(act, wd_ref[0, 0].astype(BF16))

    n_quads = n_units // 4
    rem = n_units - 4 * n_quads

    def quad(k, _):
        ffn_rows(pl.multiple_of(k * 4 * unit, 4 * unit), 4 * unit)
        return 0

    lax.fori_loop(0, n_quads, quad, 0)
    r_tail = pl.multiple_of(n_quads * 4 * unit, 4 * unit)

    @pl.when(rem >= 2)
    def _():
        ffn_rows(r_tail, 2 * unit)

    @pl.when(rem % 2 == 1)
    def _():
        ffn_rows(pl.multiple_of(r_tail + (rem // 2) * 2 * unit, unit), unit)

    @pl.when(f == nf - 1)
    def _():
        lane = lax.broadcasted_iota(jnp.int32, (1, V7X_LANES), 1)
        posc_scr[...] = jnp.sum(jnp.where(lane == e, pos_ref[...], 0.0), axis=-1, keepdims=True)
        combc_scr[...] = jnp.sum(jnp.where(lane == e, comb_ref[...], 0.0), axis=-1, keepdims=True)

        def per_unit(u, _):
            lo, hi = span(u)
            r0 = pl.multiple_of(u * unit, unit)
            want = (r0 + lax.broadcasted_iota(jnp.int32, (1, unit), 1)).astype(F32)
            y = acc_scr[pl.ds(r0, unit), :].astype(BF16)

            def per_blk(k, _):
                t0 = pl.multiple_of(k * blk, blk)
                onehot = jnp.where(posc_scr[pl.ds(t0, blk), :] == want, 1.0, 0.0).astype(BF16)
                o_ref[pl.ds(t0, blk), :] += combc_scr[pl.ds(t0, blk), :] * _dot(onehot, y)
                return 0

            lax.fori_loop(lo, hi, per_blk, 0)
            return 0

        lax.fori_loop(0, n_units, per_unit, 0)


def _moe_experts(h, pos, comb, post, bounds, w_gu_all, w_down_all, layer, tm, tf, blk):
    t, d = h.shape
    ne, ff = w_down_all.shape[1], w_down_all.shape[2]
    nf = ff // tf
    nt = t // tm
    nb = tm // blk
    bnd = jnp.transpose(bounds[:, :, :ne], (0, 2, 1)).astype(jnp.int32).reshape(nt * ne * MOE_BOUNDS_ROWS)
    once = pl.Buffered(1)
    tok = lambda w: pl.BlockSpec((tm, w), lambda i, e, f, b_: (i, 0), pipeline_mode=once)
    grid_spec = pltpu.PrefetchScalarGridSpec(
        num_scalar_prefetch=1,
        grid=(nt, ne, nf),
        in_specs=[
            tok(d), tok(V7X_LANES), tok(V7X_LANES),
            pl.BlockSpec((1, nb, ne, blk), lambda i, e, f, b_: (i, 0, 0, 0)),
            pl.BlockSpec((1, 1, d, tf), lambda i, e, f, b_: (layer, e, 0, f)),
            pl.BlockSpec((1, 1, d, tf), lambda i, e, f, b_: (layer, e, 0, nf + f)),
            pl.BlockSpec((1, 1, tf, d), lambda i, e, f, b_: (layer, e, f, 0)),
        ],
        out_specs=pl.BlockSpec((tm, d), lambda i, e, f, b_: (i, 0), pipeline_mode=once),
        scratch_shapes=[
            pltpu.VMEM((tm, d), BF16),
            pltpu.VMEM((tm, d), F32),
            pltpu.VMEM((tm, 1), F32),
            pltpu.VMEM((tm, 1), F32),
        ],
    )
    return pl.pallas_call(
        functools.partial(_moe_expert_kernel, blk=blk),
        out_shape=jax.ShapeDtypeStruct((t, d), F32),
        grid_spec=grid_spec,
        compiler_params=_cparams("arbitrary", "arbitrary", "arbitrary"),
        name="moe_experts",
    )(bnd, h, pos, comb, post, w_gu_all, w_gu_all, w_down_all)


def _residual_kernel(x_ref, f_ref, mod_ref, g_ref, o_ref, *, final_norm):
    d = x_ref.shape[1]
    x = x_ref[...] + mod_ref[0][:, 5 * d:6 * d] * f_ref[...]
    if final_norm:
        x = (x * lax.rsqrt(jnp.mean(x * x, axis=-1, keepdims=True) + EPS)) * g_ref[...]
    o_ref[...] = x


def _residual(x, f, mod3, tpb, g_final, tm, final_norm):
    t, d = x.shape
    tok = pl.BlockSpec((tm, d), lambda i: (i, 0))
    return pl.pallas_call(
        functools.partial(_residual_kernel, final_norm=final_norm),
        out_shape=jax.ShapeDtypeStruct((t, d), F32),
        grid=(t // tm,),
        in_specs=[tok, tok, _mod_spec(mod3, tpb), _full((1, d))],
        out_specs=tok,
        compiler_params=_cparams("arbitrary"),
        name="residual",
    )(x, f, mod3, g_final)


def _moe(x, mod3_of, g, w_router, b_router, w_gu_all, w_down_all, layer, g_final, tm, final_norm):
    blk = min(256, tm)
    h, pos, comb, post, bounds = _moe_route(x, *mod3_of(tm), g, w_router, b_router, tm, blk)
    f = _moe_experts(h, pos, comb, post, bounds, w_gu_all, w_down_all, layer, tm, 512, blk)
    tm_res = min(1024, tm)
    return _residual(x, f, *mod3_of(tm_res), g_final, tm_res, final_norm)


def _final_norm_kernel(x_ref, g_ref, o_ref):
    x = x_ref[...]
    o_ref[...] = (x * lax.rsqrt(jnp.mean(x * x, axis=-1, keepdims=True) + EPS)) * g_ref[...]


def _final_norm(x, g, tm):
    t, d = x.shape
    return pl.pallas_call(
        _final_norm_kernel,
        out_shape=jax.ShapeDtypeStruct((t, d), F32),
        grid=(t // tm,),
        in_specs=[pl.BlockSpec((tm, d), lambda i: (i, 0)), pl.BlockSpec((1, d), lambda i: (0, 0))],
        out_specs=pl.BlockSpec((tm, d), lambda i: (i, 0)),
        compiler_params=_cparams("arbitrary"),
        name="final_norm",
    )(x, g)


def _nsa_layer_weights(w_in, w_out, w_ck, pe_ck, w_cv, pe_cv):
    n_main = N_HEADS * HEAD_DIM + 6 * KV_WIDTH
    w_gate = jnp.pad(w_in[:, n_main:], ((0, 0), (0, V7X_LANES - N_HEADS * N_BRANCH)))
    return {"w_main": w_in[:, :n_main].astype(BF16), "w_gate": w_gate.astype(BF16),
            "w_out": w_out.astype(BF16), "cw": _compress_weights(w_ck, pe_ck, w_cv, pe_cv)}


def _rows_from_feature_major(xt):
    b, _, s = xt.shape
    return jnp.transpose(xt.reshape(b, 2, KV_HEADS, HEAD_DIM, s), (0, 4, 1, 2, 3))


def _feature_major_from_rows(x, lead):
    n = len(lead)
    perm = tuple(range(n)) + (n + 1, n + 2, n + 3, n)
    xt = jnp.transpose(x, perm)
    return xt.reshape(lead + (2 * KV_WIDTH, x.shape[n]))


def _nsa_prompt_layer(x, mod, g, lw, b, s_len):
    tm = min(512, s_len)
    mod3 = mod.reshape(b, 1, mod.shape[-1])
    rope_t = _rope_tables(jnp.arange(s_len), V7X_LANES)
    q, gates, cmp_rows, cmpt, selt, wint = _nsa_proj(x, mod3, s_len // tm, g, lw["w_main"], lw["w_gate"], rope_t,
                                                     s_len, tm, True)
    ck, cv = _compress(cmp_rows.reshape(b, s_len, 2 * KV_WIDTH), lw["cw"])
    o = _attn_prompt(q, gates, ck, cv, selt, wint, tq=V7X_LANES)
    x = _out_proj(o, lw["w_out"], x, mod3, s_len // tm, tm)
    return x, cmpt, selt, wint


def _nsa_decode_layer(x, mod3, g, lw, b, t_seq, caches, layer):
    t = x.shape[0]
    page_table = caches["page_table"]
    past_len = page_table.shape[1] * caches["cmp"].shape[3]
    rope_t = _rope_tables(past_len + jnp.arange(t) % t_seq, V7X_LANES)
    q, gates, cmp_rows, sel_rows, win_rows = _nsa_proj(x, mod3, 1, g, lw["w_main"], lw["w_gate"], rope_t, t, t, False)
    pad3 = lambda a: jnp.pad(a.reshape(b, t_seq, a.shape[-1]), ((0, 0), (0, DECODE_TOK_PAD - t_seq), (0, 0)))
    o = _attn_decode(page_table, pad3(q), pad3(gates), pad3(sel_rows), pad3(win_rows),
                     caches["cmp"], caches["sel"], caches["win"], layer, lw["cw"])
    o = o[:, :t_seq].reshape(t, o.shape[-1]).astype(BF16)
    x = _out_proj(o, lw["w_out"], x, mod3, 1, t)
    return x, cmp_rows, sel_rows, win_rows


def _trunk(x3, mod, caches, p):
    b, t_seq, d = x3.shape
    t = b * t_seq
    kvw = 2 * KV_WIDTH
    prompt = caches is None
    x = x3.reshape(t, d)
    new_cmp, new_sel, new_win, new_conv = [], [], [], []
    depth = mod.shape[0]
    g_final = p["g_norm_final"].reshape(1, d)
    finished = False
    for i in range(depth):
        j = i // 2
        if prompt:
            mod_i = mod[i].reshape(b, 1, 6 * d)
            mod3_of = lambda tm, mod_i=mod_i: (mod_i, t_seq // tm)
            tm_small, tm_big, tm_moe = min(512, t_seq), min(1024, t_seq), min(2048, t_seq)
        else:
            mod_i = jnp.repeat(mod[i], t_seq, axis=0).reshape(1, t, 6 * d)
            mod3_of = lambda tm, mod_i=mod_i: (mod_i, 1)
            tm_small = tm_big = tm_moe = t
        g_mix = p["g_norm_mix"][i].reshape(1, d)
        g_ffn = p["g_norm_ffn"][i].reshape(1, d)
        if i % 2 == 0:
            lw = _nsa_layer_weights(p["w_nsa_in"][j], p["w_nsa_out"][j], p["w_cmp_k"][j], p["pe_cmp_k"][j],
                                    p["w_cmp_v"][j], p["pe_cmp_v"][j])
            if prompt:
                x, cmpt, selt, wint = _nsa_prompt_layer(x, mod[i], g_mix, lw, b, t_seq)
                new_cmp.append(_rows_from_feature_major(cmpt))
                new_sel.append(_rows_from_feature_major(selt))
                new_win.append(_rows_from_feature_major(wint[:, :, t_seq - min(WINDOW, t_seq):]))
            else:
                x, c_rows, s_rows, w_rows = _nsa_decode_layer(x, mod_i, g_mix, lw, b, t_seq, caches, j)
                rows5 = lambda r: r.reshape(b, t_seq, 2, KV_HEADS, HEAD_DIM)
                new_cmp.append(rows5(c_rows))
                new_sel.append(rows5(s_rows))
                new_win.append(jnp.concatenate([caches["win_rows"][j], rows5(w_rows)], axis=1)[:, t_seq:])
            x = _ffn(x, *mod3_of(tm_big), g_ffn, p["w_ff_gu"], p["w_ff_down"], j, tm_big, 512)
        else:
            state = None if prompt else caches["conv"][j]
            x, st = _conv_mixer(x, *mod3_of(tm_small), g_mix, p["w_conv_in"][j].astype(BF16), p["w_conv"][j],
                                p["w_conv_out"][j].astype(BF16), state, b, t_seq, tm_small)
            new_conv.append(st)
            finished = i == depth - 1
            x = _moe(x, mod3_of, g_ffn, p["w_router"][j], p["b_router"][j], p["w_moe_gu"], p["w_moe_down"], j,
                     g_final, tm_moe, finished)
    y = x if finished else _final_norm(x, g_final, tm_big)
    return (y.reshape(b, t_seq, d), jnp.stack(new_cmp), jnp.stack(new_sel), jnp.stack(new_win), jnp.stack(new_conv))


def kernel(x_prompt, x_sample, cache_cmp_kv, cache_sel_kv, cache_win_kv, state_conv, page_table, c_prompt, c_sample,
           w_ada, b_ada, g_norm_mix, g_norm_ffn, g_norm_final, w_nsa_in, w_nsa_out, w_cmp_k, pe_cmp_k, w_cmp_v,
           pe_cmp_v, w_conv_in, w_conv, w_conv_out, w_ff_gu, w_ff_down, w_router, b_router, w_moe_gu, w_moe_down):
    p = {"g_norm_mix": g_norm_mix, "g_norm_ffn": g_norm_ffn, "g_norm_final": g_norm_final,
         "w_nsa_in": w_nsa_in, "w_nsa_out": w_nsa_out, "w_cmp_k": w_cmp_k, "pe_cmp_k": pe_cmp_k, "w_cmp_v": w_cmp_v,
         "pe_cmp_v": pe_cmp_v, "w_conv_in": w_conv_in, "w_conv": w_conv, "w_conv_out": w_conv_out,
         "w_ff_gu": w_ff_gu, "w_ff_down": w_ff_down, "w_router": w_router, "b_router": b_router,
         "w_moe_gu": w_moe_gu, "w_moe_down": w_moe_down}
    nb = x_prompt.shape[0]
    mod = _ada(jnp.concatenate([c_prompt, c_sample], axis=0), w_ada, b_ada)
    y_p, cmp_p, sel_p, win_p, conv_p = _trunk(x_prompt, mod[:, :nb], None, p)
    caches = {"cmp": _feature_major_from_rows(cache_cmp_kv, cache_cmp_kv.shape[:2]),
              "sel": _feature_major_from_rows(cache_sel_kv, cache_sel_kv.shape[:2]),
              "win": _feature_major_from_rows(cache_win_kv, cache_win_kv.shape[:2]),
              "win_rows": cache_win_kv, "conv": state_conv, "page_table": page_table}
    y_s, cmp_s, sel_s, win_s, conv_s = _trunk(x_sample, mod[:, nb:], caches, p)
    return (y_p, y_s, cmp_p, cmp_s, sel_p, sel_s, win_p, win_s, conv_p, conv_s)
```

```python
import functools

import jax
import jax.numpy as jnp
from jax import lax
from jax.experimental import pallas as pl
from jax.experimental.pallas import tpu as pltpu

F32 = jnp.float32
BF16 = jnp.bfloat16

N_HEADS = 16
HEAD_DIM = 64
KV_HEADS = 4
HEADS_PER_KV = N_HEADS // KV_HEADS
KV_WIDTH = KV_HEADS * HEAD_DIM
ROT_DIM = HEAD_DIM // 4
ROPE_THETA = 500000.0
CMP_BLOCK = 32
SEL_BLOCK = 64
TOP_N = 8
WINDOW = 512
N_BRANCH = 3
FORCE_BONUS = float(HEADS_PER_KV + 1)
N_EXPERTS = 8
EPS = 1e-6
NEG_INF = float("-inf")

V7X_LANES = 128
V7X_SUBLANES = 8
V7X_VMEM_LIMIT_BYTES = 58 * 1024 * 1024


def _cparams(*sem):
    return pltpu.CompilerParams(dimension_semantics=sem, vmem_limit_bytes=V7X_VMEM_LIMIT_BYTES)


def _dot(a, b):
    return jnp.dot(a, b, preferred_element_type=F32)


def _dot_nt(a, b):
    return lax.dot_general(a, b, (((1,), (1,)), ((), ())), preferred_element_type=F32)


def _norm_mod(x, g, sc, sh):
    y = x * lax.rsqrt(jnp.mean(x * x, axis=-1, keepdims=True) + EPS)
    return (y * g) * (1.0 + sc) + sh


def _sigmoid(x):
    return 1.0 / (1.0 + jnp.exp(-x))


def _mod_spec(mod3, tiles_per_block):
    _, r, w = mod3.shape
    return pl.BlockSpec((1, r, w), lambda i, *_: (i // tiles_per_block, 0, 0))


def _full(shape):
    return pl.BlockSpec(shape, lambda *_: (0,) * len(shape))


def _ada_kernel(c_ref, w_ref, b_ref, o_ref):
    c = c_ref[...]
    cs = (c * _sigmoid(c)).astype(BF16)
    o_ref[0] = _dot(cs, w_ref[0].astype(BF16)) + b_ref[0]


def _ada(c, w_ada, b_ada):
    depth, d, n = w_ada.shape
    nb = c.shape[0]
    tn = 1536
    return pl.pallas_call(
        _ada_kernel,
        out_shape=jax.ShapeDtypeStruct((depth, nb, n), F32),
        grid=(depth, n // tn),
        in_specs=[
            pl.BlockSpec((nb, d), lambda i, j: (0, 0)),
            pl.BlockSpec((1, d, tn), lambda i, j: (i, 0, j)),
            pl.BlockSpec((1, 1, tn), lambda i, j: (i, 0, j)),
        ],
        out_specs=pl.BlockSpec((1, nb, tn), lambda i, j: (i, 0, j)),
        compiler_params=_cparams("arbitrary", "arbitrary"),
        name="ada",
    )(c, w_ada, b_ada.reshape(depth, 1, n))


def _rope_tables(pos, width):
    half = ROT_DIM // 2
    inv = 1.0 / (ROPE_THETA ** (jnp.arange(half, dtype=F32) * (2.0 / ROT_DIM)))
    ang = pos.astype(F32)[:, None] * inv[None, :]
    cos, sin = jnp.cos(ang), jnp.sin(ang)
    n = pos.shape[0]
    z8 = jnp.zeros((n, half), F32)
    rest0 = jnp.zeros((n, HEAD_DIM - ROT_DIM), F32)
    c = jnp.concatenate([cos, cos, rest0 + 1.0], axis=1)
    a = jnp.concatenate([-sin, z8, rest0], axis=1)
    b = jnp.concatenate([z8, sin, rest0], axis=1)
    rep = width // HEAD_DIM
    return tuple(jnp.tile(t, (1, rep)) for t in (c, a, b))


def _rope(v, c, a, b):
    w = v.shape[-1]
    half = ROT_DIM // 2
    return v * c + pltpu.roll(v, w - half, 1) * a + pltpu.roll(v, half, 1) * b


def _widen(t, width):
    return jnp.concatenate([t] * (width // t.shape[-1]), axis=1)


def _nsa_proj_kernel(x_ref, mod_ref, g_ref, w_ref, wg_ref, rc_ref, ra_ref, rb_ref, q_ref, gate_ref, *kv_refs,
                     feature_major):
    d = x_ref.shape[1]
    mod = mod_ref[0]
    h = _norm_mod(x_ref[...], g_ref[...], mod[:, d:2 * d], mod[:, 0:d]).astype(BF16)
    kvw = 2 * KV_WIDTH
    c1, a1, b1 = rc_ref[...], ra_ref[...], rb_ref[...]
    c2, a2, b2 = (_widen(t, kvw) for t in (c1, a1, b1))
    scale = HEAD_DIM ** -0.5
    hq = N_HEADS * HEAD_DIM
    for j in range(hq // kvw):
        p = _dot(h, w_ref[:, j * kvw:(j + 1) * kvw])
        q_ref[:, j * kvw:(j + 1) * kvw] = (_rope(p, c2, a2, b2) * scale).astype(BF16)
    gate_ref[...] = _sigmoid(_dot(h, wg_ref[...]))
    one = jnp.ones_like(c1)
    zero = jnp.zeros_like(c1)
    ck = jnp.concatenate([_widen(c1, KV_WIDTH), _widen(one, KV_WIDTH)], axis=1)
    ak = jnp.concatenate([_widen(a1, KV_WIDTH), _widen(zero, KV_WIDTH)], axis=1)
    bk = jnp.concatenate([_widen(b1, KV_WIDTH), _widen(zero, KV_WIDTH)], axis=1)
    cmp_rows = _dot(h, w_ref[:, hq:hq + kvw])
    sel_rows = _rope(_dot(h, w_ref[:, hq + kvw:hq + 2 * kvw]), ck, ak, bk)
    win_rows = _rope(_dot(h, w_ref[:, hq + 2 * kvw:hq + 3 * kvw]), ck, ak, bk)
    if feature_major:
        cmp_ref, cmpt_ref, selt_ref, wint_ref = kv_refs
        cmp_ref[...] = cmp_rows
        cmpt_ref[0] = cmp_rows.T
        selt_ref[0] = sel_rows.T
        wint_ref[0] = win_rows.T
    else:
        cmp_ref, sel_ref, win_ref = kv_refs
        cmp_ref[...] = cmp_rows
        sel_ref[...] = sel_rows
        win_ref[...] = win_rows


def _nsa_proj(x, mod3, tpb, g, w_main, w_gate, rope_t, rows_per_seq, tm, feature_major):
    t, d = x.shape
    nw = w_main.shape[1]
    kvw = 2 * KV_WIDTH
    hq = N_HEADS * HEAD_DIM
    nrt = rows_per_seq // tm
    rspec = pl.BlockSpec((tm, V7X_LANES), lambda i: (i % nrt, 0))
    tok = lambda w: pl.BlockSpec((tm, w), lambda i: (i, 0))
    out_shape = [jax.ShapeDtypeStruct((t, hq), BF16), jax.ShapeDtypeStruct((t, V7X_LANES), F32)]
    out_specs = [tok(hq), tok(V7X_LANES)]
    if feature_major:
        b = t // rows_per_seq
        fm = pl.BlockSpec((1, kvw, tm), lambda i: (i // nrt, 0, i % nrt))
        out_shape += [jax.ShapeDtypeStruct((t, kvw), F32)] + [jax.ShapeDtypeStruct((b, kvw, rows_per_seq), F32)] * 3
        out_specs += [tok(kvw), fm, fm, fm]
    else:
        out_shape += [jax.ShapeDtypeStruct((t, kvw), F32)] * 3
        out_specs += [tok(kvw)] * 3
    return pl.pallas_call(
        functools.partial(_nsa_proj_kernel, feature_major=feature_major),
        out_shape=tuple(out_shape),
        grid=(t // tm,),
        in_specs=[tok(d), _mod_spec(mod3, tpb), _full((1, d)), _full((d, nw)), _full((d, V7X_LANES)),
                  rspec, rspec, rspec],
        out_specs=tuple(out_specs),
        compiler_params=_cparams("arbitrary"),
        name="nsa_proj",
    )(x, mod3, g, w_main, w_gate, *rope_t)


def _compress_rows(load_rows, bdk_ref, bdv_ref, pek_ref, pev_ref, nc):
    half = nc // 2
    n_chunks = 2 * KV_WIDTH // V7X_LANES
    outs = []
    for c in range(n_chunks):
        pe_ref, w_ref = (pek_ref, bdk_ref) if c < n_chunks // 2 else (pev_ref, bdv_ref)
        cols = []
        for l in range(CMP_BLOCK):
            xb = jnp.concatenate([load_rows(c, l, half), load_rows(c, CMP_BLOCK + l, half)], axis=0)
            cols.append((xb + pe_ref[l:l + 1, :]).astype(BF16))
        outs.append(_dot(jnp.concatenate(cols, axis=1), w_ref[...]))
    return (jnp.concatenate(outs[0:n_chunks // 2], axis=1), jnp.concatenate(outs[n_chunks // 2:], axis=1))


def _compress_kernel(r0_ref, r1_ref, r2_ref, r3_ref, bdk_ref, bdv_ref, pek_ref, pev_ref, rc_ref, ra_ref, rb_ref,
                     ck_ref, cv_ref):
    nc = ck_ref.shape[1]

    chunk_refs = (r0_ref, r1_ref, r2_ref, r3_ref)

    def load_rows(c, start, n):
        return chunk_refs[c][0, pl.ds(start, n, stride=2 * CMP_BLOCK), :]

    acck, accv = _compress_rows(load_rows, bdk_ref, bdv_ref, pek_ref, pev_ref, nc)
    ck_ref[0] = _rope(acck, rc_ref[...], ra_ref[...], rb_ref[...]).astype(BF16)
    cv_ref[0] = accv.astype(BF16)


def _cmp_order_end_pos(nc):
    n = jnp.arange(nc)
    half = nc // 2
    blk = jnp.where(n < half, 2 * n, 2 * (n - half) + 1)
    return blk * CMP_BLOCK + (CMP_BLOCK - 1)


def _cmp_end_row(nc):
    half = nc // 2
    col = lax.broadcasted_iota(jnp.int32, (1, nc), 1)
    return jnp.where(col < half, 2 * CMP_BLOCK * col + (CMP_BLOCK - 1),
                     2 * CMP_BLOCK * (col - half) + (2 * CMP_BLOCK - 1))


def _compress(rows3, cw):
    b, tk, w = rows3.shape
    nc = tk // CMP_BLOCK
    assert nc % 2 == 0 and nc * CMP_BLOCK == tk
    rt = _rope_tables(_cmp_order_end_pos(nc), KV_WIDTH)
    return pl.pallas_call(
        _compress_kernel,
        out_shape=(jax.ShapeDtypeStruct((b, nc, KV_WIDTH), BF16),) * 2,
        grid=(b,),
        in_specs=[
            *[pl.BlockSpec((1, tk, V7X_LANES), functools.partial(lambda i, c: (i, 0, c), c=c))
              for c in range(w // V7X_LANES)],
            _full(cw["bdk"].shape), _full(cw["bdv"].shape), _full(cw["pek"].shape), _full(cw["pev"].shape),
            _full(rt[0].shape), _full(rt[1].shape), _full(rt[2].shape),
        ],
        out_specs=(pl.BlockSpec((1, nc, KV_WIDTH), lambda i: (i, 0, 0)),) * 2,
        compiler_params=_cparams("arbitrary"),
        name="compress",
    )(rows3, rows3, rows3, rows3, cw["bdk"], cw["bdv"], cw["pek"], cw["pev"], *rt)


def _compress_weights(w_ck, pe_ck, w_cv, pe_cv):
    pair = V7X_LANES // HEAD_DIM
    eye = jnp.eye(pair, dtype=F32)

    def bd(w):
        return jnp.einsum("gh,lde->lgdhe", eye, w).reshape(CMP_BLOCK * V7X_LANES, V7X_LANES).astype(BF16)

    return {"bdk": bd(w_ck), "bdv": bd(w_cv),
            "pek": jnp.tile(pe_ck, (1, pair)), "pev": jnp.tile(pe_cv, (1, pair))}


def _softmax_rows(s):
    m = jnp.max(s, axis=-1, keepdims=True)
    m = jnp.where(m == NEG_INF, 0.0, m)
    e = jnp.exp(s - m)
    return e / jnp.maximum(jnp.sum(e, axis=-1, keepdims=True), 1e-30)


def _masked_softmax(s, mask):
    return _softmax_rows(jnp.where(mask, s, NEG_INF))


def _select_blocks(imp, t_col, ns):
    blk = lax.broadcasted_iota(jnp.int32, (1, ns), 1)
    cur = t_col // SEL_BLOCK
    forced = jnp.where(blk == 0, 1.0, jnp.where(blk == cur, 1.0, 0.0))
    score = jnp.where(blk <= cur, imp + FORCE_BONUS * forced, NEG_INF)
    rank = jnp.zeros(score.shape, F32)
    for i in range(ns):
        ci = score[:, i:i + 1]
        later = jnp.where(blk > i, 1.0, 0.0)
        rank = rank + jnp.where(ci > score, 1.0, jnp.where(ci == score, later, 0.0))
    return jnp.where(rank < float(min(TOP_N, ns)), 1.0, 0.0)


def _select_blocks_t(imp_t, t_row, ns):
    blk = lax.broadcasted_iota(jnp.int32, (ns, 1), 0)
    cur = t_row // SEL_BLOCK
    forced = jnp.where(blk == 0, 1.0, jnp.where(blk == cur, 1.0, 0.0))
    score = jnp.where(blk <= cur, imp_t + FORCE_BONUS * forced, NEG_INF)
    rank = jnp.zeros(score.shape, F32)
    for i in range(ns):
        ci = score[i:i + 1, :]
        later = jnp.where(blk > i, 1.0, 0.0)
        rank = rank + jnp.where(ci > score, 1.0, jnp.where(ci == score, later, 0.0))
    return jnp.where(rank < float(min(TOP_N, ns)), 1.0, 0.0)


def _stack_heads(x, g):
    base = g * HEADS_PER_KV * HEAD_DIM
    return jnp.concatenate([x[:, base + r * HEAD_DIM: base + (r + 1) * HEAD_DIM] for r in range(HEADS_PER_KV)],
                           axis=0)


def _gate_cols(gates, g, branch):
    cols = [(g * HEADS_PER_KV + r) * N_BRANCH + branch for r in range(HEADS_PER_KV)]
    return jnp.concatenate([gates[:, c:c + 1] for c in cols], axis=0)


def _pad_to(x, rows, cols):
    r, c = x.shape
    if c < cols:
        x = jnp.concatenate([x, jnp.zeros((r, cols - c), x.dtype)], axis=1)
    if r < rows:
        x = jnp.concatenate([x, jnp.zeros((rows - r, x.shape[1]), x.dtype)], axis=0)
    return x


WIN_CHUNK = V7X_LANES


def _attn_prompt_kernel(q_ref, gate_ref, ck_ref, cv_ref, selt_ref, wint_ref, o_ref,
                        ks_scr, vs_scr, kw_scr, vw_scr, bias_scr, *, tk):
    tq = q_ref.shape[0]
    s_len = selt_ref.shape[2]
    nc = ck_ref.shape[1]
    half = nc // 2
    ns = s_len // SEL_BLOCK
    qi = pl.program_id(1)

    @pl.when(qi == 0)
    def _():
        for c in range(s_len // tk):
            ks_scr[c] = selt_ref[0, 0:KV_WIDTH, c * tk:(c + 1) * tk].astype(BF16)
            vs_scr[c] = selt_ref[0, KV_WIDTH:2 * KV_WIDTH, c * tk:(c + 1) * tk].astype(BF16)
        for c in range(s_len // WIN_CHUNK):
            kw_scr[c] = wint_ref[0, 0:KV_WIDTH, c * WIN_CHUNK:(c + 1) * WIN_CHUNK].astype(BF16)
            vw_scr[c] = wint_ref[0, KV_WIDTH:2 * KV_WIDTH, c * WIN_CHUNK:(c + 1) * WIN_CHUNK].astype(BF16)

    q0 = qi * tq
    t_tile = q0 + lax.broadcasted_iota(jnp.int32, (tq, 1), 0)
    t_row = q0 + lax.broadcasted_iota(jnp.int32, (1, tq), 1)
    t_col4 = jnp.concatenate([t_tile] * HEADS_PER_KV, axis=0)
    q_all = q_ref[...]
    gates = gate_ref[...]
    c_end = _cmp_end_row(nc)
    n_chunks = (q0 + tq + tk - 1) // tk

    wlen = min(WINDOW + tq, s_len)
    n_wc = wlen // WIN_CHUNK
    w0 = jnp.clip(q0 + tq - wlen, 0, s_len - wlen)
    wc0 = w0 // WIN_CHUNK
    wpos = w0 + lax.broadcasted_iota(jnp.int32, (1, wlen), 1)
    wbias = jnp.where(wpos <= t_tile, jnp.where(wpos >= t_tile - WINDOW, 0.0, NEG_INF), NEG_INF)
    wbias4 = jnp.concatenate([wbias] * HEADS_PER_KV, axis=0)

    for g in range(KV_HEADS):
        rows_g = slice(g * HEAD_DIM, (g + 1) * HEAD_DIM)
        q4 = _stack_heads(q_all, g)
        p_c = _masked_softmax(_dot_nt(q4, ck_ref[0, :, rows_g]), c_end <= t_col4)
        o_c = _dot(p_c.astype(BF16), cv_ref[0, :, rows_g])
        psum = p_c[0:tq]
        for r in range(1, HEADS_PER_KV):
            psum = psum + p_c[r * tq:(r + 1) * tq]
        psum_t = _pad_to(psum, tq, V7X_LANES).T
        imp_t = psum_t[0:half] + psum_t[half:nc]
        sel_t = _select_blocks_t(imp_t, t_row, ns)
        sel = _pad_to(sel_t, V7X_LANES, tq).T[:, 0:ns].astype(BF16)

        def make_bias(c, _):
            kpos = c * tk + lax.broadcasted_iota(jnp.int32, (1, tk), 1)
            brow = lax.broadcasted_iota(jnp.int32, (ns, 1), 0)
            expand = jnp.where(kpos // SEL_BLOCK == brow, 1.0, 0.0).astype(BF16)
            chosen = _dot(sel, expand)
            bias_scr[c] = jnp.where(kpos <= t_tile, jnp.where(chosen > 0.5, 0.0, NEG_INF), NEG_INF)
            return 0

        lax.fori_loop(0, n_chunks, make_bias, 0)
        kw = jnp.concatenate([kw_scr[wc0 + i, rows_g, :] for i in range(n_wc)], axis=1)
        vw = jnp.concatenate([vw_scr[wc0 + i, rows_g, :] for i in range(n_wc)], axis=1)

        def chunk(c, carry):
            m, l, acc = carry
            b = bias_scr[c]
            s = _dot(q4, ks_scr[c, rows_g, :]) + jnp.concatenate([b] * HEADS_PER_KV, axis=0)
            m_new = jnp.maximum(m, jnp.max(s, axis=-1, keepdims=True))
            m_safe = jnp.where(m_new == NEG_INF, 0.0, m_new)
            alpha = jnp.exp(m - m_safe)
            p = jnp.exp(s - m_safe)
            l = alpha * l + jnp.sum(p, axis=-1, keepdims=True)
            acc = alpha * acc + _dot_nt(p.astype(BF16), vs_scr[c, rows_g, :])
            return m_new, l, acc

        rows = HEADS_PER_KV * tq
        init = (jnp.full((rows, 1), NEG_INF, F32), jnp.zeros((rows, 1), F32), jnp.zeros((rows, HEAD_DIM), F32))
        _, l_s, acc_s = lax.fori_loop(0, n_chunks, chunk, init)
        o_s = acc_s / jnp.maximum(l_s, 1e-30)

        p_w = _softmax_rows(_dot(q4, kw) + wbias4)
        o_w = _dot_nt(p_w.astype(BF16), vw)

        o = _gate_cols(gates, g, 0) * o_c + _gate_cols(gates, g, 1) * o_s + _gate_cols(gates, g, 2) * o_w
        for r in range(HEADS_PER_KV):
            head = g * HEADS_PER_KV + r
            o_ref[:, head * HEAD_DIM:(head + 1) * HEAD_DIM] = o[r * tq:(r + 1) * tq].astype(BF16)


def _attn_prompt(q, gates, ck, cv, selt, wint, tq):
    b, kvw, s_len = selt.shape
    nc = ck.shape[1]
    hq = N_HEADS * HEAD_DIM
    nq = s_len // tq
    tk = min(512, s_len)
    assert s_len % tk == 0 and s_len % tq == 0 and tk % tq == 0 and tq % V7X_LANES == 0 and nc <= V7X_LANES
    assert min(WINDOW + tq, s_len) % WIN_CHUNK == 0
    return pl.pallas_call(
        functools.partial(_attn_prompt_kernel, tk=tk),
        out_shape=jax.ShapeDtypeStruct((b * s_len, hq), BF16),
        grid=(b, nq),
        in_specs=[
            pl.BlockSpec((tq, hq), lambda i, j: (i * nq + j, 0)),
            pl.BlockSpec((tq, V7X_LANES), lambda i, j: (i * nq + j, 0)),
            pl.BlockSpec((1, nc, KV_WIDTH), lambda i, j: (i, 0, 0)),
            pl.BlockSpec((1, nc, KV_WIDTH), lambda i, j: (i, 0, 0)),
            pl.BlockSpec((1, kvw, s_len), lambda i, j: (i, 0, 0)),
            pl.BlockSpec((1, kvw, s_len), lambda i, j: (i, 0, 0)),
        ],
        out_specs=pl.BlockSpec((tq, hq), lambda i, j: (i * nq + j, 0)),
        scratch_shapes=[
            pltpu.VMEM((s_len // tk, KV_WIDTH, tk), BF16),
            pltpu.VMEM((s_len // tk, KV_WIDTH, tk), BF16),
            pltpu.VMEM((s_len // WIN_CHUNK, KV_WIDTH, WIN_CHUNK), BF16),
            pltpu.VMEM((s_len // WIN_CHUNK, KV_WIDTH, WIN_CHUNK), BF16),
            pltpu.VMEM((s_len // tk, tq, tk), F32),
        ],
        compiler_params=_cparams("arbitrary", "arbitrary"),
        name="attn_prompt",
    )(q, gates, ck, cv, selt, wint)


PAGES_PER_STEP = 8
DECODE_TOK_PAD = V7X_SUBLANES


def _block_diag_q(q):
    parts = []
    for g in range(KV_HEADS):
        q4 = _stack_heads(q, g)
        z = jnp.zeros_like(q4)
        parts.append(jnp.concatenate([q4 if gg == g else z for gg in range(KV_HEADS)], axis=1))
    return jnp.concatenate(parts, axis=0)


def _attn_decode_kernel(pt_ref, q_ref, gate_ref, nsel_ref, nwin_ref, wbuf_ref, bdk_ref, bdv_ref, pek_ref, pev_ref,
                        rc_ref, ra_ref, rb_ref, *rest, page, n_pages):
    pps = PAGES_PER_STEP
    cpages = rest[:pps]
    spages = rest[pps:2 * pps]
    o_ref = rest[2 * pps]
    cmp_scr, s_scr, vs_scr, qbd_scr = rest[2 * pps + 1:]
    del pt_ref
    j = pl.program_id(1)
    tq = q_ref.shape[1]
    past_len = n_pages * page
    n_lane_chunks = 2 * KV_WIDTH // V7X_LANES

    @pl.when(j == 0)
    def _():
        qbd_scr[...] = _block_diag_q(q_ref[0].astype(F32)).astype(BF16)

    qbd = qbd_scr[...]
    for p in range(pps):
        r0 = pl.multiple_of((j * pps + p) * page, page)
        for c in range(n_lane_chunks):
            cmp_scr[c, pl.ds(r0, page), :] = cpages[p][0, 0, c * V7X_LANES:(c + 1) * V7X_LANES, :].T
    k_step = jnp.concatenate([spages[p][0, 0, 0:KV_WIDTH, :] for p in range(pps)], axis=1).astype(BF16)
    s_scr[j] = _dot(qbd, k_step)
    vs_scr[j] = jnp.concatenate([spages[p][0, 0, KV_WIDTH:2 * KV_WIDTH, :] for p in range(pps)],
                                axis=1).astype(BF16)

    @pl.when(j == pl.num_programs(1) - 1)
    def _():
        tok = lax.broadcasted_iota(jnp.int32, (tq, 1), 0)
        t_rows = past_len + jnp.concatenate([tok] * N_HEADS, axis=0)
        nsel = _pad_to(nsel_ref[0], page, 2 * KV_WIDTH)
        s_new = _dot_nt(qbd, nsel[:, 0:KV_WIDTH].astype(BF16))

        nc = past_len // CMP_BLOCK
        half = nc // 2

        def load_rows(c, start, n):
            return cmp_scr[c, pl.ds(start, n, stride=2 * CMP_BLOCK), :]

        acck, accv = _compress_rows(load_rows, bdk_ref, bdv_ref, pek_ref, pev_ref, nc)
        ck = _rope(acck, rc_ref[...], ra_ref[...], rb_ref[...]).astype(BF16)
        p_c = _masked_softmax(_dot_nt(qbd, ck), _cmp_end_row(nc) <= t_rows)
        o_c = _dot(p_c.astype(BF16), accv.astype(BF16))

        ns = n_pages * page // SEL_BLOCK + 1
        grp = HEADS_PER_KV * tq
        psum = []
        for g in range(KV_HEADS):
            pg_ = p_c[g * grp:g * grp + tq]
            for r in range(1, HEADS_PER_KV):
                pg_ = pg_ + p_c[g * grp + r * tq:g * grp + (r + 1) * tq]
            psum.append(pg_)
        psum = jnp.concatenate(psum, axis=0)
        imp = jnp.concatenate([psum[:, 0:half] + psum[:, half:nc], jnp.zeros((KV_HEADS * tq, 1), F32)], axis=1)
        selm = _select_blocks(imp, past_len + jnp.concatenate([tok] * KV_HEADS, axis=0), ns)
        selm_rows = jnp.concatenate(
            [selm[g * tq:(g + 1) * tq] for g in range(KV_HEADS) for _ in range(HEADS_PER_KV)], axis=0)
        nkeys = (n_pages + 1) * page
        kpos = lax.broadcasted_iota(jnp.int32, (1, nkeys), 1)
        brow = lax.broadcasted_iota(jnp.int32, (ns, 1), 0)
        expand = jnp.where(kpos // SEL_BLOCK == brow, 1.0, 0.0).astype(BF16)
        chosen = _dot(selm_rows.astype(BF16), expand)
        n_steps = n_pages // pps
        s_all = jnp.concatenate([s_scr[i] for i in range(n_steps)] + [s_new], axis=1)
        s_all = jnp.where(kpos <= t_rows, jnp.where(chosen > 0.5, s_all, NEG_INF), NEG_INF)
        p_s = _softmax_rows(s_all).astype(BF16)
        vt_all = jnp.concatenate([vs_scr[i] for i in range(n_steps)], axis=1)
        o_s = _dot_nt(p_s[:, 0:past_len], vt_all) + _dot(p_s[:, past_len:nkeys], nsel[:, KV_WIDTH:].astype(BF16))

        wb = wbuf_ref.shape[3]
        nwin = _pad_to(nwin_ref[0], page, 2 * KV_WIDTH)
        s_w = jnp.concatenate([_dot(qbd, wbuf_ref[0, 0, 0:KV_WIDTH, :].astype(BF16)),
                               _dot_nt(qbd, nwin[:, 0:KV_WIDTH].astype(BF16))], axis=1)
        wpos = past_len - wb + lax.broadcasted_iota(jnp.int32, (1, wb + page), 1)
        s_w = jnp.where(wpos <= t_rows, jnp.where(wpos >= t_rows - WINDOW, s_w, NEG_INF), NEG_INF)
        p_w = _softmax_rows(s_w).astype(BF16)
        o_w = (_dot_nt(p_w[:, 0:wb], wbuf_ref[0, 0, KV_WIDTH:2 * KV_WIDTH, :].astype(BF16))
               + _dot(p_w[:, wb:wb + page], nwin[:, KV_WIDTH:].astype(BF16)))

        gates = gate_ref[0]
        gcol = lambda br: jnp.concatenate([_gate_cols(gates, g, br) for g in range(KV_HEADS)], axis=0)
        o = gcol(0) * o_c + gcol(1) * o_s + gcol(2) * o_w
        for g in range(KV_HEADS):
            for r in range(HEADS_PER_KV):
                col = (g * HEADS_PER_KV + r) * HEAD_DIM
                o_ref[0, :, col:col + HEAD_DIM] = o[g * grp + r * tq:g * grp + (r + 1) * tq,
                                                    g * HEAD_DIM:(g + 1) * HEAD_DIM]


def _attn_decode(page_table, q3, gates3, nsel3, nwin3, cache_cmp_t, cache_sel_t, cache_win_t, layer, cw):
    b, tq, hq = q3.shape
    n_pages = page_table.shape[1]
    page = cache_cmp_t.shape[3]
    kvw = 2 * KV_WIDTH
    pps = PAGES_PER_STEP
    assert n_pages % pps == 0 and (n_pages * page) % (2 * CMP_BLOCK) == 0 and tq < CMP_BLOCK
    assert tq % V7X_SUBLANES == 0 and page == V7X_LANES
    nc = n_pages * page // CMP_BLOCK
    rt = _rope_tables(_cmp_order_end_pos(nc), KV_WIDTH)
    wb = cache_win_t.shape[3]
    full = lambda shp: pl.BlockSpec(shp, lambda *_: (0,) * len(shp))
    per_seq = lambda shp: pl.BlockSpec((1,) + shp, lambda i, j, pt: (i,) + (0,) * len(shp))
    page_spec = lambda p: pl.BlockSpec((1, 1, kvw, page), lambda i, j, pt: (layer, pt[i, j * pps + p], 0, 0))
    rows = N_HEADS * tq
    grid_spec = pltpu.PrefetchScalarGridSpec(
        num_scalar_prefetch=1,
        grid=(b, n_pages // pps),
        in_specs=[
            per_seq((tq, hq)), per_seq((tq, V7X_LANES)), per_seq((tq, kvw)), per_seq((tq, kvw)),
            pl.BlockSpec((1, 1, kvw, wb), lambda i, j, pt: (layer, i, 0, 0)),
            full(cw["bdk"].shape), full(cw["bdv"].shape), full(cw["pek"].shape), full(cw["pev"].shape),
            full(rt[0].shape), full(rt[1].shape), full(rt[2].shape),
            *[page_spec(p) for p in range(pps)],
            *[page_spec(p) for p in range(pps)],
        ],
        out_specs=per_seq((tq, hq)),
        scratch_shapes=[
            pltpu.VMEM((kvw // V7X_LANES, n_pages * page, V7X_LANES), F32),
            pltpu.VMEM((n_pages // pps, rows, pps * page), F32),
            pltpu.VMEM((n_pages // pps, KV_WIDTH, pps * page), BF16),
            pltpu.VMEM((rows, KV_WIDTH), BF16),
        ],
    )
    return pl.pallas_call(
        functools.partial(_attn_decode_kernel, page=page, n_pages=n_pages),
        out_shape=jax.ShapeDtypeStruct((b, tq, hq), F32),
        grid_spec=grid_spec,
        compiler_params=_cparams("arbitrary", "arbitrary"),
        name="attn_decode",
    )(page_table, q3, gates3, nsel3, nwin3, cache_win_t, cw["bdk"], cw["bdv"], cw["pek"], cw["pev"], *rt,
      *([cache_cmp_t] * pps), *([cache_sel_t] * pps))


def _out_proj_kernel(o_ref, w_ref, x_ref, mod_ref, xo_ref):
    d = x_ref.shape[1]
    xo_ref[...] = x_ref[...] + mod_ref[0][:, 2 * d:3 * d] * _dot(o_ref[...], w_ref[...])


def _out_proj(o, w_bf16, x, mod3, tpb, tm):
    t, d = x.shape
    k = o.shape[1]
    return pl.pallas_call(
        _out_proj_kernel,
        out_shape=jax.ShapeDtypeStruct((t, d), F32),
        grid=(t // tm,),
        in_specs=[
            pl.BlockSpec((tm, k), lambda i: (i, 0)),
            pl.BlockSpec((k, d), lambda i: (0, 0)),
            pl.BlockSpec((tm, d), lambda i: (i, 0)),
            _mod_spec(mod3, tpb),
        ],
        out_specs=pl.BlockSpec((tm, d), lambda i: (i, 0)),
        compiler_params=_cparams("arbitrary"),
        name="out_proj",
    )(o, w_bf16, x, mod3)


def _ffn_kernel(x_ref, mod_ref, g_ref, wg_ref, wu_ref, wd_ref, o_ref, h_scr):
    d = x_ref.shape[1]
    f = pl.program_id(1)

    @pl.when(f == 0)
    def _():
        mod = mod_ref[0]
        h_scr[...] = _norm_mod(x_ref[...], g_ref[...], mod[:, 4 * d:5 * d], mod[:, 3 * d:4 * d]).astype(BF16)
        o_ref[...] = jnp.zeros(o_ref.shape, F32)

    h = h_scr[...]
    a = _dot(h, wg_ref[0].astype(BF16))
    u = _dot(h, wu_ref[0].astype(BF16))
    act = (a * _sigmoid(a) * u).astype(BF16)
    o_ref[...] += _dot(act, wd_ref[0].astype(BF16))

    @pl.when(f == pl.num_programs(1) - 1)
    def _():
        o_ref[...] = x_ref[...] + mod_ref[0][:, 5 * d:6 * d] * o_ref[...]


def _ffn(x, mod3, tpb, g, w_gu_all, w_down_all, layer, tm, tf):
    t, d = x.shape
    ff = w_down_all.shape[1]
    nf = ff // tf
    return pl.pallas_call(
        _ffn_kernel,
        out_shape=jax.ShapeDtypeStruct((t, d), F32),
        grid=(t // tm, nf),
        in_specs=[
            pl.BlockSpec((tm, d), lambda i, f: (i, 0)),
            _mod_spec(mod3, tpb),
            pl.BlockSpec((1, d), lambda i, f: (0, 0)),
            pl.BlockSpec((1, d, tf), lambda i, f: (layer, 0, f)),
            pl.BlockSpec((1, d, tf), lambda i, f: (layer, 0, nf + f)),
            pl.BlockSpec((1, tf, d), lambda i, f: (layer, f, 0)),
        ],
        out_specs=pl.BlockSpec((tm, d), lambda i, f: (i, 0)),
        scratch_shapes=[pltpu.VMEM((tm, d), BF16)],
        compiler_params=_cparams("arbitrary", "arbitrary"),
        name="ffn",
    )(x, mod3, g, w_gu_all, w_gu_all, w_down_all)


def _conv_in(x_ref, mod_ref, g_ref, win_ref):
    d = x_ref.shape[1]
    mod = mod_ref[0]
    h = _norm_mod(x_ref[...], g_ref[...], mod[:, d:2 * d], mod[:, 0:d]).astype(BF16)
    bg = _dot(h, win_ref[:, 0:d])
    u = _dot(h, win_ref[:, d:2 * d]) * _dot(h, win_ref[:, 2 * d:3 * d])
    return bg, u


def _conv_out(bg, conv, x_ref, mod_ref, wout_ref, xo_ref):
    d = x_ref.shape[1]
    y = _dot((bg * conv).astype(BF16), wout_ref[...])
    xo_ref[...] = x_ref[...] + mod_ref[0][:, 2 * d:3 * d] * y


def _conv_seq_kernel(x_ref, mod_ref, g_ref, win_ref, wc_ref, wout_ref, st_ref, xo_ref, nst_ref, carry_scr):
    tm = x_ref.shape[0]
    bg, u = _conv_in(x_ref, mod_ref, g_ref, win_ref)

    @pl.when(pl.program_id(1) == 0)
    def _():
        carry_scr[6:8, :] = st_ref[0]

    ue = jnp.concatenate([carry_scr[...], u], axis=0)
    conv = wc_ref[0:1, :] * ue[6:6 + tm] + wc_ref[1:2, :] * ue[7:7 + tm] + wc_ref[2:3, :] * u
    carry_scr[...] = u[tm - 8:tm]
    nst_ref[0] = u[tm - 2:tm]
    _conv_out(bg, conv, x_ref, mod_ref, wout_ref, xo_ref)


def _conv_tok_kernel(x_ref, mod_ref, g_ref, win_ref, wc_ref, wout_ref, p1_ref, p2_ref, xo_ref, u_ref, *, t_seq):
    tm = x_ref.shape[0]
    bg, u = _conv_in(x_ref, mod_ref, g_ref, win_ref)
    row = lax.broadcasted_iota(jnp.int32, (tm, 1), 0) % t_seq
    prev1 = jnp.where(row >= 1, pltpu.roll(u, 1, 0), p1_ref[...])
    prev2 = jnp.where(row >= 2, pltpu.roll(u, 2, 0), p2_ref[...])
    conv = wc_ref[0:1, :] * prev2 + wc_ref[1:2, :] * prev1 + wc_ref[2:3, :] * u
    u_ref[...] = u
    _conv_out(bg, conv, x_ref, mod_ref, wout_ref, xo_ref)


def _conv_mixer(x, mod3, tpb, g, w_in, w_conv, w_out, state, b, t_seq, tm):
    t, d = x.shape
    weights = [_full((1, d)), _full(w_in.shape), _full(w_conv.shape), _full(w_out.shape)]
    if state is None:
        state = jnp.zeros((b, 2, d), F32)
    if t_seq >= tm:
        nt = t_seq // tm
        xo, nst = pl.pallas_call(
            _conv_seq_kernel,
            out_shape=(jax.ShapeDtypeStruct((t, d), F32), jax.ShapeDtypeStruct((b, 2, d), F32)),
            grid=(b, nt),
            in_specs=[pl.BlockSpec((tm, d), lambda i, j: (i * nt + j, 0)),
                      pl.BlockSpec((1, 1, mod3.shape[-1]), lambda i, j: (i, 0, 0)),
                      *weights,
                      pl.BlockSpec((1, 2, d), lambda i, j: (i, 0, 0))],
            out_specs=(pl.BlockSpec((tm, d), lambda i, j: (i * nt + j, 0)),
                       pl.BlockSpec((1, 2, d), lambda i, j: (i, 0, 0))),
            scratch_shapes=[pltpu.VMEM((8, d), F32)],
            compiler_params=_cparams("arbitrary", "arbitrary"),
            name="conv_seq",
        )(x, mod3, g, w_in, w_conv, w_out, state)
        return xo, nst
    assert t == tm and t_seq >= 2
    p1 = jnp.zeros((b, t_seq, d), F32).at[:, 0].set(state[:, 1]).reshape(t, d)
    p2 = jnp.zeros((b, t_seq, d), F32).at[:, 0].set(state[:, 0]).at[:, 1].set(state[:, 1]).reshape(t, d)
    xo, u = pl.pallas_call(
        functools.partial(_conv_tok_kernel, t_seq=t_seq),
        out_shape=(jax.ShapeDtypeStruct((t, d), F32), jax.ShapeDtypeStruct((t, d), F32)),
        grid=(1,),
        in_specs=[_full((t, d)), _full(mod3.shape), *weights, _full((t, d)), _full((t, d))],
        out_specs=(_full((t, d)), _full((t, d))),
        compiler_params=_cparams("arbitrary"),
        name="conv_tok",
    )(x, mod3, g, w_in, w_conv, w_out, p1, p2)
    return xo, u.reshape(b, t_seq, d)[:, t_seq - 2:]


MOE_ROW_UNIT = 128
MOE_BOUNDS_ROWS = 16


def _split_bf16(x):
    hi = x.astype(BF16)
    return hi, (x - hi.astype(F32)).astype(BF16)


def _moe_route_kernel(x_ref, mod_ref, g_ref, wr_ref, br_ref, h_ref, pos_ref, comb_ref, post_ref, bnd_ref, *, blk):
    tm, d = x_ref.shape
    nb = tm // blk
    mod = mod_ref[0]
    h32 = _norm_mod(x_ref[...], g_ref[...], mod[:, 4 * d:5 * d], mod[:, 3 * d:4 * d])
    h_ref[...] = h32.astype(BF16)
    lane = lax.broadcasted_iota(jnp.int32, (1, V7X_LANES), 1)
    h_hi, h_lo = _split_bf16(h32)
    w_hi, w_lo = _split_bf16(wr_ref[...])
    logits = _dot(h_hi, w_hi) + (_dot(h_hi, w_lo) + _dot(h_lo, w_hi)) + br_ref[...]
    logits = jnp.where(lane < N_EXPERTS, logits, NEG_INF)
    m1 = jnp.max(logits, axis=-1, keepdims=True)
    i1 = jnp.min(jnp.where(logits == m1, lane, V7X_LANES), axis=-1, keepdims=True)
    rest = jnp.where(lane == i1, NEG_INF, logits)
    m2 = jnp.max(rest, axis=-1, keepdims=True)
    i2 = jnp.min(jnp.where(rest == m2, lane, V7X_LANES), axis=-1, keepdims=True)
    e2 = jnp.exp(m2 - m1)
    den = 1.0 + e2
    comb_ref[...] = jnp.where(lane == i1, 1.0 / den, jnp.where(lane == i2, e2 / den, 0.0))
    mask = jnp.where(lane == i1, 1.0, jnp.where(lane == i2, 1.0, 0.0))
    r = lax.broadcasted_iota(jnp.int32, (blk, blk), 0)
    c = lax.broadcasted_iota(jnp.int32, (blk, blk), 1)
    lower = jnp.where(c < r, 1.0, 0.0).astype(BF16)
    upper = jnp.where(r < c, 1.0, 0.0).astype(BF16)
    carry = jnp.zeros((1, V7X_LANES), F32)
    bnd_ref[0] = jnp.zeros((MOE_BOUNDS_ROWS, V7X_LANES), F32)
    for k in range(nb):
        mb = mask[k * blk:(k + 1) * blk]
        pos_ref[k * blk:(k + 1) * blk, :] = jnp.where(mb > 0.5, _dot(lower, mb.astype(BF16)) + carry, -1.0)
        carry = carry + jnp.sum(mb, axis=0, keepdims=True)
        bnd_ref[0, k + 1:k + 2, :] = carry
    eye = jnp.where(lax.broadcasted_iota(jnp.int32, (V7X_LANES, V7X_LANES), 0)
                    == lax.broadcasted_iota(jnp.int32, (V7X_LANES, V7X_LANES), 1), 1.0, 0.0).astype(BF16)
    mask_t = _dot_nt(eye, mask.astype(BF16))[0:N_EXPERTS]
    carry_t = jnp.zeros((N_EXPERTS, 1), F32)
    for k in range(nb):
        mb = mask_t[:, k * blk:(k + 1) * blk]
        post_ref[0, k] = jnp.where(mb > 0.5, _dot(mb.astype(BF16), upper) + carry_t, -1.0)
        carry_t = carry_t + jnp.sum(mb, axis=1, keepdims=True)


def _moe_route(x, mod3, tpb, g, w_router, b_router, tm, blk):
    t, d = x.shape
    ne = w_router.shape[1]
    nt = t // tm
    nb = tm // blk
    assert nb + 1 <= MOE_BOUNDS_ROWS
    wr = jnp.pad(w_router, ((0, 0), (0, V7X_LANES - ne)))
    br = jnp.pad(b_router, (0, V7X_LANES - ne)).reshape(1, V7X_LANES)
    tok = lambda w: pl.BlockSpec((tm, w), lambda i: (i, 0))
    return pl.pallas_call(
        functools.partial(_moe_route_kernel, blk=blk),
        out_shape=(
            jax.ShapeDtypeStruct((t, d), BF16),
            jax.ShapeDtypeStruct((t, V7X_LANES), F32),
            jax.ShapeDtypeStruct((t, V7X_LANES), F32),
            jax.ShapeDtypeStruct((nt, nb, N_EXPERTS, blk), F32),
            jax.ShapeDtypeStruct((nt, MOE_BOUNDS_ROWS, V7X_LANES), F32),
        ),
        grid=(nt,),
        in_specs=[tok(d), _mod_spec(mod3, tpb), _full((1, d)), _full((d, V7X_LANES)), _full((1, V7X_LANES))],
        out_specs=(tok(d), tok(V7X_LANES), tok(V7X_LANES),
                   pl.BlockSpec((1, nb, N_EXPERTS, blk), lambda i: (i, 0, 0, 0)),
                   pl.BlockSpec((1, MOE_BOUNDS_ROWS, V7X_LANES), lambda i: (i, 0, 0))),
        compiler_params=_cparams("arbitrary"),
        name="moe_route",
    )(x, mod3, g, wr, br)


def _moe_expert_kernel(bnd_ref, h_ref, pos_ref, comb_ref, post_ref, wg_ref, wu_ref, wd_ref, o_ref,
                       hc_scr, acc_scr, posc_scr, combc_scr, *, blk):
    tm, d = h_ref.shape
    nb = tm // blk
    unit = MOE_ROW_UNIT
    i = pl.program_id(0)
    e = pl.program_id(1)
    f = pl.program_id(2)
    nf = pl.num_programs(2)
    base = (i * N_EXPERTS + e) * MOE_BOUNDS_ROWS
    cnt = bnd_ref[base + nb]
    n_units = (cnt + unit - 1) // unit

    def span(u):
        start = u * unit
        stop = jnp.minimum(start + unit, cnt)
        lo = jnp.int32(0)
        hi = jnp.int32(0)
        for k in range(1, nb):
            bk = bnd_ref[base + k]
            lo = lo + (bk <= start).astype(jnp.int32)
            hi = hi + (bk < stop).astype(jnp.int32)
        return lo, hi + 1

    @pl.when((e == 0) & (f == 0))
    def _():
        o_ref[...] = jnp.zeros(o_ref.shape, F32)

    @pl.when(f == 0)
    def _():
        def per_unit(u, _):
            lo, hi = span(u)
            r0 = pl.multiple_of(u * unit, unit)
            want = (r0 + lax.broadcasted_iota(jnp.int32, (unit, 1), 0)).astype(F32)
            acc_scr[pl.ds(r0, unit), :] = jnp.zeros((unit, d), F32)

            def per_blk(k, _):
                t0 = pl.multiple_of(k * blk, blk)
                onehot = jnp.where(post_ref[0, k, pl.ds(e, 1), :] == want, 1.0, 0.0).astype(BF16)
                acc_scr[pl.ds(r0, unit), :] += _dot(onehot, h_ref[pl.ds(t0, blk), :])
                return 0

            lax.fori_loop(lo, hi, per_blk, 0)
            hc_scr[pl.ds(r0, unit), :] = acc_scr[pl.ds(r0, unit), :].astype(BF16)
            acc_scr[pl.ds(r0, unit), :] = jnp.zeros((unit, d), F32)
            return 0

        lax.fori_loop(0, n_units, per_unit, 0)

    def ffn_rows(r0, m):
        hc = hc_scr[pl.ds(r0, m), :]
        a = _dot(hc, wg_ref[0, 0].astype(BF16))
        act = (a * _sigmoid(a) * _dot(hc, wu_ref[0, 0].astype(BF16))).astype(BF16)
        acc_scr[pl.ds(r0, m), :] += _dot(act, wd_ref[0, 0].astype(BF16))

    n_quads = n_units // 4
    rem = n_units - 4 * n_quads

    def quad(k, _):
        ffn_rows(pl.multiple_of(k * 4 * unit, 4 * unit), 4 * unit)
        return 0

    lax.fori_loop(0, n_quads, quad, 0)
    r_tail = pl.multiple_of(n_quads * 4 * unit, 4 * unit)

    @pl.when(rem >= 2)
    def _():
        ffn_rows(r_tail, 2 * unit)

    @pl.when(rem % 2 == 1)
    def _():
        ffn_rows(pl.multiple_of(r_tail + (rem // 2) * 2 * unit, unit), unit)

    @pl.when(f == nf - 1)
    def _():
        lane = lax.broadcasted_iota(jnp.int32, (1, V7X_LANES), 1)
        posc_scr[...] = jnp.sum(jnp.where(lane == e, pos_ref[...], 0.0), axis=-1, keepdims=True)
        combc_scr[...] = jnp.sum(jnp.where(lane == e, comb_ref[...], 0.0), axis=-1, keepdims=True)

        def per_unit(u, _):
            lo, hi = span(u)
            r0 = pl.multiple_of(u * unit, unit)
            want = (r0 + lax.broadcasted_iota(jnp.int32, (1, unit), 1)).astype(F32)
            y = acc_scr[pl.ds(r0, unit), :].astype(BF16)

            def per_blk(k, _):
                t0 = pl.multiple_of(k * blk, blk)
                onehot = jnp.where(posc_scr[pl.ds(t0, blk), :] == want, 1.0, 0.0).astype(BF16)
                o_ref[pl.ds(t0, blk), :] += combc_scr[pl.ds(t0, blk), :] * _dot(onehot, y)
                return 0

            lax.fori_loop(lo, hi, per_blk, 0)
            return 0

        lax.fori_loop(0, n_units, per_unit, 0)


def _moe_experts(h, pos, comb, post, bounds, w_gu_all, w_down_all, layer, tm, tf, blk):
    t, d = h.shape
    ne, ff = w_down_all.shape[1], w_down_all.shape[2]
    nf = ff // tf
    nt = t // tm
    nb = tm // blk
    bnd = jnp.transpose(bounds[:, :, :ne], (0, 2, 1)).astype(jnp.int32).reshape(nt * ne * MOE_BOUNDS_ROWS)
    once = pl.Buffered(1)
    tok = lambda w: pl.BlockSpec((tm, w), lambda i, e, f, b_: (i, 0), pipeline_mode=once)
    grid_spec = pltpu.PrefetchScalarGridSpec(
        num_scalar_prefetch=1,
        grid=(nt, ne, nf),
        in_specs=[
            tok(d), tok(V7X_LANES), tok(V7X_LANES),
            pl.BlockSpec((1, nb, ne, blk), lambda i, e, f, b_: (i, 0, 0, 0)),
            pl.BlockSpec((1, 1, d, tf), lambda i, e, f, b_: (layer, e, 0, f)),
            pl.BlockSpec((1, 1, d, tf), lambda i, e, f, b_: (layer, e, 0, nf + f)),
            pl.BlockSpec((1, 1, tf, d), lambda i, e, f, b_: (layer, e, f, 0)),
        ],
        out_specs=pl.BlockSpec((tm, d), lambda i, e, f, b_: (i, 0), pipeline_mode=once),
        scratch_shapes=[
            pltpu.VMEM((tm, d), BF16),
            pltpu.VMEM((tm, d), F32),
            pltpu.VMEM((tm, 1), F32),
            pltpu.VMEM((tm, 1), F32),
        ],
    )
    return pl.pallas_call(
        functools.partial(_moe_expert_kernel, blk=blk),
        out_shape=jax.ShapeDtypeStruct((t, d), F32),
        grid_spec=grid_spec,
        compiler_params=_cparams("arbitrary", "arbitrary", "arbitrary"),
        name="moe_experts",
    )(bnd, h, pos, comb, post, w_gu_all, w_gu_all, w_down_all)


def _residual_kernel(x_ref, f_ref, mod_ref, g_ref, o_ref, *, final_norm):
    d = x_ref.shape[1]
    x = x_ref[...] + mod_ref[0][:, 5 * d:6 * d] * f_ref[...]
    if final_norm:
        x = (x * lax.rsqrt(jnp.mean(x * x, axis=-1, keepdims=True) + EPS)) * g_ref[...]
    o_ref[...] = x


def _residual(x, f, mod3, tpb, g_final, tm, final_norm):
    t, d = x.shape
    tok = pl.BlockSpec((tm, d), lambda i: (i, 0))
    return pl.pallas_call(
        functools.partial(_residual_kernel, final_norm=final_norm),
        out_shape=jax.ShapeDtypeStruct((t, d), F32),
        grid=(t // tm,),
        in_specs=[tok, tok, _mod_spec(mod3, tpb), _full((1, d))],
        out_specs=tok,
        compiler_params=_cparams("arbitrary"),
        name="residual",
    )(x, f, mod3, g_final)


def _moe(x, mod3_of, g, w_router, b_router, w_gu_all, w_down_all, layer, g_final, tm, final_norm):
    blk = min(256, tm)
    h, pos, comb, post, bounds = _moe_route(x, *mod3_of(tm), g, w_router, b_router, tm, blk)
    f = _moe_experts(h, pos, comb, post, bounds, w_gu_all, w_down_all, layer, tm, 512, blk)
    tm_res = min(1024, tm)
    return _residual(x, f, *mod3_of(tm_res), g_final, tm_res, final_norm)


def _final_norm_kernel(x_ref, g_ref, o_ref):
    x = x_ref[...]
    o_ref[...] = (x * lax.rsqrt(jnp.mean(x * x, axis=-1, keepdims=True) + EPS)) * g_ref[...]


def _final_norm(x, g, tm):
    t, d = x.shape
    return pl.pallas_call(
        _final_norm_kernel,
        out_shape=jax.ShapeDtypeStruct((t, d), F32),
        grid=(t // tm,),
        in_specs=[pl.BlockSpec((tm, d), lambda i: (i, 0)), pl.BlockSpec((1, d), lambda i: (0, 0))],
        out_specs=pl.BlockSpec((tm, d), lambda i: (i, 0)),
        compiler_params=_cparams("arbitrary"),
        name="final_norm",
    )(x, g)


def _nsa_layer_weights(w_in, w_out, w_ck, pe_ck, w_cv, pe_cv):
    n_main = N_HEADS * HEAD_DIM + 6 * KV_WIDTH
    w_gate = jnp.pad(w_in[:, n_main:], ((0, 0), (0, V7X_LANES - N_HEADS * N_BRANCH)))
    return {"w_main": w_in[:, :n_main].astype(BF16), "w_gate": w_gate.astype(BF16),
            "w_out": w_out.astype(BF16), "cw": _compress_weights(w_ck, pe_ck, w_cv, pe_cv)}


def _rows_from_feature_major(xt):
    b, _, s = xt.shape
    return jnp.transpose(xt.reshape(b, 2, KV_HEADS, HEAD_DIM, s), (0, 4, 1, 2, 3))


def _feature_major_from_rows(x, lead):
    n = len(lead)
    perm = tuple(range(n)) + (n + 1, n + 2, n + 3, n)
    xt = jnp.transpose(x, perm)
    return xt.reshape(lead + (2 * KV_WIDTH, x.shape[n]))


def _nsa_prompt_layer(x, mod, g, lw, b, s_len):
    tm = min(512, s_len)
    mod3 = mod.reshape(b, 1, mod.shape[-1])
    rope_t = _rope_tables(jnp.arange(s_len), V7X_LANES)
    q, gates, cmp_rows, cmpt, selt, wint = _nsa_proj(x, mod3, s_len // tm, g, lw["w_main"], lw["w_gate"], rope_t,
                                                     s_len, tm, True)
    ck, cv = _compress(cmp_rows.reshape(b, s_len, 2 * KV_WIDTH), lw["cw"])
    o = _attn_prompt(q, gates, ck, cv, selt, wint, tq=min(256, s_len))
    x = _out_proj(o, lw["w_out"], x, mod3, s_len // tm, tm)
    return x, cmpt, selt, wint


def _nsa_decode_layer(x, mod3, g, lw, b, t_seq, caches, layer):
    t = x.shape[0]
    page_table = caches["page_table"]
    past_len = page_table.shape[1] * caches["cmp"].shape[3]
    rope_t = _rope_tables(past_len + jnp.arange(t) % t_seq, V7X_LANES)
    q, gates, cmp_rows, sel_rows, win_rows = _nsa_proj(x, mod3, 1, g, lw["w_main"], lw["w_gate"], rope_t, t, t, False)
    pad3 = lambda a: jnp.pad(a.reshape(b, t_seq, a.shape[-1]), ((0, 0), (0, DECODE_TOK_PAD - t_seq), (0, 0)))
    o = _attn_decode(page_table, pad3(q), pad3(gates), pad3(sel_rows), pad3(win_rows),
                     caches["cmp"], caches["sel"], caches["win"], layer, lw["cw"])
    o = o[:, :t_seq].reshape(t, o.shape[-1]).astype(BF16)
    x = _out_proj(o, lw["w_out"], x, mod3, 1, t)
    return x, cmp_rows, sel_rows, win_rows


def _trunk(x3, mod, caches, p):
    b, t_seq, d = x3.shape
    t = b * t_seq
    kvw = 2 * KV_WIDTH
    prompt = caches is None
    x = x3.reshape(t, d)
    new_cmp, new_sel, new_win, new_conv = [], [], [], []
    depth = mod.shape[0]
    g_final = p["g_norm_final"].reshape(1, d)
    finished = False
    for i in range(depth):
        j = i // 2
        if prompt:
            mod_i = mod[i].reshape(b, 1, 6 * d)
            mod3_of = lambda tm, mod_i=mod_i: (mod_i, t_seq // tm)
            tm_small, tm_big, tm_moe = min(512, t_seq), min(1024, t_seq), min(2048, t_seq)
        else:
            mod_i = jnp.repeat(mod[i], t_seq, axis=0).reshape(1, t, 6 * d)
            mod3_of = lambda tm, mod_i=mod_i: (mod_i, 1)
            tm_small = tm_big = tm_moe = t
        g_mix = p["g_norm_mix"][i].reshape(1, d)
        g_ffn = p["g_norm_ffn"][i].reshape(1, d)
        if i % 2 == 0:
            lw = _nsa_layer_weights(p["w_nsa_in"][j], p["w_nsa_out"][j], p["w_cmp_k"][j], p["pe_cmp_k"][j],
                                    p["w_cmp_v"][j], p["pe_cmp_v"][j])
            if prompt:
                x, cmpt, selt, wint = _nsa_prompt_layer(x, mod[i], g_mix, lw, b, t_seq)
                new_cmp.append(_rows_from_feature_major(cmpt))
                new_sel.append(_rows_from_feature_major(selt))
                new_win.append(_rows_from_feature_major(wint[:, :, t_seq - min(WINDOW, t_seq):]))
            else:
                x, c_rows, s_rows, w_rows = _nsa_decode_layer(x, mod_i, g_mix, lw, b, t_seq, caches, j)
                rows5 = lambda r: r.reshape(b, t_seq, 2, KV_HEADS, HEAD_DIM)
                new_cmp.append(rows5(c_rows))
                new_sel.append(rows5(s_rows))
                new_win.append(jnp.concatenate([caches["win_rows"][j], rows5(w_rows)], axis=1)[:, t_seq:])
            x = _ffn(x, *mod3_of(tm_big), g_ffn, p["w_ff_gu"], p["w_ff_down"], j, tm_big, 512)
        else:
            state = None if prompt else caches["conv"][j]
            x, st = _conv_mixer(x, *mod3_of(tm_small), g_mix, p["w_conv_in"][j].astype(BF16), p["w_conv"][j],
                                p["w_conv_out"][j].astype(BF16), state, b, t_seq, tm_small)
            new_conv.append(st)
            finished = i == depth - 1
            x = _moe(x, mod3_of, g_ffn, p["w_router"][j], p["b_router"][j], p["w_moe_gu"], p["w_moe_down"], j,
                     g_final, tm_moe, finished)
    y = x if finished else _final_norm(x, g_final, tm_big)
    return (y.reshape(b, t_seq, d), jnp.stack(new_cmp), jnp.stack(new_sel), jnp.stack(new_win), jnp.stack(new_conv))


def kernel(x_prompt, x_sample, cache_cmp_kv, cache_sel_kv, cache_win_kv, state_conv, page_table, c_prompt, c_sample,
           w_ada, b_ada, g_norm_mix, g_norm_ffn, g_norm_final, w_nsa_in, w_nsa_out, w_cmp_k, pe_cmp_k, w_cmp_v,
           pe_cmp_v, w_conv_in, w_conv, w_conv_out, w_ff_gu, w_ff_down, w_router, b_router, w_moe_gu, w_moe_down):
    p = {"g_norm_mix": g_norm_mix, "g_norm_ffn": g_norm_ffn, "g_norm_final": g_norm_final,
         "w_nsa_in": w_nsa_in, "w_nsa_out": w_nsa_out, "w_cmp_k": w_cmp_k, "pe_cmp_k": pe_cmp_k, "w_cmp_v": w_cmp_v,
         "pe_cmp_v": pe_cmp_v, "w_conv_in": w_conv_in, "w_conv": w_conv, "w_conv_out": w_conv_out,
         "w_ff_gu": w_ff_gu, "w_ff_down": w_ff_down, "w_router": w_router, "b_router": b_router,
         "w_moe_gu": w_moe_gu, "w_moe_down": w_moe_down}
    nb = x_prompt.shape[0]
    mod = _ada(jnp.concatenate([c_prompt, c_sample], axis=0), w_ada, b_ada)
    y_p, cmp_p, sel_p, win_p, conv_p = _trunk(x_prompt, mod[:, :nb], None, p)
    caches = {"cmp": _feature_major_from_rows(cache_cmp_kv, cache_cmp_kv.shape[:2]),
              "sel": _feature_major_from_rows(cache_sel_kv, cache_sel_kv.shape[:2]),
              "win": _feature_major_from_rows(cache_win_kv, cache_win_kv.shape[:2]),
              "win_rows": cache_win_kv, "conv": state_conv, "page_table": page_table}
    y_s, cmp_s, sel_s, win_s, conv_s = _trunk(x_sample, mod[:, nb:], caches, p)
    return (y_p, y_s, cmp_p, cmp_s, sel_p, sel_s, win_p, win_s, conv_p, conv_s)
```

```python
import functools

import jax
import jax.numpy as jnp
from jax import lax
from jax.experimental import pallas as pl
from jax.experimental.pallas import tpu as pltpu

F32 = jnp.float32
BF16 = jnp.bfloat16

N_HEADS = 16
HEAD_DIM = 64
KV_HEADS = 4
HEADS_PER_KV = N_HEADS // KV_HEADS
KV_WIDTH = KV_HEADS * HEAD_DIM
ROT_DIM = HEAD_DIM // 4
ROPE_THETA = 500000.0
CMP_BLOCK = 32
SEL_BLOCK = 64
TOP_N = 8
WINDOW = 512
N_BRANCH = 3
FORCE_BONUS = float(HEADS_PER_KV + 1)
N_EXPERTS = 8
EPS = 1e-6
NEG_INF = float("-inf")

V7X_LANES = 128
V7X_SUBLANES = 8
V7X_VMEM_LIMIT_BYTES = 58 * 1024 * 1024


def _cparams(*sem):
    return pltpu.CompilerParams(dimension_semantics=sem, vmem_limit_bytes=V7X_VMEM_LIMIT_BYTES)


def _dot(a, b):
    return jnp.dot(a, b, preferred_element_type=F32)


def _dot_nt(a, b):
    return lax.dot_general(a, b, (((1,), (1,)), ((), ())), preferred_element_type=F32)


def _norm_mod(x, g, sc, sh):
    y = x * lax.rsqrt(jnp.mean(x * x, axis=-1, keepdims=True) + EPS)
    return (y * g) * (1.0 + sc) + sh


def _sigmoid(x):
    return 1.0 / (1.0 + jnp.exp(-x))


def _mod_spec(mod3, tiles_per_block):
    _, r, w = mod3.shape
    return pl.BlockSpec((1, r, w), lambda i, *_: (i // tiles_per_block, 0, 0))


def _full(shape):
    return pl.BlockSpec(shape, lambda *_: (0,) * len(shape))


def _ada_kernel(c_ref, w_ref, b_ref, o_ref):
    c = c_ref[...]
    cs = (c * _sigmoid(c)).astype(BF16)
    o_ref[0] = _dot(cs, w_ref[0].astype(BF16)) + b_ref[0]


def _ada(c, w_ada, b_ada):
    depth, d, n = w_ada.shape
    nb = c.shape[0]
    tn = 1536
    return pl.pallas_call(
        _ada_kernel,
        out_shape=jax.ShapeDtypeStruct((depth, nb, n), F32),
        grid=(depth, n // tn),
        in_specs=[
            pl.BlockSpec((nb, d), lambda i, j: (0, 0)),
            pl.BlockSpec((1, d, tn), lambda i, j: (i, 0, j)),
            pl.BlockSpec((1, 1, tn), lambda i, j: (i, 0, j)),
        ],
        out_specs=pl.BlockSpec((1, nb, tn), lambda i, j: (i, 0, j)),
        compiler_params=_cparams("arbitrary", "arbitrary"),
        name="ada",
    )(c, w_ada, b_ada.reshape(depth, 1, n))


def _rope_tables(pos, width):
    half = ROT_DIM // 2
    inv = 1.0 / (ROPE_THETA ** (jnp.arange(half, dtype=F32) * (2.0 / ROT_DIM)))
    ang = pos.astype(F32)[:, None] * inv[None, :]
    cos, sin = jnp.cos(ang), jnp.sin(ang)
    n = pos.shape[0]
    z8 = jnp.zeros((n, half), F32)
    rest0 = jnp.zeros((n, HEAD_DIM - ROT_DIM), F32)
    c = jnp.concatenate([cos, cos, rest0 + 1.0], axis=1)
    a = jnp.concatenate([-sin, z8, rest0], axis=1)
    b = jnp.concatenate([z8, sin, rest0], axis=1)
    rep = width // HEAD_DIM
    return tuple(jnp.tile(t, (1, rep)) for t in (c, a, b))


def _rope(v, c, a, b):
    w = v.shape[-1]
    half = ROT_DIM // 2
    return v * c + pltpu.roll(v, w - half, 1) * a + pltpu.roll(v, half, 1) * b


def _widen(t, width):
    return jnp.concatenate([t] * (width // t.shape[-1]), axis=1)


def _nsa_proj_kernel(x_ref, mod_ref, g_ref, w_ref, wg_ref, rc_ref, ra_ref, rb_ref, q_ref, gate_ref, *kv_refs,
                     feature_major):
    d = x_ref.shape[1]
    mod = mod_ref[0]
    h = _norm_mod(x_ref[...], g_ref[...], mod[:, d:2 * d], mod[:, 0:d]).astype(BF16)
    kvw = 2 * KV_WIDTH
    c1, a1, b1 = rc_ref[...], ra_ref[...], rb_ref[...]
    c2, a2, b2 = (_widen(t, kvw) for t in (c1, a1, b1))
    scale = HEAD_DIM ** -0.5
    hq = N_HEADS * HEAD_DIM
    for j in range(hq // kvw):
        p = _dot(h, w_ref[:, j * kvw:(j + 1) * kvw])
        q_ref[:, j * kvw:(j + 1) * kvw] = (_rope(p, c2, a2, b2) * scale).astype(BF16)
    gate_ref[...] = _sigmoid(_dot(h, wg_ref[...]))
    one = jnp.ones_like(c1)
    zero = jnp.zeros_like(c1)
    ck = jnp.concatenate([_widen(c1, KV_WIDTH), _widen(one, KV_WIDTH)], axis=1)
    ak = jnp.concatenate([_widen(a1, KV_WIDTH), _widen(zero, KV_WIDTH)], axis=1)
    bk = jnp.concatenate([_widen(b1, KV_WIDTH), _widen(zero, KV_WIDTH)], axis=1)
    cmp_rows = _dot(h, w_ref[:, hq:hq + kvw])
    sel_rows = _rope(_dot(h, w_ref[:, hq + kvw:hq + 2 * kvw]), ck, ak, bk)
    win_rows = _rope(_dot(h, w_ref[:, hq + 2 * kvw:hq + 3 * kvw]), ck, ak, bk)
    if feature_major:
        cmp_ref, cmpt_ref, selt_ref, wint_ref = kv_refs
        cmp_ref[...] = cmp_rows
        cmpt_ref[0] = cmp_rows.T
        selt_ref[0] = sel_rows.T
        wint_ref[0] = win_rows.T
    else:
        cmp_ref, sel_ref, win_ref = kv_refs
        cmp_ref[...] = cmp_rows
        sel_ref[...] = sel_rows
        win_ref[...] = win_rows


def _nsa_proj(x, mod3, tpb, g, w_main, w_gate, rope_t, rows_per_seq, tm, feature_major):
    t, d = x.shape
    nw = w_main.shape[1]
    kvw = 2 * KV_WIDTH
    hq = N_HEADS * HEAD_DIM
    nrt = rows_per_seq // tm
    rspec = pl.BlockSpec((tm, V7X_LANES), lambda i: (i % nrt, 0))
    tok = lambda w: pl.BlockSpec((tm, w), lambda i: (i, 0))
    out_shape = [jax.ShapeDtypeStruct((t, hq), BF16), jax.ShapeDtypeStruct((t, V7X_LANES), F32)]
    out_specs = [tok(hq), tok(V7X_LANES)]
    if feature_major:
        b = t // rows_per_seq
        fm = pl.BlockSpec((1, kvw, tm), lambda i: (i // nrt, 0, i % nrt))
        out_shape += [jax.ShapeDtypeStruct((t, kvw), F32)] + [jax.ShapeDtypeStruct((b, kvw, rows_per_seq), F32)] * 3
        out_specs += [tok(kvw), fm, fm, fm]
    else:
        out_shape += [jax.ShapeDtypeStruct((t, kvw), F32)] * 3
        out_specs += [tok(kvw)] * 3
    return pl.pallas_call(
        functools.partial(_nsa_proj_kernel, feature_major=feature_major),
        out_shape=tuple(out_shape),
        grid=(t // tm,),
        in_specs=[tok(d), _mod_spec(mod3, tpb), _full((1, d)), _full((d, nw)), _full((d, V7X_LANES)),
                  rspec, rspec, rspec],
        out_specs=tuple(out_specs),
        compiler_params=_cparams("arbitrary"),
        name="nsa_proj",
    )(x, mod3, g, w_main, w_gate, *rope_t)


def _compress_rows(load_rows, bdk_ref, bdv_ref, pek_ref, pev_ref, nc, perm_scr):
    half = nc // 2
    n_chunks = 2 * KV_WIDTH // V7X_LANES
    outs = []
    for c in range(n_chunks):
        pe_ref, w_ref = (pek_ref, bdk_ref) if c < n_chunks // 2 else (pev_ref, bdv_ref)
        cols = [(load_rows(c, l) + pe_ref[l:l + 1, :]).astype(BF16) for l in range(CMP_BLOCK)]
        perm_scr[...] = _dot(jnp.concatenate(cols, axis=1), w_ref[...])
        outs.append(jnp.concatenate([perm_scr[pl.ds(0, half, stride=2), :], perm_scr[pl.ds(1, half, stride=2), :]],
                                    axis=0))
    return (jnp.concatenate(outs[0:n_chunks // 2], axis=1), jnp.concatenate(outs[n_chunks // 2:], axis=1))


def _compress_kernel(r0_ref, r1_ref, r2_ref, r3_ref, bdk_ref, bdv_ref, pek_ref, pev_ref, rc_ref, ra_ref, rb_ref,
                     ck_ref, cv_ref, perm_scr):
    nc = ck_ref.shape[1]

    chunk_refs = (r0_ref, r1_ref, r2_ref, r3_ref)

    def load_rows(c, l):
        return chunk_refs[c][0, pl.ds(l, nc, stride=CMP_BLOCK), :]

    acck, accv = _compress_rows(load_rows, bdk_ref, bdv_ref, pek_ref, pev_ref, nc, perm_scr)
    ck_ref[0] = _rope(acck, rc_ref[...], ra_ref[...], rb_ref[...]).astype(BF16)
    cv_ref[0] = accv.astype(BF16)


def _cmp_order_end_pos(nc):
    n = jnp.arange(nc)
    half = nc // 2
    blk = jnp.where(n < half, 2 * n, 2 * (n - half) + 1)
    return blk * CMP_BLOCK + (CMP_BLOCK - 1)


def _cmp_end_row(nc):
    half = nc // 2
    col = lax.broadcasted_iota(jnp.int32, (1, nc), 1)
    return jnp.where(col < half, 2 * CMP_BLOCK * col + (CMP_BLOCK - 1),
                     2 * CMP_BLOCK * (col - half) + (2 * CMP_BLOCK - 1))


def _compress(rows3, cw):
    b, tk, w = rows3.shape
    nc = tk // CMP_BLOCK
    assert nc % 2 == 0 and nc * CMP_BLOCK == tk
    rt = _rope_tables(_cmp_order_end_pos(nc), KV_WIDTH)
    return pl.pallas_call(
        _compress_kernel,
        out_shape=(jax.ShapeDtypeStruct((b, nc, KV_WIDTH), BF16),) * 2,
        grid=(b,),
        in_specs=[
            *[pl.BlockSpec((1, tk, V7X_LANES), functools.partial(lambda i, c: (i, 0, c), c=c))
              for c in range(w // V7X_LANES)],
            _full(cw["bdk"].shape), _full(cw["bdv"].shape), _full(cw["pek"].shape), _full(cw["pev"].shape),
            _full(rt[0].shape), _full(rt[1].shape), _full(rt[2].shape),
        ],
        out_specs=(pl.BlockSpec((1, nc, KV_WIDTH), lambda i: (i, 0, 0)),) * 2,
        scratch_shapes=[pltpu.VMEM((nc, V7X_LANES), F32)],
        compiler_params=_cparams("arbitrary"),
        name="compress",
    )(rows3, rows3, rows3, rows3, cw["bdk"], cw["bdv"], cw["pek"], cw["pev"], *rt)


def _compress_weights(w_ck, pe_ck, w_cv, pe_cv):
    pair = V7X_LANES // HEAD_DIM
    eye = jnp.eye(pair, dtype=F32)

    def bd(w):
        return jnp.einsum("gh,lde->lgdhe", eye, w).reshape(CMP_BLOCK * V7X_LANES, V7X_LANES).astype(BF16)

    return {"bdk": bd(w_ck), "bdv": bd(w_cv),
            "pek": jnp.tile(pe_ck, (1, pair)), "pev": jnp.tile(pe_cv, (1, pair))}


def _softmax_rows(s):
    m = jnp.max(s, axis=-1, keepdims=True)
    m = jnp.where(m == NEG_INF, 0.0, m)
    e = jnp.exp(s - m)
    return e / jnp.maximum(jnp.sum(e, axis=-1, keepdims=True), 1e-30)


def _masked_softmax(s, mask):
    return _softmax_rows(jnp.where(mask, s, NEG_INF))


def _select_blocks(imp, t_col, ns):
    blk = lax.broadcasted_iota(jnp.int32, (1, ns), 1)
    cur = t_col // SEL_BLOCK
    forced = jnp.where(blk == 0, 1.0, jnp.where(blk == cur, 1.0, 0.0))
    score = jnp.where(blk <= cur, imp + FORCE_BONUS * forced, NEG_INF)
    rank = jnp.zeros(score.shape, F32)
    for i in range(ns):
        ci = score[:, i:i + 1]
        later = jnp.where(blk > i, 1.0, 0.0)
        rank = rank + jnp.where(ci > score, 1.0, jnp.where(ci == score, later, 0.0))
    return jnp.where(rank < float(min(TOP_N, ns)), 1.0, 0.0)


def _select_blocks_t(imp_t, t_row, ns):
    blk = lax.broadcasted_iota(jnp.int32, (ns, 1), 0)
    cur = t_row // SEL_BLOCK
    forced = jnp.where(blk == 0, 1.0, jnp.where(blk == cur, 1.0, 0.0))
    score = jnp.where(blk <= cur, imp_t + FORCE_BONUS * forced, NEG_INF)
    rank = jnp.zeros(score.shape, F32)
    for i in range(ns):
        ci = score[i:i + 1, :]
        later = jnp.where(blk > i, 1.0, 0.0)
        rank = rank + jnp.where(ci > score, 1.0, jnp.where(ci == score, later, 0.0))
    return jnp.where(rank < float(min(TOP_N, ns)), 1.0, 0.0)


def _stack_heads(x, g):
    base = g * HEADS_PER_KV * HEAD_DIM
    return jnp.concatenate([x[:, base + r * HEAD_DIM: base + (r + 1) * HEAD_DIM] for r in range(HEADS_PER_KV)],
                           axis=0)


def _gate_cols(gates, g, branch):
    cols = [(g * HEADS_PER_KV + r) * N_BRANCH + branch for r in range(HEADS_PER_KV)]
    return jnp.concatenate([gates[:, c:c + 1] for c in cols], axis=0)


def _pad_to(x, rows, cols):
    r, c = x.shape
    if c < cols:
        x = jnp.concatenate([x, jnp.zeros((r, cols - c), x.dtype)], axis=1)
    if r < rows:
        x = jnp.concatenate([x, jnp.zeros((rows - r, x.shape[1]), x.dtype)], axis=0)
    return x


WIN_CHUNK = V7X_LANES
MASK_BIG = 2.0 ** 100


def _attn_prompt_kernel(q_ref, gate_ref, ck_ref, cv_ref, selt_ref, wint_ref, o_ref,
                        ks_scr, vs_scr, kw_scr, vw_scr, *, tk):
    tq = q_ref.shape[0]
    s_len = selt_ref.shape[2]
    nc = ck_ref.shape[1]
    half = nc // 2
    ns = s_len // SEL_BLOCK
    qi = pl.program_id(1)
    k_pad = ks_scr.shape[2] - HEAD_DIM - ns

    @pl.when(qi == 0)
    def _():
        for c in range(s_len // tk):
            kpos = c * tk + lax.broadcasted_iota(jnp.int32, (1, tk), 1)
            brow = lax.broadcasted_iota(jnp.int32, (ns + k_pad, 1), 0)
            expand = jnp.where(kpos // SEL_BLOCK == brow, 1.0, 0.0).astype(BF16)
            for g in range(KV_HEADS):
                k_g = selt_ref[0, g * HEAD_DIM:(g + 1) * HEAD_DIM, c * tk:(c + 1) * tk].astype(BF16)
                ks_scr[c, g] = jnp.concatenate([k_g, expand], axis=0)
            vs_scr[c] = selt_ref[0, KV_WIDTH:2 * KV_WIDTH, c * tk:(c + 1) * tk].astype(BF16)
        for c in range(s_len // WIN_CHUNK):
            kw_scr[c] = wint_ref[0, 0:KV_WIDTH, c * WIN_CHUNK:(c + 1) * WIN_CHUNK].astype(BF16)
            vw_scr[c] = wint_ref[0, KV_WIDTH:2 * KV_WIDTH, c * WIN_CHUNK:(c + 1) * WIN_CHUNK].astype(BF16)

    q0 = qi * tq
    t_tile = q0 + lax.broadcasted_iota(jnp.int32, (tq, 1), 0)
    t_row = q0 + lax.broadcasted_iota(jnp.int32, (1, tq), 1)
    t_col4 = jnp.concatenate([t_tile] * HEADS_PER_KV, axis=0)
    q_all = q_ref[...]
    gates = gate_ref[...]
    c_end = _cmp_end_row(nc)
    last = q0 // tk
    kpos_last = last * tk + lax.broadcasted_iota(jnp.int32, (1, tk), 1)
    causal = jnp.where(kpos_last <= t_tile, 0.0, -MASK_BIG)
    causal4 = jnp.concatenate([causal] * HEADS_PER_KV, axis=0)

    wlen = min(WINDOW + tq, s_len)
    n_wc = wlen // WIN_CHUNK
    w0 = jnp.clip(q0 + tq - wlen, 0, s_len - wlen)
    wc0 = w0 // WIN_CHUNK
    wpos = w0 + lax.broadcasted_iota(jnp.int32, (1, wlen), 1)
    wbias = jnp.where(wpos <= t_tile, jnp.where(wpos >= t_tile - WINDOW, 0.0, NEG_INF), NEG_INF)
    wbias4 = jnp.concatenate([wbias] * HEADS_PER_KV, axis=0)

    for g in range(KV_HEADS):
        rows_g = slice(g * HEAD_DIM, (g + 1) * HEAD_DIM)
        q4 = _stack_heads(q_all, g)
        p_c = _masked_softmax(_dot_nt(q4, ck_ref[0, :, rows_g]), c_end <= t_col4)
        o_c = _dot(p_c.astype(BF16), cv_ref[0, :, rows_g])
        psum = p_c[0:tq]
        for r in range(1, HEADS_PER_KV):
            psum = psum + p_c[r * tq:(r + 1) * tq]
        psum_t = _pad_to(psum, tq, V7X_LANES).T
        imp_t = psum_t[0:half] + psum_t[half:nc]
        sel_t = _select_blocks_t(imp_t, t_row, ns)
        sel = _pad_to(sel_t, V7X_LANES, tq).T[:, 0:ns + k_pad]
        unchosen = ((sel - 1.0) * MASK_BIG).astype(BF16)
        q_ext = jnp.concatenate([q4, jnp.concatenate([unchosen] * HEADS_PER_KV, axis=0)], axis=1)

        kw = jnp.concatenate([kw_scr[wc0 + i, rows_g, :] for i in range(n_wc)], axis=1)
        vw = jnp.concatenate([vw_scr[wc0 + i, rows_g, :] for i in range(n_wc)], axis=1)

        def attend(c, carry, bias):
            m, l, acc = carry
            s = _dot(q_ext, ks_scr[c, g])
            if bias is not None:
                s = s + bias
            m_new = jnp.maximum(m, jnp.max(s, axis=-1, keepdims=True))
            alpha = jnp.exp(m - m_new)
            p = jnp.exp(s - m_new)
            l = alpha * l + jnp.sum(p, axis=-1, keepdims=True)
            acc = alpha * acc + _dot_nt(p.astype(BF16), vs_scr[c, rows_g, :])
            return m_new, l, acc

        rows = HEADS_PER_KV * tq
        init = (jnp.full((rows, 1), NEG_INF, F32), jnp.zeros((rows, 1), F32), jnp.zeros((rows, HEAD_DIM), F32))
        carry = lax.fori_loop(0, last, lambda c, carry: attend(c, carry, None), init)
        _, l_s, acc_s = attend(last, carry, causal4)
        o_s = acc_s / jnp.maximum(l_s, 1e-30)

        p_w = _softmax_rows(_dot(q4, kw) + wbias4)
        o_w = _dot_nt(p_w.astype(BF16), vw)

        o = _gate_cols(gates, g, 0) * o_c + _gate_cols(gates, g, 1) * o_s + _gate_cols(gates, g, 2) * o_w
        for r in range(HEADS_PER_KV):
            head = g * HEADS_PER_KV + r
            o_ref[:, head * HEAD_DIM:(head + 1) * HEAD_DIM] = o[r * tq:(r + 1) * tq].astype(BF16)


def _attn_prompt(q, gates, ck, cv, selt, wint, tq):
    b, kvw, s_len = selt.shape
    nc = ck.shape[1]
    hq = N_HEADS * HEAD_DIM
    nq = s_len // tq
    tk = min(512, s_len)
    assert s_len % tk == 0 and s_len % tq == 0 and tk % tq == 0 and tq % V7X_LANES == 0 and nc <= V7X_LANES
    assert min(WINDOW + tq, s_len) % WIN_CHUNK == 0 and HEAD_DIM + s_len // SEL_BLOCK <= V7X_LANES
    return pl.pallas_call(
        functools.partial(_attn_prompt_kernel, tk=tk),
        out_shape=jax.ShapeDtypeStruct((b * s_len, hq), BF16),
        grid=(b, nq),
        in_specs=[
            pl.BlockSpec((tq, hq), lambda i, j: (i * nq + j, 0)),
            pl.BlockSpec((tq, V7X_LANES), lambda i, j: (i * nq + j, 0)),
            pl.BlockSpec((1, nc, KV_WIDTH), lambda i, j: (i, 0, 0)),
            pl.BlockSpec((1, nc, KV_WIDTH), lambda i, j: (i, 0, 0)),
            pl.BlockSpec((1, kvw, s_len), lambda i, j: (i, 0, 0)),
            pl.BlockSpec((1, kvw, s_len), lambda i, j: (i, 0, 0)),
        ],
        out_specs=pl.BlockSpec((tq, hq), lambda i, j: (i * nq + j, 0)),
        scratch_shapes=[
            pltpu.VMEM((s_len // tk, KV_HEADS, V7X_LANES, tk), BF16),
            pltpu.VMEM((s_len // tk, KV_WIDTH, tk), BF16),
            pltpu.VMEM((s_len // WIN_CHUNK, KV_WIDTH, WIN_CHUNK), BF16),
            pltpu.VMEM((s_len // WIN_CHUNK, KV_WIDTH, WIN_CHUNK), BF16),
        ],
        compiler_params=_cparams("arbitrary", "arbitrary"),
        name="attn_prompt",
    )(q, gates, ck, cv, selt, wint)


PAGES_PER_STEP = 8
CMP_PITCH = CMP_BLOCK + V7X_SUBLANES
DECODE_TOK_PAD = V7X_SUBLANES


def _block_diag_q(q):
    parts = []
    for g in range(KV_HEADS):
        q4 = _stack_heads(q, g)
        z = jnp.zeros_like(q4)
        parts.append(jnp.concatenate([q4 if gg == g else z for gg in range(KV_HEADS)], axis=1))
    return jnp.concatenate(parts, axis=0)


def _attn_decode_kernel(pt_ref, q_ref, gate_ref, nsel_ref, nwin_ref, wbuf_ref, bdk_ref, bdv_ref, pek_ref, pev_ref,
                        rc_ref, ra_ref, rb_ref, *rest, page, n_pages):
    pps = PAGES_PER_STEP
    cpages = rest[:pps]
    spages = rest[pps:2 * pps]
    o_ref = rest[2 * pps]
    cmp_scr, s_scr, vs_scr, qbd_scr, perm_scr = rest[2 * pps + 1:]
    blocks_per_page = page // CMP_BLOCK
    del pt_ref
    j = pl.program_id(1)
    tq = q_ref.shape[1]
    past_len = n_pages * page
    n_lane_chunks = 2 * KV_WIDTH // V7X_LANES

    @pl.when(j == 0)
    def _():
        qbd_scr[...] = _block_diag_q(q_ref[0].astype(F32)).astype(BF16)

    qbd = qbd_scr[...]
    for p in range(pps):
        for c in range(n_lane_chunks):
            rows_pc = cpages[p][0, 0, c * V7X_LANES:(c + 1) * V7X_LANES, :].T
            for n in range(blocks_per_page):
                r0 = pl.multiple_of(((j * pps + p) * blocks_per_page + n) * CMP_PITCH, V7X_SUBLANES)
                cmp_scr[c, pl.ds(r0, CMP_BLOCK), :] = rows_pc[n * CMP_BLOCK:(n + 1) * CMP_BLOCK]
    k_step = jnp.concatenate([spages[p][0, 0, 0:KV_WIDTH, :] for p in range(pps)], axis=1).astype(BF16)
    s_scr[j] = _dot(qbd, k_step)
    vs_scr[j] = jnp.concatenate([spages[p][0, 0, KV_WIDTH:2 * KV_WIDTH, :] for p in range(pps)],
                                axis=1).astype(BF16)

    @pl.when(j == pl.num_programs(1) - 1)
    def _():
        tok = lax.broadcasted_iota(jnp.int32, (tq, 1), 0)
        t_rows = past_len + jnp.concatenate([tok] * N_HEADS, axis=0)
        nsel = _pad_to(nsel_ref[0], page, 2 * KV_WIDTH)
        s_new = _dot_nt(qbd, nsel[:, 0:KV_WIDTH].astype(BF16))

        nc = past_len // CMP_BLOCK
        half = nc // 2

        def load_rows(c, l):
            return cmp_scr[c, pl.ds(l, nc, stride=CMP_PITCH), :]

        acck, accv = _compress_rows(load_rows, bdk_ref, bdv_ref, pek_ref, pev_ref, nc, perm_scr)
        ck = _rope(acck, rc_ref[...], ra_ref[...], rb_ref[...]).astype(BF16)
        p_c = _masked_softmax(_dot_nt(qbd, ck), _cmp_end_row(nc) <= t_rows)
        o_c = _dot(p_c.astype(BF16), accv.astype(BF16))

        ns = n_pages * page // SEL_BLOCK + 1
        grp = HEADS_PER_KV * tq
        psum = []
        for g in range(KV_HEADS):
            pg_ = p_c[g * grp:g * grp + tq]
            for r in range(1, HEADS_PER_KV):
                pg_ = pg_ + p_c[g * grp + r * tq:g * grp + (r + 1) * tq]
            psum.append(pg_)
        psum = jnp.concatenate(psum, axis=0)
        imp = jnp.concatenate([psum[:, 0:half] + psum[:, half:nc], jnp.zeros((KV_HEADS * tq, 1), F32)], axis=1)
        selm = _select_blocks(imp, past_len + jnp.concatenate([tok] * KV_HEADS, axis=0), ns)
        selm_rows = jnp.concatenate(
            [selm[g * tq:(g + 1) * tq] for g in range(KV_HEADS) for _ in range(HEADS_PER_KV)], axis=0)
        nkeys = (n_pages + 1) * page
        kpos = lax.broadcasted_iota(jnp.int32, (1, nkeys), 1)
        brow = lax.broadcasted_iota(jnp.int32, (ns, 1), 0)
        expand = jnp.where(kpos // SEL_BLOCK == brow, 1.0, 0.0).astype(BF16)
        chosen = _dot(selm_rows.astype(BF16), expand)
        n_steps = n_pages // pps
        s_all = jnp.concatenate([s_scr[i] for i in range(n_steps)] + [s_new], axis=1)
        s_all = jnp.where(kpos <= t_rows, jnp.where(chosen > 0.5, s_all, NEG_INF), NEG_INF)
        p_s = _softmax_rows(s_all).astype(BF16)
        vt_all = jnp.concatenate([vs_scr[i] for i in range(n_steps)], axis=1)
        o_s = _dot_nt(p_s[:, 0:past_len], vt_all) + _dot(p_s[:, past_len:nkeys], nsel[:, KV_WIDTH:].astype(BF16))

        wb = wbuf_ref.shape[3]
        nwin = _pad_to(nwin_ref[0], page, 2 * KV_WIDTH)
        s_w = jnp.concatenate([_dot(qbd, wbuf_ref[0, 0, 0:KV_WIDTH, :].astype(BF16)),
                               _dot_nt(qbd, nwin[:, 0:KV_WIDTH].astype(BF16))], axis=1)
        wpos = past_len - wb + lax.broadcasted_iota(jnp.int32, (1, wb + page), 1)
        s_w = jnp.where(wpos <= t_rows, jnp.where(wpos >= t_rows - WINDOW, s_w, NEG_INF), NEG_INF)
        p_w = _softmax_rows(s_w).astype(BF16)
        o_w = (_dot_nt(p_w[:, 0:wb], wbuf_ref[0, 0, KV_WIDTH:2 * KV_WIDTH, :].astype(BF16))
               + _dot(p_w[:, wb:wb + page], nwin[:, KV_WIDTH:].astype(BF16)))

        gates = gate_ref[0]
        gcol = lambda br: jnp.concatenate([_gate_cols(gates, g, br) for g in range(KV_HEADS)], axis=0)
        o = gcol(0) * o_c + gcol(1) * o_s + gcol(2) * o_w
        for g in range(KV_HEADS):
            for r in range(HEADS_PER_KV):
                col = (g * HEADS_PER_KV + r) * HEAD_DIM
                o_ref[0, :, col:col + HEAD_DIM] = o[g * grp + r * tq:g * grp + (r + 1) * tq,
                                                    g * HEAD_DIM:(g + 1) * HEAD_DIM]


def _attn_decode(page_table, q3, gates3, nsel3, nwin3, cache_cmp_t, cache_sel_t, cache_win_t, layer, cw):
    b, tq, hq = q3.shape
    n_pages = page_table.shape[1]
    page = cache_cmp_t.shape[3]
    kvw = 2 * KV_WIDTH
    pps = PAGES_PER_STEP
    assert n_pages % pps == 0 and (n_pages * page) % (2 * CMP_BLOCK) == 0 and tq < CMP_BLOCK
    assert tq % V7X_SUBLANES == 0 and page == V7X_LANES
    nc = n_pages * page // CMP_BLOCK
    rt = _rope_tables(_cmp_order_end_pos(nc), KV_WIDTH)
    wb = cache_win_t.shape[3]
    full = lambda shp: pl.BlockSpec(shp, lambda *_: (0,) * len(shp))
    per_seq = lambda shp: pl.BlockSpec((1,) + shp, lambda i, j, pt: (i,) + (0,) * len(shp))
    page_spec = lambda p: pl.BlockSpec((1, 1, kvw, page), lambda i, j, pt: (layer, pt[i, j * pps + p], 0, 0))
    rows = N_HEADS * tq
    grid_spec = pltpu.PrefetchScalarGridSpec(
        num_scalar_prefetch=1,
        grid=(b, n_pages // pps),
        in_specs=[
            per_seq((tq, hq)), per_seq((tq, V7X_LANES)), per_seq((tq, kvw)), per_seq((tq, kvw)),
            pl.BlockSpec((1, 1, kvw, wb), lambda i, j, pt: (layer, i, 0, 0)),
            full(cw["bdk"].shape), full(cw["bdv"].shape), full(cw["pek"].shape), full(cw["pev"].shape),
            full(rt[0].shape), full(rt[1].shape), full(rt[2].shape),
            *[page_spec(p) for p in range(pps)],
            *[page_spec(p) for p in range(pps)],
        ],
        out_specs=per_seq((tq, hq)),
        scratch_shapes=[
            pltpu.VMEM((kvw // V7X_LANES, nc * CMP_PITCH, V7X_LANES), F32),
            pltpu.VMEM((n_pages // pps, rows, pps * page), F32),
            pltpu.VMEM((n_pages // pps, KV_WIDTH, pps * page), BF16),
            pltpu.VMEM((rows, KV_WIDTH), BF16),
            pltpu.VMEM((nc, V7X_LANES), F32),
        ],
    )
    return pl.pallas_call(
        functools.partial(_attn_decode_kernel, page=page, n_pages=n_pages),
        out_shape=jax.ShapeDtypeStruct((b, tq, hq), F32),
        grid_spec=grid_spec,
        compiler_params=_cparams("arbitrary", "arbitrary"),
        name="attn_decode",
    )(page_table, q3, gates3, nsel3, nwin3, cache_win_t, cw["bdk"], cw["bdv"], cw["pek"], cw["pev"], *rt,
      *([cache_cmp_t] * pps), *([cache_sel_t] * pps))


def _out_proj_kernel(o_ref, w_ref, x_ref, mod_ref, xo_ref):
    d = x_ref.shape[1]
    xo_ref[...] = x_ref[...] + mod_ref[0][:, 2 * d:3 * d] * _dot(o_ref[...], w_ref[...])


def _out_proj(o, w_bf16, x, mod3, tpb, tm):
    t, d = x.shape
    k = o.shape[1]
    return pl.pallas_call(
        _out_proj_kernel,
        out_shape=jax.ShapeDtypeStruct((t, d), F32),
        grid=(t // tm,),
        in_specs=[
            pl.BlockSpec((tm, k), lambda i: (i, 0)),
            pl.BlockSpec((k, d), lambda i: (0, 0)),
            pl.BlockSpec((tm, d), lambda i: (i, 0)),
            _mod_spec(mod3, tpb),
        ],
        out_specs=pl.BlockSpec((tm, d), lambda i: (i, 0)),
        compiler_params=_cparams("arbitrary"),
        name="out_proj",
    )(o, w_bf16, x, mod3)


def _ffn_kernel(x_ref, mod_ref, g_ref, wg_ref, wu_ref, wd_ref, o_ref, h_scr):
    d = x_ref.shape[1]
    f = pl.program_id(1)

    @pl.when(f == 0)
    def _():
        mod = mod_ref[0]
        h_scr[...] = _norm_mod(x_ref[...], g_ref[...], mod[:, 4 * d:5 * d], mod[:, 3 * d:4 * d]).astype(BF16)
        o_ref[...] = jnp.zeros(o_ref.shape, F32)

    h = h_scr[...]
    a = _dot(h, wg_ref[0].astype(BF16))
    u = _dot(h, wu_ref[0].astype(BF16))
    act = (a * _sigmoid(a) * u).astype(BF16)
    o_ref[...] += _dot(act, wd_ref[0].astype(BF16))

    @pl.when(f == pl.num_programs(1) - 1)
    def _():
        o_ref[...] = x_ref[...] + mod_ref[0][:, 5 * d:6 * d] * o_ref[...]


def _ffn(x, mod3, tpb, g, w_gu_all, w_down_all, layer, tm, tf):
    t, d = x.shape
    ff = w_down_all.shape[1]
    nf = ff // tf
    return pl.pallas_call(
        _ffn_kernel,
        out_shape=jax.ShapeDtypeStruct((t, d), F32),
        grid=(t // tm, nf),
        in_specs=[
            pl.BlockSpec((tm, d), lambda i, f: (i, 0)),
            _mod_spec(mod3, tpb),
            pl.BlockSpec((1, d), lambda i, f: (0, 0)),
            pl.BlockSpec((1, d, tf), lambda i, f: (layer, 0, f)),
            pl.BlockSpec((1, d, tf), lambda i, f: (layer, 0, nf + f)),
            pl.BlockSpec((1, tf, d), lambda i, f: (layer, f, 0)),
        ],
        out_specs=pl.BlockSpec((tm, d), lambda i, f: (i, 0)),
        scratch_shapes=[pltpu.VMEM((tm, d), BF16)],
        compiler_params=_cparams("arbitrary", "arbitrary"),
        name="ffn",
    )(x, mod3, g, w_gu_all, w_gu_all, w_down_all)


def _conv_in(x_ref, mod_ref, g_ref, win_ref):
    d = x_ref.shape[1]
    mod = mod_ref[0]
    h = _norm_mod(x_ref[...], g_ref[...], mod[:, d:2 * d], mod[:, 0:d]).astype(BF16)
    bg = _dot(h, win_ref[:, 0:d])
    u = _dot(h, win_ref[:, d:2 * d]) * _dot(h, win_ref[:, 2 * d:3 * d])
    return bg, u


def _conv_out(bg, conv, x_ref, mod_ref, wout_ref, xo_ref):
    d = x_ref.shape[1]
    y = _dot((bg * conv).astype(BF16), wout_ref[...])
    xo_ref[...] = x_ref[...] + mod_ref[0][:, 2 * d:3 * d] * y


def _conv_seq_kernel(x_ref, mod_ref, g_ref, win_ref, wc_ref, wout_ref, st_ref, xo_ref, nst_ref, carry_scr):
    tm = x_ref.shape[0]
    bg, u = _conv_in(x_ref, mod_ref, g_ref, win_ref)

    @pl.when(pl.program_id(1) == 0)
    def _():
        carry_scr[6:8, :] = st_ref[0]

    ue = jnp.concatenate([carry_scr[...], u], axis=0)
    conv = wc_ref[0:1, :] * ue[6:6 + tm] + wc_ref[1:2, :] * ue[7:7 + tm] + wc_ref[2:3, :] * u
    carry_scr[...] = u[tm - 8:tm]
    nst_ref[0] = u[tm - 2:tm]
    _conv_out(bg, conv, x_ref, mod_ref, wout_ref, xo_ref)


def _conv_tok_kernel(x_ref, mod_ref, g_ref, win_ref, wc_ref, wout_ref, p1_ref, p2_ref, xo_ref, u_ref, *, t_seq):
    tm = x_ref.shape[0]
    bg, u = _conv_in(x_ref, mod_ref, g_ref, win_ref)
    row = lax.broadcasted_iota(jnp.int32, (tm, 1), 0) % t_seq
    prev1 = jnp.where(row >= 1, pltpu.roll(u, 1, 0), p1_ref[...])
    prev2 = jnp.where(row >= 2, pltpu.roll(u, 2, 0), p2_ref[...])
    conv = wc_ref[0:1, :] * prev2 + wc_ref[1:2, :] * prev1 + wc_ref[2:3, :] * u
    u_ref[...] = u
    _conv_out(bg, conv, x_ref, mod_ref, wout_ref, xo_ref)


def _conv_mixer(x, mod3, tpb, g, w_in, w_conv, w_out, state, b, t_seq, tm):
    t, d = x.shape
    weights = [_full((1, d)), _full(w_in.shape), _full(w_conv.shape), _full(w_out.shape)]
    if state is None:
        state = jnp.zeros((b, 2, d), F32)
    if t_seq >= tm:
        nt = t_seq // tm
        xo, nst = pl.pallas_call(
            _conv_seq_kernel,
            out_shape=(jax.ShapeDtypeStruct((t, d), F32), jax.ShapeDtypeStruct((b, 2, d), F32)),
            grid=(b, nt),
            in_specs=[pl.BlockSpec((tm, d), lambda i, j: (i * nt + j, 0)),
                      pl.BlockSpec((1, 1, mod3.shape[-1]), lambda i, j: (i, 0, 0)),
                      *weights,
                      pl.BlockSpec((1, 2, d), lambda i, j: (i, 0, 0))],
            out_specs=(pl.BlockSpec((tm, d), lambda i, j: (i * nt + j, 0)),
                       pl.BlockSpec((1, 2, d), lambda i, j: (i, 0, 0))),
            scratch_shapes=[pltpu.VMEM((8, d), F32)],
            compiler_params=_cparams("arbitrary", "arbitrary"),
            name="conv_seq",
        )(x, mod3, g, w_in, w_conv, w_out, state)
        return xo, nst
    assert t == tm and t_seq >= 2
    p1 = jnp.zeros((b, t_seq, d), F32).at[:, 0].set(state[:, 1]).reshape(t, d)
    p2 = jnp.zeros((b, t_seq, d), F32).at[:, 0].set(state[:, 0]).at[:, 1].set(state[:, 1]).reshape(t, d)
    xo, u = pl.pallas_call(
        functools.partial(_conv_tok_kernel, t_seq=t_seq),
        out_shape=(jax.ShapeDtypeStruct((t, d), F32), jax.ShapeDtypeStruct((t, d), F32)),
        grid=(1,),
        in_specs=[_full((t, d)), _full(mod3.shape), *weights, _full((t, d)), _full((t, d))],
        out_specs=(_full((t, d)), _full((t, d))),
        compiler_params=_cparams("arbitrary"),
        name="conv_tok",
    )(x, mod3, g, w_in, w_conv, w_out, p1, p2)
    return xo, u.reshape(b, t_seq, d)[:, t_seq - 2:]


MOE_ROW_UNIT = 128
MOE_BOUNDS_ROWS = 16


def _split_bf16(x):
    hi = x.astype(BF16)
    return hi, (x - hi.astype(F32)).astype(BF16)


def _moe_route_kernel(x_ref, mod_ref, g_ref, wr_ref, br_ref, h_ref, pos_ref, comb_ref, post_ref, bnd_ref, *, blk):
    tm, d = x_ref.shape
    nb = tm // blk
    mod = mod_ref[0]
    h32 = _norm_mod(x_ref[...], g_ref[...], mod[:, 4 * d:5 * d], mod[:, 3 * d:4 * d])
    h_ref[...] = h32.astype(BF16)
    lane = lax.broadcasted_iota(jnp.int32, (1, V7X_LANES), 1)
    h_hi, h_lo = _split_bf16(h32)
    w_hi, w_lo = _split_bf16(wr_ref[...])
    logits = _dot(h_hi, w_hi) + (_dot(h_hi, w_lo) + _dot(h_lo, w_hi)) + br_ref[...]
    logits = jnp.where(lane < N_EXPERTS, logits, NEG_INF)
    m1 = jnp.max(logits, axis=-1, keepdims=True)
    i1 = jnp.min(jnp.where(logits == m1, lane, V7X_LANES), axis=-1, keepdims=True)
    rest = jnp.where(lane == i1, NEG_INF, logits)
    m2 = jnp.max(rest, axis=-1, keepdims=True)
    i2 = jnp.min(jnp.where(rest == m2, lane, V7X_LANES), axis=-1, keepdims=True)
    e2 = jnp.exp(m2 - m1)
    den = 1.0 + e2
    comb_ref[...] = jnp.where(lane == i1, 1.0 / den, jnp.where(lane == i2, e2 / den, 0.0))
    mask = jnp.where(lane == i1, 1.0, jnp.where(lane == i2, 1.0, 0.0))
    r = lax.broadcasted_iota(jnp.int32, (blk, blk), 0)
    c = lax.broadcasted_iota(jnp.int32, (blk, blk), 1)
    lower = jnp.where(c < r, 1.0, 0.0).astype(BF16)
    upper = jnp.where(r < c, 1.0, 0.0).astype(BF16)
    carry = jnp.zeros((1, V7X_LANES), F32)
    bnd_ref[0] = jnp.zeros((MOE_BOUNDS_ROWS, V7X_LANES), F32)
    for k in range(nb):
        mb = mask[k * blk:(k + 1) * blk]
        pos_ref[k * blk:(k + 1) * blk, :] = jnp.where(mb > 0.5, _dot(lower, mb.astype(BF16)) + carry, -1.0)
        carry = carry + jnp.sum(mb, axis=0, keepdims=True)
        bnd_ref[0, k + 1:k + 2, :] = carry
    eye = jnp.where(lax.broadcasted_iota(jnp.int32, (V7X_LANES, V7X_LANES), 0)
                    == lax.broadcasted_iota(jnp.int32, (V7X_LANES, V7X_LANES), 1), 1.0, 0.0).astype(BF16)
    mask_t = _dot_nt(eye, mask.astype(BF16))[0:N_EXPERTS]
    carry_t = jnp.zeros((N_EXPERTS, 1), F32)
    for k in range(nb):
        mb = mask_t[:, k * blk:(k + 1) * blk]
        post_ref[0, k] = jnp.where(mb > 0.5, _dot(mb.astype(BF16), upper) + carry_t, -1.0)
        carry_t = carry_t + jnp.sum(mb, axis=1, keepdims=True)


def _moe_route(x, mod3, tpb, g, w_router, b_router, tm, blk):
    t, d = x.shape
    ne = w_router.shape[1]
    nt = t // tm
    nb = tm // blk
    assert nb + 1 <= MOE_BOUNDS_ROWS
    wr = jnp.pad(w_router, ((0, 0), (0, V7X_LANES - ne)))
    br = jnp.pad(b_router, (0, V7X_LANES - ne)).reshape(1, V7X_LANES)
    tok = lambda w: pl.BlockSpec((tm, w), lambda i: (i, 0))
    return pl.pallas_call(
        functools.partial(_moe_route_kernel, blk=blk),
        out_shape=(
            jax.ShapeDtypeStruct((t, d), BF16),
            jax.ShapeDtypeStruct((t, V7X_LANES), F32),
            jax.ShapeDtypeStruct((t, V7X_LANES), F32),
            jax.ShapeDtypeStruct((nt, nb, N_EXPERTS, blk), F32),
            jax.ShapeDtypeStruct((nt, MOE_BOUNDS_ROWS, V7X_LANES), F32),
        ),
        grid=(nt,),
        in_specs=[tok(d), _mod_spec(mod3, tpb), _full((1, d)), _full((d, V7X_LANES)), _full((1, V7X_LANES))],
        out_specs=(tok(d), tok(V7X_LANES), tok(V7X_LANES),
                   pl.BlockSpec((1, nb, N_EXPERTS, blk), lambda i: (i, 0, 0, 0)),
                   pl.BlockSpec((1, MOE_BOUNDS_ROWS, V7X_LANES), lambda i: (i, 0, 0))),
        compiler_params=_cparams("arbitrary"),
        name="moe_route",
    )(x, mod3, g, wr, br)


def _moe_expert_kernel(bnd_ref, h_ref, pos_ref, comb_ref, post_ref, wg_ref, wu_ref, wd_ref, o_ref,
                       hc_scr, acc_scr, posc_scr, combc_scr, *, blk):
    tm, d = h_ref.shape
    nb = tm // blk
    unit = MOE_ROW_UNIT
    i = pl.program_id(0)
    e = pl.program_id(1)
    f = pl.program_id(2)
    nf = pl.num_programs(2)
    base = (i * N_EXPERTS + e) * MOE_BOUNDS_ROWS
    cnt = bnd_ref[base + nb]
    n_units = (cnt + unit - 1) // unit

    def span(u):
        start = u * unit
        stop = jnp.minimum(start + unit, cnt)
        lo = jnp.int32(0)
        hi = jnp.int32(0)
        for k in range(1, nb):
            bk = bnd_ref[base + k]
            lo = lo + (bk <= start).astype(jnp.int32)
            hi = hi + (bk < stop).astype(jnp.int32)
        return lo, hi + 1

    @pl.when((e == 0) & (f == 0))
    def _():
        o_ref[...] = jnp.zeros(o_ref.shape, F32)

    @pl.when(f == 0)
    def _():
        def per_unit(u, _):
            lo, hi = span(u)
            r0 = pl.multiple_of(u * unit, unit)
            want = (r0 + lax.broadcasted_iota(jnp.int32, (unit, 1), 0)).astype(F32)
            acc_scr[pl.ds(r0, unit), :] = jnp.zeros((unit, d), F32)

            def per_blk(k, _):
                t0 = pl.multiple_of(k * blk, blk)
                onehot = jnp.where(post_ref[0, k, pl.ds(e, 1), :] == want, 1.0, 0.0).astype(BF16)
                acc_scr[pl.ds(r0, unit), :] += _dot(onehot, h_ref[pl.ds(t0, blk), :])
                return 0

            lax.fori_loop(lo, hi, per_blk, 0)
            hc_scr[pl.ds(r0, unit), :] = acc_scr[pl.ds(r0, unit), :].astype(BF16)
            acc_scr[pl.ds(r0, unit), :] = jnp.zeros((unit, d), F32)
            return 0

        lax.fori_loop(0, n_units, per_unit, 0)

    def ffn_rows(r0, m):
        hc = hc_scr[pl.ds(r0, m), :]
        a = _dot(hc, wg_ref[0, 0])
        act = (a * _sigmoid(a) * _dot(hc, wu_ref[0, 0])).astype(BF16)
        acc_scr[pl.ds(r0, m), :] += _dot(act, wd_ref[0, 0])

    n_quads = n_units // 4
    rem = n_units - 4 * n_quads

    def quad(k, _):
        ffn_rows(pl.multiple_of(k * 4 * unit, 4 * unit), 4 * unit)
        return 0

    lax.fori_loop(0, n_quads, quad, 0)
    r_tail = pl.multiple_of(n_quads * 4 * unit, 4 * unit)

    @pl.when(rem >= 2)
    def _():
        ffn_rows(r_tail, 2 * unit)

    @pl.when(rem % 2 == 1)
    def _():
        ffn_rows(pl.multiple_of(r_tail + (rem // 2) * 2 * unit, unit), unit)

    @pl.when(f == nf - 1)
    def _():
        lane = lax.broadcasted_iota(jnp.int32, (1, V7X_LANES), 1)
        posc_scr[...] = jnp.sum(jnp.where(lane == e, pos_ref[...], 0.0), axis=-1, keepdims=True)
        combc_scr[...] = jnp.sum(jnp.where(lane == e, comb_ref[...], 0.0), axis=-1, keepdims=True)

        def per_unit(u, _):
            lo, hi = span(u)
            r0 = pl.multiple_of(u * unit, unit)
            want = (r0 + lax.broadcasted_iota(jnp.int32, (1, unit), 1)).astype(F32)
            y = acc_scr[pl.ds(r0, unit), :].astype(BF16)

            def per_blk(k, _):
                t0 = pl.multiple_of(k * blk, blk)
                onehot = jnp.where(posc_scr[pl.ds(t0, blk), :] == want, 1.0, 0.0).astype(BF16)
                o_ref[pl.ds(t0, blk), :] += combc_scr[pl.ds(t0, blk), :] * _dot(onehot, y)
                return 0

            lax.fori_loop(lo, hi, per_blk, 0)
            return 0

        lax.fori_loop(0, n_units, per_unit, 0)


def _moe_experts(h, pos, comb, post, bounds, w_gu_all, w_down_all, layer, tm, tf, blk):
    t, d = h.shape
    ne, ff = w_down_all.shape[1], w_down_all.shape[2]
    nf = ff // tf
    nt = t // tm
    nb = tm // blk
    bnd = jnp.transpose(bounds[:, :, :ne], (0, 2, 1)).astype(jnp.int32).reshape(nt * ne * MOE_BOUNDS_ROWS)
    once = pl.Buffered(1)
    tok = lambda w: pl.BlockSpec((tm, w), lambda i, e, f, b_: (i, 0), pipeline_mode=once)
    grid_spec = pltpu.PrefetchScalarGridSpec(
        num_scalar_prefetch=1,
        grid=(nt, ne, nf),
        in_specs=[
            tok(d), tok(V7X_LANES), tok(V7X_LANES),
            pl.BlockSpec((1, nb, ne, blk), lambda i, e, f, b_: (i, 0, 0, 0)),
            pl.BlockSpec((1, 1, d, tf), lambda i, e, f, b_: (layer, e, 0, f)),
            pl.BlockSpec((1, 1, d, tf), lambda i, e, f, b_: (layer, e, 0, nf + f)),
            pl.BlockSpec((1, 1, tf, d), lambda i, e, f, b_: (layer, e, f, 0)),
        ],
        out_specs=pl.BlockSpec((tm, d), lambda i, e, f, b_: (i, 0), pipeline_mode=once),
        scratch_shapes=[
            pltpu.VMEM((tm, d), BF16),
            pltpu.VMEM((tm, d), F32),
            pltpu.VMEM((tm, 1), F32),
            pltpu.VMEM((tm, 1), F32),
        ],
    )
    return pl.pallas_call(
        functools.partial(_moe_expert_kernel, blk=blk),
        out_shape=jax.ShapeDtypeStruct((t, d), F32),
        grid_spec=grid_spec,
        compiler_params=_cparams("arbitrary", "arbitrary", "arbitrary"),
        name="moe_experts",
    )(bnd, h, pos, comb, post, w_gu_all, w_gu_all, w_down_all)


def _residual_kernel(x_ref, f_ref, mod_ref, g_ref, o_ref, *, final_norm):
    d = x_ref.shape[1]
    x = x_ref[...] + mod_ref[0][:, 5 * d:6 * d] * f_ref[...]
    if final_norm:
        x = (x * lax.rsqrt(jnp.mean(x * x, axis=-1, keepdims=True) + EPS)) * g_ref[...]
    o_ref[...] = x


def _residual(x, f, mod3, tpb, g_final, tm, final_norm):
    t, d = x.shape
    tok = pl.BlockSpec((tm, d), lambda i: (i, 0))
    return pl.pallas_call(
        functools.partial(_residual_kernel, final_norm=final_norm),
        out_shape=jax.ShapeDtypeStruct((t, d), F32),
        grid=(t // tm,),
        in_specs=[tok, tok, _mod_spec(mod3, tpb), _full((1, d))],
        out_specs=tok,
        compiler_params=_cparams("arbitrary"),
        name="residual",
    )(x, f, mod3, g_final)


def _moe(x, mod3_of, g, w_router, b_router, w_gu_all, w_down_all, layer, g_final, tm, final_norm):
    blk = min(256, tm)
    h, pos, comb, post, bounds = _moe_route(x, *mod3_of(tm), g, w_router, b_router, tm, blk)
    f = _moe_experts(h, pos, comb, post, bounds, w_gu_all, w_down_all, layer, tm, 512, blk)
    tm_res = min(1024, tm)
    return _residual(x, f, *mod3_of(tm_res), g_final, tm_res, final_norm)


def _final_norm_kernel(x_ref, g_ref, o_ref):
    x = x_ref[...]
    o_ref[...] = (x * lax.rsqrt(jnp.mean(x * x, axis=-1, keepdims=True) + EPS)) * g_ref[...]


def _final_norm(x, g, tm):
    t, d = x.shape
    return pl.pallas_call(
        _final_norm_kernel,
        out_shape=jax.ShapeDtypeStruct((t, d), F32),
        grid=(t // tm,),
        in_specs=[pl.BlockSpec((tm, d), lambda i: (i, 0)), pl.BlockSpec((1, d), lambda i: (0, 0))],
        out_specs=pl.BlockSpec((tm, d), lambda i: (i, 0)),
        compiler_params=_cparams("arbitrary"),
        name="final_norm",
    )(x, g)


def _nsa_layer_weights(w_in, w_out, w_ck, pe_ck, w_cv, pe_cv):
    n_main = N_HEADS * HEAD_DIM + 6 * KV_WIDTH
    w_gate = jnp.pad(w_in[:, n_main:], ((0, 0), (0, V7X_LANES - N_HEADS * N_BRANCH)))
    return {"w_main": w_in[:, :n_main].astype(BF16), "w_gate": w_gate.astype(BF16),
            "w_out": w_out.astype(BF16), "cw": _compress_weights(w_ck, pe_ck, w_cv, pe_cv)}


def _rows_from_feature_major(xt):
    b, _, s = xt.shape
    return jnp.transpose(xt.reshape(b, 2, KV_HEADS, HEAD_DIM, s), (0, 4, 1, 2, 3))


def _feature_major_from_rows(x, lead):
    n = len(lead)
    perm = tuple(range(n)) + (n + 1, n + 2, n + 3, n)
    xt = jnp.transpose(x, perm)
    return xt.reshape(lead + (2 * KV_WIDTH, x.shape[n]))


def _nsa_prompt_layer(x, mod, g, lw, b, s_len):
    tm = min(512, s_len)
    mod3 = mod.reshape(b, 1, mod.shape[-1])
    rope_t = _rope_tables(jnp.arange(s_len), V7X_LANES)
    q, gates, cmp_rows, cmpt, selt, wint = _nsa_proj(x, mod3, s_len // tm, g, lw["w_main"], lw["w_gate"], rope_t,
                                                     s_len, tm, True)
    ck, cv = _compress(cmp_rows.reshape(b, s_len, 2 * KV_WIDTH), lw["cw"])
    o = _attn_prompt(q, gates, ck, cv, selt, wint, tq=min(256, s_len))
    x = _out_proj(o, lw["w_out"], x, mod3, s_len // tm, tm)
    return x, cmpt, selt, wint


def _nsa_decode_layer(x, mod3, g, lw, b, t_seq, caches, layer):
    t = x.shape[0]
    page_table = caches["page_table"]
    past_len = page_table.shape[1] * caches["cmp"].shape[3]
    rope_t = _rope_tables(past_len + jnp.arange(t) % t_seq, V7X_LANES)
    q, gates, cmp_rows, sel_rows, win_rows = _nsa_proj(x, mod3, 1, g, lw["w_main"], lw["w_gate"], rope_t, t, t, False)
    pad3 = lambda a: jnp.pad(a.reshape(b, t_seq, a.shape[-1]), ((0, 0), (0, DECODE_TOK_PAD - t_seq), (0, 0)))
    o = _attn_decode(page_table, pad3(q), pad3(gates), pad3(sel_rows), pad3(win_rows),
                     caches["cmp"], caches["sel"], caches["win"], layer, lw["cw"])
    o = o[:, :t_seq].reshape(t, o.shape[-1]).astype(BF16)
    x = _out_proj(o, lw["w_out"], x, mod3, 1, t)
    return x, cmp_rows, sel_rows, win_rows


def _trunk(x3, mod, caches, p):
    b, t_seq, d = x3.shape
    t = b * t_seq
    kvw = 2 * KV_WIDTH
    prompt = caches is None
    x = x3.reshape(t, d)
    new_cmp, new_sel, new_win, new_conv = [], [], [], []
    depth = mod.shape[0]
    g_final = p["g_norm_final"].reshape(1, d)
    finished = False
    for i in range(depth):
        j = i // 2
        if prompt:
            mod_i = mod[i].reshape(b, 1, 6 * d)
            mod3_of = lambda tm, mod_i=mod_i: (mod_i, t_seq // tm)
            tm_small, tm_big, tm_moe = min(512, t_seq), min(1024, t_seq), min(2048, t_seq)
        else:
            mod_i = jnp.repeat(mod[i], t_seq, axis=0).reshape(1, t, 6 * d)
            mod3_of = lambda tm, mod_i=mod_i: (mod_i, 1)
            tm_small = tm_big = tm_moe = t
        g_mix = p["g_norm_mix"][i].reshape(1, d)
        g_ffn = p["g_norm_ffn"][i].reshape(1, d)
        if i % 2 == 0:
            lw = _nsa_layer_weights(p["w_nsa_in"][j], p["w_nsa_out"][j], p["w_cmp_k"][j], p["pe_cmp_k"][j],
                                    p["w_cmp_v"][j], p["pe_cmp_v"][j])
            if prompt:
                x, cmpt, selt, wint = _nsa_prompt_layer(x, mod[i], g_mix, lw, b, t_seq)
                new_cmp.append(_rows_from_feature_major(cmpt))
                new_sel.append(_rows_from_feature_major(selt))
                new_win.append(_rows_from_feature_major(wint[:, :, t_seq - min(WINDOW, t_seq):]))
            else:
                x, c_rows, s_rows, w_rows = _nsa_decode_layer(x, mod_i, g_mix, lw, b, t_seq, caches, j)
                rows5 = lambda r: r.reshape(b, t_seq, 2, KV_HEADS, HEAD_DIM)
                new_cmp.append(rows5(c_rows))
                new_sel.append(rows5(s_rows))
                new_win.append(jnp.concatenate([caches["win_rows"][j], rows5(w_rows)], axis=1)[:, t_seq:])
            x = _ffn(x, *mod3_of(tm_big), g_ffn, p["w_ff_gu"], p["w_ff_down"], j, tm_big, 512)
        else:
            state = None if prompt else caches["conv"][j]
            x, st = _conv_mixer(x, *mod3_of(tm_small), g_mix, p["w_conv_in"][j].astype(BF16), p["w_conv"][j],
                                p["w_conv_out"][j].astype(BF16), state, b, t_seq, tm_small)
            new_conv.append(st)
            finished = i == depth - 1
            x = _moe(x, mod3_of, g_ffn, p["w_router"][j], p["b_router"][j], p["w_moe_gu"], p["w_moe_down"], j,
                     g_final, tm_moe, finished)
    y = x if finished else _final_norm(x, g_final, tm_big)
    return (y.reshape(b, t_seq, d), jnp.stack(new_cmp), jnp.stack(new_sel), jnp.stack(new_win), jnp.stack(new_conv))


def kernel(x_prompt, x_sample, cache_cmp_kv, cache_sel_kv, cache_win_kv, state_conv, page_table, c_prompt, c_sample,
           w_ada, b_ada, g_norm_mix, g_norm_ffn, g_norm_final, w_nsa_in, w_nsa_out, w_cmp_k, pe_cmp_k, w_cmp_v,
           pe_cmp_v, w_conv_in, w_conv, w_conv_out, w_ff_gu, w_ff_down, w_router, b_router, w_moe_gu, w_moe_down):
    p = {"g_norm_mix": g_norm_mix, "g_norm_ffn": g_norm_ffn, "g_norm_final": g_norm_final,
         "w_nsa_in": w_nsa_in, "w_nsa_out": w_nsa_out, "w_cmp_k": w_cmp_k, "pe_cmp_k": pe_cmp_k, "w_cmp_v": w_cmp_v,
         "pe_cmp_v": pe_cmp_v, "w_conv_in": w_conv_in, "w_conv": w_conv, "w_conv_out": w_conv_out,
         "w_ff_gu": w_ff_gu, "w_ff_down": w_ff_down, "w_router": w_router, "b_router": b_router,
         "w_moe_gu": w_moe_gu.astype(BF16), "w_moe_down": w_moe_down.astype(BF16)}
    nb = x_prompt.shape[0]
    mod = _ada(jnp.concatenate([c_prompt, c_sample], axis=0), w_ada, b_ada)
    y_p, cmp_p, sel_p, win_p, conv_p = _trunk(x_prompt, mod[:, :nb], None, p)
    caches = {"cmp": _feature_major_from_rows(cache_cmp_kv, cache_cmp_kv.shape[:2]),
              "sel": _feature_major_from_rows(cache_sel_kv, cache_sel_kv.shape[:2]),
              "win": _feature_major_from_rows(cache_win_kv, cache_win_kv.shape[:2]),
              "win_rows": cache_win_kv, "conv": state_conv, "page_table": page_table}
    y_s, cmp_s, sel_s, win_s, conv_s = _trunk(x_sample, mod[:, nb:], caches, p)
    return (y_p, y_s, cmp_p, cmp_s, sel_p, sel_s, win_p, win_s, conv_p, conv_s)
```

```python
import functools

import jax
import jax.numpy as jnp
from jax import lax
from jax.experimental import pallas as pl
from jax.experimental.pallas import tpu as pltpu

F32 = jnp.float32
BF16 = jnp.bfloat16

N_HEADS = 16
HEAD_DIM = 64
KV_HEADS = 4
HEADS_PER_KV = N_HEADS // KV_HEADS
KV_WIDTH = KV_HEADS * HEAD_DIM
ROT_DIM = HEAD_DIM // 4
ROPE_THETA = 500000.0
CMP_BLOCK = 32
SEL_BLOCK = 64
TOP_N = 8
WINDOW = 512
N_BRANCH = 3
FORCE_BONUS = float(HEADS_PER_KV + 1)
N_EXPERTS = 8
EPS = 1e-6
NEG_INF = float("-inf")

V7X_LANES = 128
V7X_SUBLANES = 8
V7X_VMEM_LIMIT_BYTES = 58 * 1024 * 1024


def _cparams(*sem):
    return pltpu.CompilerParams(dimension_semantics=sem, vmem_limit_bytes=V7X_VMEM_LIMIT_BYTES)


def _dot(a, b):
    return jnp.dot(a, b, preferred_element_type=F32)


def _dot_nt(a, b):
    return lax.dot_general(a, b, (((1,), (1,)), ((), ())), preferred_element_type=F32)


def _norm_mod(x, g, sc, sh):
    y = x * lax.rsqrt(jnp.mean(x * x, axis=-1, keepdims=True) + EPS)
    return (y * g) * (1.0 + sc) + sh


def _sigmoid(x):
    return 1.0 / (1.0 + jnp.exp(-x))


def _mod_spec(mod3, tiles_per_block):
    _, r, w = mod3.shape
    return pl.BlockSpec((1, r, w), lambda i, *_: (i // tiles_per_block, 0, 0))


def _full(shape):
    return pl.BlockSpec(shape, lambda *_: (0,) * len(shape))


def _ada_kernel(c_ref, w_ref, b_ref, o_ref):
    c = c_ref[...]
    cs = (c * _sigmoid(c)).astype(BF16)
    o_ref[0] = _dot(cs, w_ref[0].astype(BF16)) + b_ref[0]


def _ada(c, w_ada, b_ada):
    depth, d, n = w_ada.shape
    nb = c.shape[0]
    tn = 1536
    return pl.pallas_call(
        _ada_kernel,
        out_shape=jax.ShapeDtypeStruct((depth, nb, n), F32),
        grid=(depth, n // tn),
        in_specs=[
            pl.BlockSpec((nb, d), lambda i, j: (0, 0)),
            pl.BlockSpec((1, d, tn), lambda i, j: (i, 0, j)),
            pl.BlockSpec((1, 1, tn), lambda i, j: (i, 0, j)),
        ],
        out_specs=pl.BlockSpec((1, nb, tn), lambda i, j: (i, 0, j)),
        compiler_params=_cparams("arbitrary", "arbitrary"),
        name="ada",
    )(c, w_ada, b_ada.reshape(depth, 1, n))


def _rope_tables(pos, width):
    half = ROT_DIM // 2
    inv = 1.0 / (ROPE_THETA ** (jnp.arange(half, dtype=F32) * (2.0 / ROT_DIM)))
    ang = pos.astype(F32)[:, None] * inv[None, :]
    cos, sin = jnp.cos(ang), jnp.sin(ang)
    n = pos.shape[0]
    z8 = jnp.zeros((n, half), F32)
    rest0 = jnp.zeros((n, HEAD_DIM - ROT_DIM), F32)
    c = jnp.concatenate([cos, cos, rest0 + 1.0], axis=1)
    a = jnp.concatenate([-sin, z8, rest0], axis=1)
    b = jnp.concatenate([z8, sin, rest0], axis=1)
    rep = width // HEAD_DIM
    return tuple(jnp.tile(t, (1, rep)) for t in (c, a, b))


def _rope(v, c, a, b):
    w = v.shape[-1]
    half = ROT_DIM // 2
    return v * c + pltpu.roll(v, w - half, 1) * a + pltpu.roll(v, half, 1) * b


def _widen(t, width):
    return jnp.concatenate([t] * (width // t.shape[-1]), axis=1)


def _nsa_proj_kernel(x_ref, mod_ref, g_ref, w_ref, wg_ref, rc_ref, ra_ref, rb_ref, q_ref, gate_ref, *kv_refs,
                     feature_major):
    d = x_ref.shape[1]
    mod = mod_ref[0]
    h = _norm_mod(x_ref[...], g_ref[...], mod[:, d:2 * d], mod[:, 0:d]).astype(BF16)
    kvw = 2 * KV_WIDTH
    c1, a1, b1 = rc_ref[...], ra_ref[...], rb_ref[...]
    c2, a2, b2 = (_widen(t, kvw) for t in (c1, a1, b1))
    scale = HEAD_DIM ** -0.5
    hq = N_HEADS * HEAD_DIM
    for j in range(hq // kvw):
        p = _dot(h, w_ref[:, j * kvw:(j + 1) * kvw])
        q_ref[:, j * kvw:(j + 1) * kvw] = (_rope(p, c2, a2, b2) * scale).astype(BF16)
    gate_ref[...] = _sigmoid(_dot(h, wg_ref[...]))
    one = jnp.ones_like(c1)
    zero = jnp.zeros_like(c1)
    ck = jnp.concatenate([_widen(c1, KV_WIDTH), _widen(one, KV_WIDTH)], axis=1)
    ak = jnp.concatenate([_widen(a1, KV_WIDTH), _widen(zero, KV_WIDTH)], axis=1)
    bk = jnp.concatenate([_widen(b1, KV_WIDTH), _widen(zero, KV_WIDTH)], axis=1)
    cmp_rows = _dot(h, w_ref[:, hq:hq + kvw])
    sel_rows = _rope(_dot(h, w_ref[:, hq + kvw:hq + 2 * kvw]), ck, ak, bk)
    win_rows = _rope(_dot(h, w_ref[:, hq + 2 * kvw:hq + 3 * kvw]), ck, ak, bk)
    if feature_major:
        cmp_ref, cmpt_ref, selt_ref, wint_ref = kv_refs
        cmp_ref[...] = cmp_rows
        cmpt_ref[0] = cmp_rows.T
        selt_ref[0] = sel_rows.T
        wint_ref[0] = win_rows.T
    else:
        cmp_ref, sel_ref, win_ref = kv_refs
        cmp_ref[...] = cmp_rows
        sel_ref[...] = sel_rows
        win_ref[...] = win_rows


def _nsa_proj(x, mod3, tpb, g, w_main, w_gate, rope_t, rows_per_seq, tm, feature_major):
    t, d = x.shape
    nw = w_main.shape[1]
    kvw = 2 * KV_WIDTH
    hq = N_HEADS * HEAD_DIM
    nrt = rows_per_seq // tm
    rspec = pl.BlockSpec((tm, V7X_LANES), lambda i: (i % nrt, 0))
    tok = lambda w: pl.BlockSpec((tm, w), lambda i: (i, 0))
    out_shape = [jax.ShapeDtypeStruct((t, hq), BF16), jax.ShapeDtypeStruct((t, V7X_LANES), F32)]
    out_specs = [tok(hq), tok(V7X_LANES)]
    if feature_major:
        b = t // rows_per_seq
        fm = pl.BlockSpec((1, kvw, tm), lambda i: (i // nrt, 0, i % nrt))
        out_shape += [jax.ShapeDtypeStruct((t, kvw), F32)] + [jax.ShapeDtypeStruct((b, kvw, rows_per_seq), F32)] * 3
        out_specs += [tok(kvw), fm, fm, fm]
    else:
        out_shape += [jax.ShapeDtypeStruct((t, kvw), F32)] * 3
        out_specs += [tok(kvw)] * 3
    return pl.pallas_call(
        functools.partial(_nsa_proj_kernel, feature_major=feature_major),
        out_shape=tuple(out_shape),
        grid=(t // tm,),
        in_specs=[tok(d), _mod_spec(mod3, tpb), _full((1, d)), _full((d, nw)), _full((d, V7X_LANES)),
                  rspec, rspec, rspec],
        out_specs=tuple(out_specs),
        compiler_params=_cparams("arbitrary"),
        name="nsa_proj",
    )(x, mod3, g, w_main, w_gate, *rope_t)


def _compress_rows(load_rows, bdk_ref, bdv_ref, pek_ref, pev_ref, nc, perm_scr):
    half = nc // 2
    n_chunks = 2 * KV_WIDTH // V7X_LANES
    outs = []
    for c in range(n_chunks):
        pe_ref, w_ref = (pek_ref, bdk_ref) if c < n_chunks // 2 else (pev_ref, bdv_ref)
        cols = [(load_rows(c, l) + pe_ref[l:l + 1, :]).astype(BF16) for l in range(CMP_BLOCK)]
        perm_scr[...] = _dot(jnp.concatenate(cols, axis=1), w_ref[...])
        outs.append(jnp.concatenate([perm_scr[pl.ds(0, half, stride=2), :], perm_scr[pl.ds(1, half, stride=2), :]],
                                    axis=0))
    return (jnp.concatenate(outs[0:n_chunks // 2], axis=1), jnp.concatenate(outs[n_chunks // 2:], axis=1))


def _compress_kernel(r0_ref, r1_ref, r2_ref, r3_ref, bdk_ref, bdv_ref, pek_ref, pev_ref, rc_ref, ra_ref, rb_ref,
                     ck_ref, cv_ref, perm_scr):
    nc = ck_ref.shape[1]

    chunk_refs = (r0_ref, r1_ref, r2_ref, r3_ref)

    def load_rows(c, l):
        return chunk_refs[c][0, pl.ds(l, nc, stride=CMP_BLOCK), :]

    acck, accv = _compress_rows(load_rows, bdk_ref, bdv_ref, pek_ref, pev_ref, nc, perm_scr)
    ck_ref[0] = _rope(acck, rc_ref[...], ra_ref[...], rb_ref[...]).astype(BF16)
    cv_ref[0] = accv.astype(BF16)


def _cmp_order_end_pos(nc):
    n = jnp.arange(nc)
    half = nc // 2
    blk = jnp.where(n < half, 2 * n, 2 * (n - half) + 1)
    return blk * CMP_BLOCK + (CMP_BLOCK - 1)


def _cmp_end_row(nc):
    half = nc // 2
    col = lax.broadcasted_iota(jnp.int32, (1, nc), 1)
    return jnp.where(col < half, 2 * CMP_BLOCK * col + (CMP_BLOCK - 1),
                     2 * CMP_BLOCK * (col - half) + (2 * CMP_BLOCK - 1))


def _compress(rows3, cw):
    b, tk, w = rows3.shape
    nc = tk // CMP_BLOCK
    assert nc % 2 == 0 and nc * CMP_BLOCK == tk
    rt = _rope_tables(_cmp_order_end_pos(nc), KV_WIDTH)
    return pl.pallas_call(
        _compress_kernel,
        out_shape=(jax.ShapeDtypeStruct((b, nc, KV_WIDTH), BF16),) * 2,
        grid=(b,),
        in_specs=[
            *[pl.BlockSpec((1, tk, V7X_LANES), functools.partial(lambda i, c: (i, 0, c), c=c))
              for c in range(w // V7X_LANES)],
            _full(cw["bdk"].shape), _full(cw["bdv"].shape), _full(cw["pek"].shape), _full(cw["pev"].shape),
            _full(rt[0].shape), _full(rt[1].shape), _full(rt[2].shape),
        ],
        out_specs=(pl.BlockSpec((1, nc, KV_WIDTH), lambda i: (i, 0, 0)),) * 2,
        scratch_shapes=[pltpu.VMEM((nc, V7X_LANES), F32)],
        compiler_params=_cparams("arbitrary"),
        name="compress",
    )(rows3, rows3, rows3, rows3, cw["bdk"], cw["bdv"], cw["pek"], cw["pev"], *rt)


def _compress_weights(w_ck, pe_ck, w_cv, pe_cv):
    pair = V7X_LANES // HEAD_DIM
    eye = jnp.eye(pair, dtype=F32)

    def bd(w):
        return jnp.einsum("gh,lde->lgdhe", eye, w).reshape(CMP_BLOCK * V7X_LANES, V7X_LANES).astype(BF16)

    return {"bdk": bd(w_ck), "bdv": bd(w_cv),
            "pek": jnp.tile(pe_ck, (1, pair)), "pev": jnp.tile(pe_cv, (1, pair))}


def _softmax_rows(s):
    m = jnp.max(s, axis=-1, keepdims=True)
    m = jnp.where(m == NEG_INF, 0.0, m)
    e = jnp.exp(s - m)
    return e / jnp.maximum(jnp.sum(e, axis=-1, keepdims=True), 1e-30)


def _masked_softmax(s, mask):
    return _softmax_rows(jnp.where(mask, s, NEG_INF))


def _select_blocks(imp, t_col, ns):
    blk = lax.broadcasted_iota(jnp.int32, (1, ns), 1)
    cur = t_col // SEL_BLOCK
    forced = jnp.where(blk == 0, 1.0, jnp.where(blk == cur, 1.0, 0.0))
    score = jnp.where(blk <= cur, imp + FORCE_BONUS * forced, NEG_INF)
    rank = jnp.zeros(score.shape, F32)
    for i in range(ns):
        ci = score[:, i:i + 1]
        later = jnp.where(blk > i, 1.0, 0.0)
        rank = rank + jnp.where(ci > score, 1.0, jnp.where(ci == score, later, 0.0))
    return jnp.where(rank < float(min(TOP_N, ns)), 1.0, 0.0)


def _select_blocks_t(imp_t, t_row, ns):
    blk = lax.broadcasted_iota(jnp.int32, (ns, 1), 0)
    cur = t_row // SEL_BLOCK
    forced = jnp.where(blk == 0, 1.0, jnp.where(blk == cur, 1.0, 0.0))
    score = jnp.where(blk <= cur, imp_t + FORCE_BONUS * forced, NEG_INF)
    rank = jnp.zeros(score.shape, F32)
    for i in range(ns):
        ci = score[i:i + 1, :]
        later = jnp.where(blk > i, 1.0, 0.0)
        rank = rank + jnp.where(ci > score, 1.0, jnp.where(ci == score, later, 0.0))
    return jnp.where(rank < float(min(TOP_N, ns)), 1.0, 0.0)


def _stack_heads(x, g):
    base = g * HEADS_PER_KV * HEAD_DIM
    return jnp.concatenate([x[:, base + r * HEAD_DIM: base + (r + 1) * HEAD_DIM] for r in range(HEADS_PER_KV)],
                           axis=0)


def _gate_cols(gates, g, branch):
    cols = [(g * HEADS_PER_KV + r) * N_BRANCH + branch for r in range(HEADS_PER_KV)]
    return jnp.concatenate([gates[:, c:c + 1] for c in cols], axis=0)


def _sum_rows(n):
    row = lax.broadcasted_iota(jnp.int32, (HEAD_DIM, n), 0)
    return jnp.where(row == 0, 1.0, 0.0).astype(BF16)


def _pad_to(x, rows, cols):
    r, c = x.shape
    if c < cols:
        x = jnp.concatenate([x, jnp.zeros((r, cols - c), x.dtype)], axis=1)
    if r < rows:
        x = jnp.concatenate([x, jnp.zeros((rows - r, x.shape[1]), x.dtype)], axis=0)
    return x


WIN_CHUNK = V7X_LANES
MASK_BIG = 2.0 ** 100


def _attn_prompt_kernel(q_ref, gate_ref, ck_ref, cv_ref, selt_ref, wint_ref, o_ref,
                        ks_scr, vs_scr, kw_scr, vw_scr, *, tk):
    tq = q_ref.shape[0]
    s_len = selt_ref.shape[2]
    nc = ck_ref.shape[1]
    half = nc // 2
    ns = s_len // SEL_BLOCK
    qi = pl.program_id(1)
    k_pad = ks_scr.shape[2] - HEAD_DIM - ns

    @pl.when(qi == 0)
    def _():
        for c in range(s_len // tk):
            kpos = c * tk + lax.broadcasted_iota(jnp.int32, (1, tk), 1)
            brow = lax.broadcasted_iota(jnp.int32, (ns + k_pad, 1), 0)
            expand = jnp.where(kpos // SEL_BLOCK == brow, 1.0, 0.0).astype(BF16)
            for g in range(KV_HEADS):
                k_g = selt_ref[0, g * HEAD_DIM:(g + 1) * HEAD_DIM, c * tk:(c + 1) * tk].astype(BF16)
                ks_scr[c, g] = jnp.concatenate([k_g, expand], axis=0)
                v_g = selt_ref[0, KV_WIDTH + g * HEAD_DIM:KV_WIDTH + (g + 1) * HEAD_DIM, c * tk:(c + 1) * tk]
                vs_scr[c, g] = jnp.concatenate([v_g.astype(BF16), _sum_rows(tk)], axis=0)
        for c in range(s_len // WIN_CHUNK):
            kw_scr[c] = wint_ref[0, 0:KV_WIDTH, c * WIN_CHUNK:(c + 1) * WIN_CHUNK].astype(BF16)
            vw_scr[c] = wint_ref[0, KV_WIDTH:2 * KV_WIDTH, c * WIN_CHUNK:(c + 1) * WIN_CHUNK].astype(BF16)

    q0 = qi * tq
    t_tile = q0 + lax.broadcasted_iota(jnp.int32, (tq, 1), 0)
    t_row = q0 + lax.broadcasted_iota(jnp.int32, (1, tq), 1)
    t_col4 = jnp.concatenate([t_tile] * HEADS_PER_KV, axis=0)
    q_all = q_ref[...]
    gates = gate_ref[...]
    c_end = _cmp_end_row(nc)
    last = q0 // tk
    kpos_last = last * tk + lax.broadcasted_iota(jnp.int32, (1, tk), 1)
    causal = jnp.where(kpos_last <= t_tile, 0.0, -MASK_BIG)
    causal4 = jnp.concatenate([causal] * HEADS_PER_KV, axis=0)

    wlen = min(WINDOW + tq, s_len)
    n_wc = wlen // WIN_CHUNK
    w0 = jnp.clip(q0 + tq - wlen, 0, s_len - wlen)
    wc0 = w0 // WIN_CHUNK
    wpos = w0 + lax.broadcasted_iota(jnp.int32, (1, wlen), 1)
    wbias = jnp.where(wpos <= t_tile, jnp.where(wpos >= t_tile - WINDOW, 0.0, NEG_INF), NEG_INF)
    wbias4 = jnp.concatenate([wbias] * HEADS_PER_KV, axis=0)

    for g in range(KV_HEADS):
        rows_g = slice(g * HEAD_DIM, (g + 1) * HEAD_DIM)
        q4 = _stack_heads(q_all, g)
        p_c = _masked_softmax(_dot_nt(q4, ck_ref[0, :, rows_g]), c_end <= t_col4)
        o_c = _dot(p_c.astype(BF16), cv_ref[0, :, rows_g])
        psum = p_c[0:tq]
        for r in range(1, HEADS_PER_KV):
            psum = psum + p_c[r * tq:(r + 1) * tq]
        psum_t = _pad_to(psum, tq, V7X_LANES).T
        imp_t = psum_t[0:half] + psum_t[half:nc]
        sel_t = _select_blocks_t(imp_t, t_row, ns)
        sel = _pad_to(sel_t, V7X_LANES, tq).T[:, 0:ns + k_pad]
        unchosen = ((sel - 1.0) * MASK_BIG).astype(BF16)
        q_ext = jnp.concatenate([q4, jnp.concatenate([unchosen] * HEADS_PER_KV, axis=0)], axis=1)

        kw = jnp.concatenate([kw_scr[wc0 + i, rows_g, :] for i in range(n_wc)], axis=1)
        vw = jnp.concatenate([vw_scr[wc0 + i, rows_g, :] for i in range(n_wc)], axis=1)
        vw = jnp.concatenate([vw, _sum_rows(wlen)], axis=0)

        def attend(c, carry, bias):
            m, acc = carry
            s = _dot(q_ext, ks_scr[c, g])
            if bias is not None:
                s = s + bias
            m_new = jnp.maximum(m, jnp.max(s, axis=-1, keepdims=True))
            p = jnp.exp(s - m_new)
            acc = jnp.exp(m - m_new) * acc + _dot_nt(p.astype(BF16), vs_scr[c, g])
            return m_new, acc

        rows = HEADS_PER_KV * tq
        init = (jnp.full((rows, 1), NEG_INF, F32), jnp.zeros((rows, 2 * HEAD_DIM), F32))
        carry = lax.fori_loop(0, last, lambda c, carry: attend(c, carry, None), init)
        _, acc_s = attend(last, carry, causal4)
        o_s = acc_s[:, 0:HEAD_DIM] / jnp.maximum(acc_s[:, HEAD_DIM:HEAD_DIM + 1], 1e-30)

        s_w = _dot(q4, kw) + wbias4
        m_w = jnp.max(s_w, axis=-1, keepdims=True)
        p_w = jnp.exp(s_w - jnp.where(m_w == NEG_INF, 0.0, m_w))
        acc_w = _dot_nt(p_w.astype(BF16), vw)
        o_w = acc_w[:, 0:HEAD_DIM] / jnp.maximum(acc_w[:, HEAD_DIM:HEAD_DIM + 1], 1e-30)

        o = _gate_cols(gates, g, 0) * o_c + _gate_cols(gates, g, 1) * o_s + _gate_cols(gates, g, 2) * o_w
        for r in range(HEADS_PER_KV):
            head = g * HEADS_PER_KV + r
            o_ref[:, head * HEAD_DIM:(head + 1) * HEAD_DIM] = o[r * tq:(r + 1) * tq].astype(BF16)


def _attn_prompt(q, gates, ck, cv, selt, wint, tq):
    b, kvw, s_len = selt.shape
    nc = ck.shape[1]
    hq = N_HEADS * HEAD_DIM
    nq = s_len // tq
    tk = min(512, s_len)
    assert s_len % tk == 0 and s_len % tq == 0 and tk % tq == 0 and tq % V7X_LANES == 0 and nc <= V7X_LANES
    assert min(WINDOW + tq, s_len) % WIN_CHUNK == 0 and HEAD_DIM + s_len // SEL_BLOCK <= V7X_LANES
    return pl.pallas_call(
        functools.partial(_attn_prompt_kernel, tk=tk),
        out_shape=jax.ShapeDtypeStruct((b * s_len, hq), BF16),
        grid=(b, nq),
        in_specs=[
            pl.BlockSpec((tq, hq), lambda i, j: (i * nq + j, 0)),
            pl.BlockSpec((tq, V7X_LANES), lambda i, j: (i * nq + j, 0)),
            pl.BlockSpec((1, nc, KV_WIDTH), lambda i, j: (i, 0, 0)),
            pl.BlockSpec((1, nc, KV_WIDTH), lambda i, j: (i, 0, 0)),
            pl.BlockSpec((1, kvw, s_len), lambda i, j: (i, 0, 0)),
            pl.BlockSpec((1, kvw, s_len), lambda i, j: (i, 0, 0)),
        ],
        out_specs=pl.BlockSpec((tq, hq), lambda i, j: (i * nq + j, 0)),
        scratch_shapes=[
            pltpu.VMEM((s_len // tk, KV_HEADS, V7X_LANES, tk), BF16),
            pltpu.VMEM((s_len // tk, KV_HEADS, 2 * HEAD_DIM, tk), BF16),
            pltpu.VMEM((s_len // WIN_CHUNK, KV_WIDTH, WIN_CHUNK), BF16),
            pltpu.VMEM((s_len // WIN_CHUNK, KV_WIDTH, WIN_CHUNK), BF16),
        ],
        compiler_params=_cparams("arbitrary", "arbitrary"),
        name="attn_prompt",
    )(q, gates, ck, cv, selt, wint)


PAGES_PER_STEP = 16
CMP_PITCH = CMP_BLOCK + V7X_SUBLANES
DECODE_TOK_PAD = V7X_SUBLANES


def _block_diag_q(q):
    parts = []
    for g in range(KV_HEADS):
        q4 = _stack_heads(q, g)
        z = jnp.zeros_like(q4)
        parts.append(jnp.concatenate([q4 if gg == g else z for gg in range(KV_HEADS)], axis=1))
    return jnp.concatenate(parts, axis=0)


def _attn_decode_kernel(pt_ref, q_ref, gate_ref, nsel_ref, nwin_ref, wbuf_ref, bdk_ref, bdv_ref, pek_ref, pev_ref,
                        rc_ref, ra_ref, rb_ref, *rest, page, n_pages):
    pps = PAGES_PER_STEP
    cpages = rest[:pps]
    spages = rest[pps:2 * pps]
    o_ref = rest[2 * pps]
    cmp_scr, s_scr, vs_scr, qbd_scr, perm_scr = rest[2 * pps + 1:]
    blocks_per_page = page // CMP_BLOCK
    del pt_ref
    j = pl.program_id(1)
    tq = q_ref.shape[1]
    past_len = n_pages * page
    n_lane_chunks = 2 * KV_WIDTH // V7X_LANES

    @pl.when(j == 0)
    def _():
        qbd_scr[...] = _block_diag_q(q_ref[0].astype(F32)).astype(BF16)

    qbd = qbd_scr[...]
    for p in range(pps):
        for c in range(n_lane_chunks):
            rows_pc = cpages[p][0, 0, c * V7X_LANES:(c + 1) * V7X_LANES, :].T
            for n in range(blocks_per_page):
                r0 = pl.multiple_of(((j * pps + p) * blocks_per_page + n) * CMP_PITCH, V7X_SUBLANES)
                cmp_scr[c, pl.ds(r0, CMP_BLOCK), :] = rows_pc[n * CMP_BLOCK:(n + 1) * CMP_BLOCK]
    k_step = jnp.concatenate([spages[p][0, 0, 0:KV_WIDTH, :] for p in range(pps)], axis=1).astype(BF16)
    s_scr[j] = _dot(qbd, k_step)
    vs_scr[j] = jnp.concatenate([spages[p][0, 0, KV_WIDTH:2 * KV_WIDTH, :] for p in range(pps)],
                                axis=1).astype(BF16)

    @pl.when(j == pl.num_programs(1) - 1)
    def _():
        tok = lax.broadcasted_iota(jnp.int32, (tq, 1), 0)
        t_rows = past_len + jnp.concatenate([tok] * N_HEADS, axis=0)
        nsel = _pad_to(nsel_ref[0], page, 2 * KV_WIDTH)
        s_new = _dot_nt(qbd, nsel[:, 0:KV_WIDTH].astype(BF16))

        nc = past_len // CMP_BLOCK
        half = nc // 2

        def load_rows(c, l):
            return cmp_scr[c, pl.ds(l, nc, stride=CMP_PITCH), :]

        acck, accv = _compress_rows(load_rows, bdk_ref, bdv_ref, pek_ref, pev_ref, nc, perm_scr)
        ck = _rope(acck, rc_ref[...], ra_ref[...], rb_ref[...]).astype(BF16)
        p_c = _masked_softmax(_dot_nt(qbd, ck), _cmp_end_row(nc) <= t_rows)
        o_c = _dot(p_c.astype(BF16), accv.astype(BF16))

        ns = n_pages * page // SEL_BLOCK + 1
        grp = HEADS_PER_KV * tq
        psum = []
        for g in range(KV_HEADS):
            pg_ = p_c[g * grp:g * grp + tq]
            for r in range(1, HEADS_PER_KV):
                pg_ = pg_ + p_c[g * grp + r * tq:g * grp + (r + 1) * tq]
            psum.append(pg_)
        psum = jnp.concatenate(psum, axis=0)
        imp = jnp.concatenate([psum[:, 0:half] + psum[:, half:nc], jnp.zeros((KV_HEADS * tq, 1), F32)], axis=1)
        selm = _select_blocks(imp, past_len + jnp.concatenate([tok] * KV_HEADS, axis=0), ns)
        selm_rows = jnp.concatenate(
            [selm[g * tq:(g + 1) * tq] for g in range(KV_HEADS) for _ in range(HEADS_PER_KV)], axis=0)
        nkeys = (n_pages + 1) * page
        kpos = lax.broadcasted_iota(jnp.int32, (1, nkeys), 1)
        brow = lax.broadcasted_iota(jnp.int32, (ns, 1), 0)
        expand = jnp.where(kpos // SEL_BLOCK == brow, 1.0, 0.0).astype(BF16)
        chosen = _dot(selm_rows.astype(BF16), expand)
        n_steps = n_pages // pps
        s_all = jnp.concatenate([s_scr[i] for i in range(n_steps)] + [s_new], axis=1)
        s_all = jnp.where(kpos <= t_rows, jnp.where(chosen > 0.5, s_all, NEG_INF), NEG_INF)
        p_s = _softmax_rows(s_all).astype(BF16)
        vt_all = jnp.concatenate([vs_scr[i] for i in range(n_steps)], axis=1)
        o_s = _dot_nt(p_s[:, 0:past_len], vt_all) + _dot(p_s[:, past_len:nkeys], nsel[:, KV_WIDTH:].astype(BF16))

        wb = wbuf_ref.shape[3]
        nwin = _pad_to(nwin_ref[0], page, 2 * KV_WIDTH)
        s_w = jnp.concatenate([_dot(qbd, wbuf_ref[0, 0, 0:KV_WIDTH, :].astype(BF16)),
                               _dot_nt(qbd, nwin[:, 0:KV_WIDTH].astype(BF16))], axis=1)
        wpos = past_len - wb + lax.broadcasted_iota(jnp.int32, (1, wb + page), 1)
        s_w = jnp.where(wpos <= t_rows, jnp.where(wpos >= t_rows - WINDOW, s_w, NEG_INF), NEG_INF)
        p_w = _softmax_rows(s_w).astype(BF16)
        o_w = (_dot_nt(p_w[:, 0:wb], wbuf_ref[0, 0, KV_WIDTH:2 * KV_WIDTH, :].astype(BF16))
               + _dot(p_w[:, wb:wb + page], nwin[:, KV_WIDTH:].astype(BF16)))

        gates = gate_ref[0]
        gcol = lambda br: jnp.concatenate([_gate_cols(gates, g, br) for g in range(KV_HEADS)], axis=0)
        o = gcol(0) * o_c + gcol(1) * o_s + gcol(2) * o_w
        for g in range(KV_HEADS):
            for r in range(HEADS_PER_KV):
                col = (g * HEADS_PER_KV + r) * HEAD_DIM
                o_ref[0, :, col:col + HEAD_DIM] = o[g * grp + r * tq:g * grp + (r + 1) * tq,
                                                    g * HEAD_DIM:(g + 1) * HEAD_DIM]


def _attn_decode(page_table, q3, gates3, nsel3, nwin3, cache_cmp_t, cache_sel_t, cache_win_t, layer, cw):
    b, tq, hq = q3.shape
    n_pages = page_table.shape[1]
    page = cache_cmp_t.shape[3]
    kvw = 2 * KV_WIDTH
    pps = PAGES_PER_STEP
    assert n_pages % pps == 0 and (n_pages * page) % (2 * CMP_BLOCK) == 0 and tq < CMP_BLOCK
    assert tq % V7X_SUBLANES == 0 and page == V7X_LANES
    nc = n_pages * page // CMP_BLOCK
    rt = _rope_tables(_cmp_order_end_pos(nc), KV_WIDTH)
    wb = cache_win_t.shape[3]
    full = lambda shp: pl.BlockSpec(shp, lambda *_: (0,) * len(shp))
    per_seq = lambda shp: pl.BlockSpec((1,) + shp, lambda i, j, pt: (i,) + (0,) * len(shp))
    page_spec = lambda p: pl.BlockSpec((1, 1, kvw, page), lambda i, j, pt: (layer, pt[i, j * pps + p], 0, 0))
    rows = N_HEADS * tq
    grid_spec = pltpu.PrefetchScalarGridSpec(
        num_scalar_prefetch=1,
        grid=(b, n_pages // pps),
        in_specs=[
            per_seq((tq, hq)), per_seq((tq, V7X_LANES)), per_seq((tq, kvw)), per_seq((tq, kvw)),
            pl.BlockSpec((1, 1, kvw, wb), lambda i, j, pt: (layer, i, 0, 0)),
            full(cw["bdk"].shape), full(cw["bdv"].shape), full(cw["pek"].shape), full(cw["pev"].shape),
            full(rt[0].shape), full(rt[1].shape), full(rt[2].shape),
            *[page_spec(p) for p in range(pps)],
            *[page_spec(p) for p in range(pps)],
        ],
        out_specs=per_seq((tq, hq)),
        scratch_shapes=[
            pltpu.VMEM((kvw // V7X_LANES, nc * CMP_PITCH, V7X_LANES), F32),
            pltpu.VMEM((n_pages // pps, rows, pps * page), F32),
            pltpu.VMEM((n_pages // pps, KV_WIDTH, pps * page), BF16),
            pltpu.VMEM((rows, KV_WIDTH), BF16),
            pltpu.VMEM((nc, V7X_LANES), F32),
        ],
    )
    return pl.pallas_call(
        functools.partial(_attn_decode_kernel, page=page, n_pages=n_pages),
        out_shape=jax.ShapeDtypeStruct((b, tq, hq), F32),
        grid_spec=grid_spec,
        compiler_params=_cparams("arbitrary", "arbitrary"),
        name="attn_decode",
    )(page_table, q3, gates3, nsel3, nwin3, cache_win_t, cw["bdk"], cw["bdv"], cw["pek"], cw["pev"], *rt,
      *([cache_cmp_t] * pps), *([cache_sel_t] * pps))


def _out_proj_kernel(o_ref, w_ref, x_ref, mod_ref, xo_ref):
    d = x_ref.shape[1]
    xo_ref[...] = x_ref[...] + mod_ref[0][:, 2 * d:3 * d] * _dot(o_ref[...], w_ref[...])


def _out_proj(o, w_bf16, x, mod3, tpb, tm):
    t, d = x.shape
    k = o.shape[1]
    return pl.pallas_call(
        _out_proj_kernel,
        out_shape=jax.ShapeDtypeStruct((t, d), F32),
        grid=(t // tm,),
        in_specs=[
            pl.BlockSpec((tm, k), lambda i: (i, 0)),
            pl.BlockSpec((k, d), lambda i: (0, 0)),
            pl.BlockSpec((tm, d), lambda i: (i, 0)),
            _mod_spec(mod3, tpb),
        ],
        out_specs=pl.BlockSpec((tm, d), lambda i: (i, 0)),
        compiler_params=_cparams("arbitrary"),
        name="out_proj",
    )(o, w_bf16, x, mod3)


def _ffn_kernel(x_ref, mod_ref, g_ref, wg_ref, wu_ref, wd_ref, o_ref, h_scr):
    d = x_ref.shape[1]
    f = pl.program_id(1)

    @pl.when(f == 0)
    def _():
        mod = mod_ref[0]
        h_scr[...] = _norm_mod(x_ref[...], g_ref[...], mod[:, 4 * d:5 * d], mod[:, 3 * d:4 * d]).astype(BF16)
        o_ref[...] = jnp.zeros(o_ref.shape, F32)

    h = h_scr[...]
    a = _dot(h, wg_ref[0].astype(BF16))
    u = _dot(h, wu_ref[0].astype(BF16))
    act = (a * _sigmoid(a) * u).astype(BF16)
    o_ref[...] += _dot(act, wd_ref[0].astype(BF16))

    @pl.when(f == pl.num_programs(1) - 1)
    def _():
        o_ref[...] = x_ref[...] + mod_ref[0][:, 5 * d:6 * d] * o_ref[...]


def _ffn(x, mod3, tpb, g, w_gu_all, w_down_all, layer, tm, tf):
    t, d = x.shape
    ff = w_down_all.shape[1]
    nf = ff // tf
    return pl.pallas_call(
        _ffn_kernel,
        out_shape=jax.ShapeDtypeStruct((t, d), F32),
        grid=(t // tm, nf),
        in_specs=[
            pl.BlockSpec((tm, d), lambda i, f: (i, 0)),
            _mod_spec(mod3, tpb),
            pl.BlockSpec((1, d), lambda i, f: (0, 0)),
            pl.BlockSpec((1, d, tf), lambda i, f: (layer, 0, f)),
            pl.BlockSpec((1, d, tf), lambda i, f: (layer, 0, nf + f)),
            pl.BlockSpec((1, tf, d), lambda i, f: (layer, f, 0)),
        ],
        out_specs=pl.BlockSpec((tm, d), lambda i, f: (i, 0)),
        scratch_shapes=[pltpu.VMEM((tm, d), BF16)],
        compiler_params=_cparams("arbitrary", "arbitrary"),
        name="ffn",
    )(x, mod3, g, w_gu_all, w_gu_all, w_down_all)


def _conv_in(x_ref, mod_ref, g_ref, win_ref):
    d = x_ref.shape[1]
    mod = mod_ref[0]
    h = _norm_mod(x_ref[...], g_ref[...], mod[:, d:2 * d], mod[:, 0:d]).astype(BF16)
    bg = _dot(h, win_ref[:, 0:d])
    u = _dot(h, win_ref[:, d:2 * d]) * _dot(h, win_ref[:, 2 * d:3 * d])
    return bg, u


def _conv_out(bg, conv, x_ref, mod_ref, wout_ref, xo_ref):
    d = x_ref.shape[1]
    y = _dot((bg * conv).astype(BF16), wout_ref[...])
    xo_ref[...] = x_ref[...] + mod_ref[0][:, 2 * d:3 * d] * y


def _conv_seq_kernel(x_ref, mod_ref, g_ref, win_ref, wc_ref, wout_ref, st_ref, xo_ref, nst_ref, carry_scr):
    tm = x_ref.shape[0]
    bg, u = _conv_in(x_ref, mod_ref, g_ref, win_ref)

    @pl.when(pl.program_id(1) == 0)
    def _():
        carry_scr[6:8, :] = st_ref[0]

    ue = jnp.concatenate([carry_scr[...], u], axis=0)
    conv = wc_ref[0:1, :] * ue[6:6 + tm] + wc_ref[1:2, :] * ue[7:7 + tm] + wc_ref[2:3, :] * u
    carry_scr[...] = u[tm - 8:tm]
    nst_ref[0] = u[tm - 2:tm]
    _conv_out(bg, conv, x_ref, mod_ref, wout_ref, xo_ref)


def _conv_tok_kernel(x_ref, mod_ref, g_ref, win_ref, wc_ref, wout_ref, p1_ref, p2_ref, xo_ref, u_ref, *, t_seq):
    tm = x_ref.shape[0]
    bg, u = _conv_in(x_ref, mod_ref, g_ref, win_ref)
    row = lax.broadcasted_iota(jnp.int32, (tm, 1), 0) % t_seq
    prev1 = jnp.where(row >= 1, pltpu.roll(u, 1, 0), p1_ref[...])
    prev2 = jnp.where(row >= 2, pltpu.roll(u, 2, 0), p2_ref[...])
    conv = wc_ref[0:1, :] * prev2 + wc_ref[1:2, :] * prev1 + wc_ref[2:3, :] * u
    u_ref[...] = u
    _conv_out(bg, conv, x_ref, mod_ref, wout_ref, xo_ref)


def _conv_mixer(x, mod3, tpb, g, w_in, w_conv, w_out, state, b, t_seq, tm):
    t, d = x.shape
    weights = [_full((1, d)), _full(w_in.shape), _full(w_conv.shape), _full(w_out.shape)]
    if state is None:
        state = jnp.zeros((b, 2, d), F32)
    if t_seq >= tm:
        nt = t_seq // tm
        xo, nst = pl.pallas_call(
            _conv_seq_kernel,
            out_shape=(jax.ShapeDtypeStruct((t, d), F32), jax.ShapeDtypeStruct((b, 2, d), F32)),
            grid=(b, nt),
            in_specs=[pl.BlockSpec((tm, d), lambda i, j: (i * nt + j, 0)),
                      pl.BlockSpec((1, 1, mod3.shape[-1]), lambda i, j: (i, 0, 0)),
                      *weights,
                      pl.BlockSpec((1, 2, d), lambda i, j: (i, 0, 0))],
            out_specs=(pl.BlockSpec((tm, d), lambda i, j: (i * nt + j, 0)),
                       pl.BlockSpec((1, 2, d), lambda i, j: (i, 0, 0))),
            scratch_shapes=[pltpu.VMEM((8, d), F32)],
            compiler_params=_cparams("arbitrary", "arbitrary"),
            name="conv_seq",
        )(x, mod3, g, w_in, w_conv, w_out, state)
        return xo, nst
    assert t == tm and t_seq >= 2
    p1 = jnp.zeros((b, t_seq, d), F32).at[:, 0].set(state[:, 1]).reshape(t, d)
    p2 = jnp.zeros((b, t_seq, d), F32).at[:, 0].set(state[:, 0]).at[:, 1].set(state[:, 1]).reshape(t, d)
    xo, u = pl.pallas_call(
        functools.partial(_conv_tok_kernel, t_seq=t_seq),
        out_shape=(jax.ShapeDtypeStruct((t, d), F32), jax.ShapeDtypeStruct((t, d), F32)),
        grid=(1,),
        in_specs=[_full((t, d)), _full(mod3.shape), *weights, _full((t, d)), _full((t, d))],
        out_specs=(_full((t, d)), _full((t, d))),
        compiler_params=_cparams("arbitrary"),
        name="conv_tok",
    )(x, mod3, g, w_in, w_conv, w_out, p1, p2)
    return xo, u.reshape(b, t_seq, d)[:, t_seq - 2:]


MOE_ROW_UNIT = 128
MOE_BOUNDS_ROWS = 16
MOE_GATHER_BLOCKS = 4
MOE_SCATTER_UNITS = 2


def _split_bf16(x):
    hi = x.astype(BF16)
    return hi, (x - hi.astype(F32)).astype(BF16)


def _moe_route_kernel(x_ref, mod_ref, g_ref, wr_ref, br_ref, h_ref, pos_ref, comb_ref, post_ref, bnd_ref, *, blk):
    tm, d = x_ref.shape
    nb = tm // blk
    mod = mod_ref[0]
    h32 = _norm_mod(x_ref[...], g_ref[...], mod[:, 4 * d:5 * d], mod[:, 3 * d:4 * d])
    h_ref[...] = h32.astype(BF16)
    lane = lax.broadcasted_iota(jnp.int32, (1, V7X_LANES), 1)
    h_hi, h_lo = _split_bf16(h32)
    w_hi, w_lo = _split_bf16(wr_ref[...])
    logits = _dot(h_hi, w_hi) + (_dot(h_hi, w_lo) + _dot(h_lo, w_hi)) + br_ref[...]
    logits = jnp.where(lane < N_EXPERTS, logits, NEG_INF)
    m1 = jnp.max(logits, axis=-1, keepdims=True)
    i1 = jnp.min(jnp.where(logits == m1, lane, V7X_LANES), axis=-1, keepdims=True)
    rest = jnp.where(lane == i1, NEG_INF, logits)
    m2 = jnp.max(rest, axis=-1, keepdims=True)
    i2 = jnp.min(jnp.where(rest == m2, lane, V7X_LANES), axis=-1, keepdims=True)
    e2 = jnp.exp(m2 - m1)
    den = 1.0 + e2
    comb_ref[...] = jnp.where(lane == i1, 1.0 / den, jnp.where(lane == i2, e2 / den, 0.0))
    mask = jnp.where(lane == i1, 1.0, jnp.where(lane == i2, 1.0, 0.0))
    r = lax.broadcasted_iota(jnp.int32, (blk, blk), 0)
    c = lax.broadcasted_iota(jnp.int32, (blk, blk), 1)
    lower = jnp.where(c < r, 1.0, 0.0).astype(BF16)
    upper = jnp.where(r < c, 1.0, 0.0).astype(BF16)
    carry = jnp.zeros((1, V7X_LANES), F32)
    bnd_ref[0] = jnp.zeros((MOE_BOUNDS_ROWS, V7X_LANES), F32)
    for k in range(nb):
        mb = mask[k * blk:(k + 1) * blk]
        pos_ref[k * blk:(k + 1) * blk, :] = jnp.where(mb > 0.5, _dot(lower, mb.astype(BF16)) + carry, -1.0)
        carry = carry + jnp.sum(mb, axis=0, keepdims=True)
        bnd_ref[0, k + 1:k + 2, :] = carry
    eye = jnp.where(lax.broadcasted_iota(jnp.int32, (V7X_LANES, V7X_LANES), 0)
                    == lax.broadcasted_iota(jnp.int32, (V7X_LANES, V7X_LANES), 1), 1.0, 0.0).astype(BF16)
    mask_t = _dot_nt(eye, mask.astype(BF16))[0:N_EXPERTS]
    carry_t = jnp.zeros((N_EXPERTS, 1), F32)
    for k in range(nb):
        mb = mask_t[:, k * blk:(k + 1) * blk]
        post_ref[0, k] = jnp.where(mb > 0.5, _dot(mb.astype(BF16), upper) + carry_t, -1.0)
        carry_t = carry_t + jnp.sum(mb, axis=1, keepdims=True)


def _moe_route(x, mod3, tpb, g, w_router, b_router, tm, blk):
    t, d = x.shape
    ne = w_router.shape[1]
    nt = t // tm
    nb = tm // blk
    assert nb + 1 <= MOE_BOUNDS_ROWS
    wr = jnp.pad(w_router, ((0, 0), (0, V7X_LANES - ne)))
    br = jnp.pad(b_router, (0, V7X_LANES - ne)).reshape(1, V7X_LANES)
    tok = lambda w: pl.BlockSpec((tm, w), lambda i: (i, 0))
    return pl.pallas_call(
        functools.partial(_moe_route_kernel, blk=blk),
        out_shape=(
            jax.ShapeDtypeStruct((t, d), BF16),
            jax.ShapeDtypeStruct((t, V7X_LANES), F32),
            jax.ShapeDtypeStruct((t, V7X_LANES), F32),
            jax.ShapeDtypeStruct((nt, nb, N_EXPERTS, blk), F32),
            jax.ShapeDtypeStruct((nt, MOE_BOUNDS_ROWS, V7X_LANES), F32),
        ),
        grid=(nt,),
        in_specs=[tok(d), _mod_spec(mod3, tpb), _full((1, d)), _full((d, V7X_LANES)), _full((1, V7X_LANES))],
        out_specs=(tok(d), tok(V7X_LANES), tok(V7X_LANES),
                   pl.BlockSpec((1, nb, N_EXPERTS, blk), lambda i: (i, 0, 0, 0)),
                   pl.BlockSpec((1, MOE_BOUNDS_ROWS, V7X_LANES), lambda i: (i, 0, 0))),
        compiler_params=_cparams("arbitrary"),
        name="moe_route",
    )(x, mod3, g, wr, br)


def _moe_expert_kernel(bnd_ref, h_ref, pos_ref, comb_ref, post_ref, wg_ref, wu_ref, wd_ref, o_ref,
                       hc_scr, acc_scr, posc_scr, combc_scr, *, blk):
    tm, d = h_ref.shape
    nb = tm // blk
    unit = MOE_ROW_UNIT
    i = pl.program_id(0)
    e = pl.program_id(1)
    f = pl.program_id(2)
    nf = pl.num_programs(2)
    base = (i * N_EXPERTS + e) * MOE_BOUNDS_ROWS
    cnt = bnd_ref[base + nb]
    n_units = (cnt + unit - 1) // unit

    def span(u):
        start = u * unit
        stop = jnp.minimum(start + unit, cnt)
        lo = jnp.int32(0)
        hi = jnp.int32(0)
        for k in range(1, nb):
            bk = bnd_ref[base + k]
            lo = lo + (bk <= start).astype(jnp.int32)
            hi = hi + (bk < stop).astype(jnp.int32)
        return lo, hi + 1

    @pl.when((i == 0) & (e == 0) & (f == 0))
    def _():
        acc_scr[...] = jnp.zeros(acc_scr.shape, F32)

    @pl.when((e == 0) & (f == 0))
    def _():
        o_ref[...] = jnp.zeros(o_ref.shape, F32)

    gw = min(MOE_GATHER_BLOCKS, nb)

    @pl.when(f == 0)
    def _():
        def per_unit(u, _):
            lo, hi = span(u)
            r0 = pl.multiple_of(u * unit, unit)
            want = (r0 + lax.broadcasted_iota(jnp.int32, (unit, 1), 0)).astype(F32)
            lo_w = jnp.minimum(lo, nb - gw)
            pos_t = jnp.concatenate([post_ref[0, lo_w + k, pl.ds(e, 1), :] for k in range(gw)], axis=1)
            onehot = jnp.where(pos_t == want, 1.0, 0.0).astype(BF16)
            t0 = pl.multiple_of(lo_w * blk, blk)
            acc_scr[pl.ds(r0, unit), :] = _dot(onehot, h_ref[pl.ds(t0, gw * blk), :])

            def per_blk(k, _):
                t1 = pl.multiple_of(k * blk, blk)
                oh = jnp.where(post_ref[0, k, pl.ds(e, 1), :] == want, 1.0, 0.0).astype(BF16)
                acc_scr[pl.ds(r0, unit), :] += _dot(oh, h_ref[pl.ds(t1, blk), :])
                return 0

            lax.fori_loop(lo_w + gw, hi, per_blk, 0)
            hc_scr[pl.ds(r0, unit), :] = acc_scr[pl.ds(r0, unit), :].astype(BF16)
            acc_scr[pl.ds(r0, unit), :] = jnp.zeros((unit, d), F32)
            return 0

        lax.fori_loop(0, n_units, per_unit, 0)

    def ffn_rows(r0, m):
        hc = hc_scr[pl.ds(r0, m), :]
        a = _dot(hc, wg_ref[0, 0])
        act = (a * _sigmoid(a) * _dot(hc, wu_ref[0, 0])).astype(BF16)
        acc_scr[pl.ds(r0, m), :] += _dot(act, wd_ref[0, 0])

    n_quads = n_units // 4
    rem = n_units - 4 * n_quads

    def quad(k, _):
        ffn_rows(pl.multiple_of(k * 4 * unit, 4 * unit), 4 * unit)
        return 0

    lax.fori_loop(0, n_quads, quad, 0)
    r_tail = pl.multiple_of(n_quads * 4 * unit, 4 * unit)

    @pl.when(rem >= 2)
    def _():
        ffn_rows(r_tail, 2 * unit)

    @pl.when(rem % 2 == 1)
    def _():
        ffn_rows(pl.multiple_of(r_tail + (rem // 2) * 2 * unit, unit), unit)

    @pl.when(f == nf - 1)
    def _():
        lane = lax.broadcasted_iota(jnp.int32, (1, V7X_LANES), 1)
        posc_scr[...] = jnp.sum(jnp.where(lane == e, pos_ref[...], 0.0), axis=-1, keepdims=True)
        combc_scr[...] = jnp.sum(jnp.where(lane == e, comb_ref[...], 0.0), axis=-1, keepdims=True)

        sw = min(MOE_SCATTER_UNITS * unit, tm)

        def per_blk(k, _):
            start = bnd_ref[base + k]
            stop = bnd_ref[base + k + 1]
            t0 = pl.multiple_of(k * blk, blk)
            r0 = pl.multiple_of(jnp.minimum((start // unit) * unit, tm - sw), unit)
            pc = posc_scr[pl.ds(t0, blk), :]
            cc = combc_scr[pl.ds(t0, blk), :]
            want = (r0 + lax.broadcasted_iota(jnp.int32, (1, sw), 1)).astype(F32)
            onehot = jnp.where(pc == want, 1.0, 0.0).astype(BF16)
            o_ref[pl.ds(t0, blk), :] += cc * _dot(onehot, acc_scr[pl.ds(r0, sw), :].astype(BF16))

            def per_unit(u, _):
                r1 = pl.multiple_of(u * unit, unit)
                want1 = (r1 + lax.broadcasted_iota(jnp.int32, (1, unit), 1)).astype(F32)
                oh = jnp.where(pc == want1, 1.0, 0.0).astype(BF16)
                o_ref[pl.ds(t0, blk), :] += cc * _dot(oh, acc_scr[pl.ds(r1, unit), :].astype(BF16))
                return 0

            lax.fori_loop((r0 + sw) // unit, (stop + unit - 1) // unit, per_unit, 0)
            return 0

        lax.fori_loop(0, nb, per_blk, 0)


def _moe_experts(h, pos, comb, post, bounds, w_gu_all, w_down_all, layer, tm, tf, blk):
    t, d = h.shape
    ne, ff = w_down_all.shape[1], w_down_all.shape[2]
    nf = ff // tf
    nt = t // tm
    nb = tm // blk
    bnd = jnp.transpose(bounds[:, :, :ne], (0, 2, 1)).astype(jnp.int32).reshape(nt * ne * MOE_BOUNDS_ROWS)
    once = pl.Buffered(1)
    tok = lambda w: pl.BlockSpec((tm, w), lambda i, e, f, b_: (i, 0), pipeline_mode=once)
    grid_spec = pltpu.PrefetchScalarGridSpec(
        num_scalar_prefetch=1,
        grid=(nt, ne, nf),
        in_specs=[
            tok(d), tok(V7X_LANES), tok(V7X_LANES),
            pl.BlockSpec((1, nb, ne, blk), lambda i, e, f, b_: (i, 0, 0, 0)),
            pl.BlockSpec((1, 1, d, tf), lambda i, e, f, b_: (layer, e, 0, f)),
            pl.BlockSpec((1, 1, d, tf), lambda i, e, f, b_: (layer, e, 0, nf + f)),
            pl.BlockSpec((1, 1, tf, d), lambda i, e, f, b_: (layer, e, f, 0)),
        ],
        out_specs=pl.BlockSpec((tm, d), lambda i, e, f, b_: (i, 0), pipeline_mode=once),
        scratch_shapes=[
            pltpu.VMEM((tm, d), BF16),
            pltpu.VMEM((tm, d), F32),
            pltpu.VMEM((tm, 1), F32),
            pltpu.VMEM((tm, 1), F32),
        ],
    )
    return pl.pallas_call(
        functools.partial(_moe_expert_kernel, blk=blk),
        out_shape=jax.ShapeDtypeStruct((t, d), F32),
        grid_spec=grid_spec,
        compiler_params=_cparams("arbitrary", "arbitrary", "arbitrary"),
        name="moe_experts",
    )(bnd, h, pos, comb, post, w_gu_all, w_gu_all, w_down_all)


def _residual_kernel(x_ref, f_ref, mod_ref, g_ref, o_ref, *, final_norm):
    d = x_ref.shape[1]
    x = x_ref[...] + mod_ref[0][:, 5 * d:6 * d] * f_ref[...]
    if final_norm:
        x = (x * lax.rsqrt(jnp.mean(x * x, axis=-1, keepdims=True) + EPS)) * g_ref[...]
    o_ref[...] = x


def _residual(x, f, mod3, tpb, g_final, tm, final_norm):
    t, d = x.shape
    tok = pl.BlockSpec((tm, d), lambda i: (i, 0))
    return pl.pallas_call(
        functools.partial(_residual_kernel, final_norm=final_norm),
        out_shape=jax.ShapeDtypeStruct((t, d), F32),
        grid=(t // tm,),
        in_specs=[tok, tok, _mod_spec(mod3, tpb), _full((1, d))],
        out_specs=tok,
        compiler_params=_cparams("arbitrary"),
        name="residual",
    )(x, f, mod3, g_final)


def _moe(x, mod3_of, g, w_router, b_router, w_gu_all, w_down_all, layer, g_final, tm, final_norm):
    blk = min(256, tm)
    h, pos, comb, post, bounds = _moe_route(x, *mod3_of(tm), g, w_router, b_router, tm, blk)
    f = _moe_experts(h, pos, comb, post, bounds, w_gu_all, w_down_all, layer, tm, 512, blk)
    tm_res = min(1024, tm)
    return _residual(x, f, *mod3_of(tm_res), g_final, tm_res, final_norm)


def _final_norm_kernel(x_ref, g_ref, o_ref):
    x = x_ref[...]
    o_ref[...] = (x * lax.rsqrt(jnp.mean(x * x, axis=-1, keepdims=True) + EPS)) * g_ref[...]


def _final_norm(x, g, tm):
    t, d = x.shape
    return pl.pallas_call(
        _final_norm_kernel,
        out_shape=jax.ShapeDtypeStruct((t, d), F32),
        grid=(t // tm,),
        in_specs=[pl.BlockSpec((tm, d), lambda i: (i, 0)), pl.BlockSpec((1, d), lambda i: (0, 0))],
        out_specs=pl.BlockSpec((tm, d), lambda i: (i, 0)),
        compiler_params=_cparams("arbitrary"),
        name="final_norm",
    )(x, g)


def _nsa_layer_weights(w_in, w_out, w_ck, pe_ck, w_cv, pe_cv):
    n_main = N_HEADS * HEAD_DIM + 6 * KV_WIDTH
    w_gate = jnp.pad(w_in[:, n_main:], ((0, 0), (0, V7X_LANES - N_HEADS * N_BRANCH)))
    return {"w_main": w_in[:, :n_main].astype(BF16), "w_gate": w_gate.astype(BF16),
            "w_out": w_out.astype(BF16), "cw": _compress_weights(w_ck, pe_ck, w_cv, pe_cv)}


def _rows_from_feature_major(xt):
    b, _, s = xt.shape
    return jnp.transpose(xt.reshape(b, 2, KV_HEADS, HEAD_DIM, s), (0, 4, 1, 2, 3))


def _feature_major_from_rows(x, lead):
    n = len(lead)
    perm = tuple(range(n)) + (n + 1, n + 2, n + 3, n)
    xt = jnp.transpose(x, perm)
    return xt.reshape(lead + (2 * KV_WIDTH, x.shape[n]))


def _nsa_prompt_layer(x, mod, g, lw, b, s_len):
    tm = min(512, s_len)
    mod3 = mod.reshape(b, 1, mod.shape[-1])
    rope_t = _rope_tables(jnp.arange(s_len), V7X_LANES)
    q, gates, cmp_rows, cmpt, selt, wint = _nsa_proj(x, mod3, s_len // tm, g, lw["w_main"], lw["w_gate"], rope_t,
                                                     s_len, tm, True)
    ck, cv = _compress(cmp_rows.reshape(b, s_len, 2 * KV_WIDTH), lw["cw"])
    o = _attn_prompt(q, gates, ck, cv, selt, wint, tq=min(256, s_len))
    x = _out_proj(o, lw["w_out"], x, mod3, s_len // tm, tm)
    return x, cmpt, selt, wint


def _nsa_decode_layer(x, mod3, g, lw, b, t_seq, caches, layer):
    t = x.shape[0]
    page_table = caches["page_table"]
    past_len = page_table.shape[1] * caches["cmp"].shape[3]
    rope_t = _rope_tables(past_len + jnp.arange(t) % t_seq, V7X_LANES)
    q, gates, cmp_rows, sel_rows, win_rows = _nsa_proj(x, mod3, 1, g, lw["w_main"], lw["w_gate"], rope_t, t, t, False)
    pad3 = lambda a: jnp.pad(a.reshape(b, t_seq, a.shape[-1]), ((0, 0), (0, DECODE_TOK_PAD - t_seq), (0, 0)))
    o = _attn_decode(page_table, pad3(q), pad3(gates), pad3(sel_rows), pad3(win_rows),
                     caches["cmp"], caches["sel"], caches["win"], layer, lw["cw"])
    o = o[:, :t_seq].reshape(t, o.shape[-1]).astype(BF16)
    x = _out_proj(o, lw["w_out"], x, mod3, 1, t)
    return x, cmp_rows, sel_rows, win_rows


def _trunk(x3, mod, caches, p):
    b, t_seq, d = x3.shape
    t = b * t_seq
    kvw = 2 * KV_WIDTH
    prompt = caches is None
    x = x3.reshape(t, d)
    new_cmp, new_sel, new_win, new_conv = [], [], [], []
    depth = mod.shape[0]
    g_final = p["g_norm_final"].reshape(1, d)
    finished = False
    for i in range(depth):
        j = i // 2
        if prompt:
            mod_i = mod[i].reshape(b, 1, 6 * d)
            mod3_of = lambda tm, mod_i=mod_i: (mod_i, t_seq // tm)
            tm_small, tm_big, tm_moe = min(512, t_seq), min(1024, t_seq), min(2048, t_seq)
        else:
            mod_i = jnp.repeat(mod[i], t_seq, axis=0).reshape(1, t, 6 * d)
            mod3_of = lambda tm, mod_i=mod_i: (mod_i, 1)
            tm_small = tm_big = tm_moe = t
        g_mix = p["g_norm_mix"][i].reshape(1, d)
        g_ffn = p["g_norm_ffn"][i].reshape(1, d)
        if i % 2 == 0:
            lw = _nsa_layer_weights(p["w_nsa_in"][j], p["w_nsa_out"][j], p["w_cmp_k"][j], p["pe_cmp_k"][j],
                                    p["w_cmp_v"][j], p["pe_cmp_v"][j])
            if prompt:
                x, cmpt, selt, wint = _nsa_prompt_layer(x, mod[i], g_mix, lw, b, t_seq)
                new_cmp.append(_rows_from_feature_major(cmpt))
                new_sel.append(_rows_from_feature_major(selt))
                new_win.append(_rows_from_feature_major(wint[:, :, t_seq - min(WINDOW, t_seq):]))
            else:
                x, c_rows, s_rows, w_rows = _nsa_decode_layer(x, mod_i, g_mix, lw, b, t_seq, caches, j)
                rows5 = lambda r: r.reshape(b, t_seq, 2, KV_HEADS, HEAD_DIM)
                new_cmp.append(rows5(c_rows))
                new_sel.append(rows5(s_rows))
                new_win.append(jnp.concatenate([caches["win_rows"][j], rows5(w_rows)], axis=1)[:, t_seq:])
            x = _ffn(x, *mod3_of(tm_big), g_ffn, p["w_ff_gu"], p["w_ff_down"], j, tm_big, 512)
        else:
            state = None if prompt else caches["conv"][j]
            x, st = _conv_mixer(x, *mod3_of(tm_small), g_mix, p["w_conv_in"][j].astype(BF16), p["w_conv"][j],
                                p["w_conv_out"][j].astype(BF16), state, b, t_seq, tm_small)
            new_conv.append(st)
            finished = i == depth - 1
            x = _moe(x, mod3_of, g_ffn, p["w_router"][j], p["b_router"][j], p["w_moe_gu"], p["w_moe_down"], j,
                     g_final, tm_moe, finished)
    y = x if finished else _final_norm(x, g_final, tm_big)
    return (y.reshape(b, t_seq, d), jnp.stack(new_cmp), jnp.stack(new_sel), jnp.stack(new_win), jnp.stack(new_conv))


def kernel(x_prompt, x_sample, cache_cmp_kv, cache_sel_kv, cache_win_kv, state_conv, page_table, c_prompt, c_sample,
           w_ada, b_ada, g_norm_mix, g_norm_ffn, g_norm_final, w_nsa_in, w_nsa_out, w_cmp_k, pe_cmp_k, w_cmp_v,
           pe_cmp_v, w_conv_in, w_conv, w_conv_out, w_ff_gu, w_ff_down, w_router, b_router, w_moe_gu, w_moe_down):
    p = {"g_norm_mix": g_norm_mix, "g_norm_ffn": g_norm_ffn, "g_norm_final": g_norm_final,
         "w_nsa_in": w_nsa_in, "w_nsa_out": w_nsa_out, "w_cmp_k": w_cmp_k, "pe_cmp_k": pe_cmp_k, "w_cmp_v": w_cmp_v,
         "pe_cmp_v": pe_cmp_v, "w_conv_in": w_conv_in, "w_conv": w_conv, "w_conv_out": w_conv_out,
         "w_ff_gu": w_ff_gu, "w_ff_down": w_ff_down, "w_router": w_router, "b_router": b_router,
         "w_moe_gu": w_moe_gu.astype(BF16), "w_moe_down": w_moe_down.astype(BF16)}
    nb = x_prompt.shape[0]
    mod = _ada(jnp.concatenate([c_prompt, c_sample], axis=0), w_ada, b_ada)
    y_p, cmp_p, sel_p, win_p, conv_p = _trunk(x_prompt, mod[:, :nb], None, p)
    caches = {"cmp": _feature_major_from_rows(cache_cmp_kv, cache_cmp_kv.shape[:2]),
              "sel": _feature_major_from_rows(cache_sel_kv, cache_sel_kv.shape[:2]),
              "win": _feature_major_from_rows(cache_win_kv, cache_win_kv.shape[:2]),
              "win_rows": cache_win_kv, "conv": state_conv, "page_table": page_table}
    y_s, cmp_s, sel_s, win_s, conv_s = _trunk(x_sample, mod[:, nb:], caches, p)
    return (y_p, y_s, cmp_p, cmp_s, sel_p, sel_s, win_p, win_s, conv_p, conv_s)
```

```python
import functools

import jax
import jax.numpy as jnp
from jax import lax
from jax.experimental import pallas as pl
from jax.experimental.pallas import tpu as pltpu

F32 = jnp.float32
BF16 = jnp.bfloat16

N_HEADS = 16
HEAD_DIM = 64
KV_HEADS = 4
HEADS_PER_KV = N_HEADS // KV_HEADS
KV_WIDTH = KV_HEADS * HEAD_DIM
ROT_DIM = HEAD_DIM // 4
ROPE_THETA = 500000.0
CMP_BLOCK = 32
SEL_BLOCK = 64
TOP_N = 8
WINDOW = 512
N_BRANCH = 3
FORCE_BONUS = float(HEADS_PER_KV + 1)
N_EXPERTS = 8
EPS = 1e-6
NEG_INF = float("-inf")

V7X_LANES = 128
V7X_SUBLANES = 8
V7X_VMEM_LIMIT_BYTES = 58 * 1024 * 1024


def _cparams(*sem):
    return pltpu.CompilerParams(dimension_semantics=sem, vmem_limit_bytes=V7X_VMEM_LIMIT_BYTES)


def _dot(a, b):
    return jnp.dot(a, b, preferred_element_type=F32)


def _dot_nt(a, b):
    return lax.dot_general(a, b, (((1,), (1,)), ((), ())), preferred_element_type=F32)


def _norm_mod(x, g, sc, sh):
    y = x * lax.rsqrt(jnp.mean(x * x, axis=-1, keepdims=True) + EPS)
    return (y * g) * (1.0 + sc) + sh


def _sigmoid(x):
    return 1.0 / (1.0 + jnp.exp(-x))


def _mod_spec(mod3, tiles_per_block):
    _, r, w = mod3.shape
    return pl.BlockSpec((1, r, w), lambda i, *_: (i // tiles_per_block, 0, 0))


def _full(shape):
    return pl.BlockSpec(shape, lambda *_: (0,) * len(shape))


def _ada_kernel(c_ref, w_ref, b_ref, o_ref):
    c = c_ref[...]
    cs = (c * _sigmoid(c)).astype(BF16)
    o_ref[0] = _dot(cs, w_ref[0].astype(BF16)) + b_ref[0]


def _ada(c, w_ada, b_ada):
    depth, d, n = w_ada.shape
    nb = c.shape[0]
    tn = 1536
    return pl.pallas_call(
        _ada_kernel,
        out_shape=jax.ShapeDtypeStruct((depth, nb, n), F32),
        grid=(depth, n // tn),
        in_specs=[
            pl.BlockSpec((nb, d), lambda i, j: (0, 0)),
            pl.BlockSpec((1, d, tn), lambda i, j: (i, 0, j)),
            pl.BlockSpec((1, 1, tn), lambda i, j: (i, 0, j)),
        ],
        out_specs=pl.BlockSpec((1, nb, tn), lambda i, j: (i, 0, j)),
        compiler_params=_cparams("arbitrary", "arbitrary"),
        name="ada",
    )(c, w_ada, b_ada.reshape(depth, 1, n))


def _rope_tables(pos, width):
    half = ROT_DIM // 2
    inv = 1.0 / (ROPE_THETA ** (jnp.arange(half, dtype=F32) * (2.0 / ROT_DIM)))
    ang = pos.astype(F32)[:, None] * inv[None, :]
    cos, sin = jnp.cos(ang), jnp.sin(ang)
    n = pos.shape[0]
    z8 = jnp.zeros((n, half), F32)
    rest0 = jnp.zeros((n, HEAD_DIM - ROT_DIM), F32)
    c = jnp.concatenate([cos, cos, rest0 + 1.0], axis=1)
    a = jnp.concatenate([-sin, z8, rest0], axis=1)
    b = jnp.concatenate([z8, sin, rest0], axis=1)
    rep = width // HEAD_DIM
    return tuple(jnp.tile(t, (1, rep)) for t in (c, a, b))


def _rope(v, c, a, b):
    w = v.shape[-1]
    half = ROT_DIM // 2
    return v * c + pltpu.roll(v, w - half, 1) * a + pltpu.roll(v, half, 1) * b


def _widen(t, width):
    return jnp.concatenate([t] * (width // t.shape[-1]), axis=1)


def _nsa_proj_kernel(x_ref, mod_ref, g_ref, w_ref, wg_ref, rc_ref, ra_ref, rb_ref, q_ref, gate_ref, *kv_refs,
                     feature_major):
    d = x_ref.shape[1]
    mod = mod_ref[0]
    h = _norm_mod(x_ref[...], g_ref[...], mod[:, d:2 * d], mod[:, 0:d]).astype(BF16)
    kvw = 2 * KV_WIDTH
    c1, a1, b1 = rc_ref[...], ra_ref[...], rb_ref[...]
    c2, a2, b2 = (_widen(t, kvw) for t in (c1, a1, b1))
    scale = HEAD_DIM ** -0.5
    hq = N_HEADS * HEAD_DIM
    for j in range(hq // kvw):
        p = _dot(h, w_ref[:, j * kvw:(j + 1) * kvw])
        q_ref[:, j * kvw:(j + 1) * kvw] = (_rope(p, c2, a2, b2) * scale).astype(BF16)
    gate_ref[...] = _sigmoid(_dot(h, wg_ref[...]))
    one = jnp.ones_like(c1)
    zero = jnp.zeros_like(c1)
    ck = jnp.concatenate([_widen(c1, KV_WIDTH), _widen(one, KV_WIDTH)], axis=1)
    ak = jnp.concatenate([_widen(a1, KV_WIDTH), _widen(zero, KV_WIDTH)], axis=1)
    bk = jnp.concatenate([_widen(b1, KV_WIDTH), _widen(zero, KV_WIDTH)], axis=1)
    cmp_rows = _dot(h, w_ref[:, hq:hq + kvw])
    sel_rows = _rope(_dot(h, w_ref[:, hq + kvw:hq + 2 * kvw]), ck, ak, bk)
    win_rows = _rope(_dot(h, w_ref[:, hq + 2 * kvw:hq + 3 * kvw]), ck, ak, bk)
    if feature_major:
        cmp_ref, cmpt_ref, selt_ref, wint_ref = kv_refs
        cmp_ref[...] = cmp_rows
        cmpt_ref[0] = cmp_rows.T
        selt_ref[0] = sel_rows.T
        wint_ref[0] = win_rows.T
    else:
        cmp_ref, sel_ref, win_ref = kv_refs
        cmp_ref[...] = cmp_rows
        sel_ref[...] = sel_rows
        win_ref[...] = win_rows


def _nsa_proj(x, mod3, tpb, g, w_main, w_gate, rope_t, rows_per_seq, tm, feature_major):
    t, d = x.shape
    nw = w_main.shape[1]
    kvw = 2 * KV_WIDTH
    hq = N_HEADS * HEAD_DIM
    nrt = rows_per_seq // tm
    rspec = pl.BlockSpec((tm, V7X_LANES), lambda i: (i % nrt, 0))
    tok = lambda w: pl.BlockSpec((tm, w), lambda i: (i, 0))
    out_shape = [jax.ShapeDtypeStruct((t, hq), BF16), jax.ShapeDtypeStruct((t, V7X_LANES), F32)]
    out_specs = [tok(hq), tok(V7X_LANES)]
    if feature_major:
        b = t // rows_per_seq
        fm = pl.BlockSpec((1, kvw, tm), lambda i: (i // nrt, 0, i % nrt))
        out_shape += [jax.ShapeDtypeStruct((t, kvw), F32)] + [jax.ShapeDtypeStruct((b, kvw, rows_per_seq), F32)] * 3
        out_specs += [tok(kvw), fm, fm, fm]
    else:
        out_shape += [jax.ShapeDtypeStruct((t, kvw), F32)] * 3
        out_specs += [tok(kvw)] * 3
    return pl.pallas_call(
        functools.partial(_nsa_proj_kernel, feature_major=feature_major),
        out_shape=tuple(out_shape),
        grid=(t // tm,),
        in_specs=[tok(d), _mod_spec(mod3, tpb), _full((1, d)), _full((d, nw)), _full((d, V7X_LANES)),
                  rspec, rspec, rspec],
        out_specs=tuple(out_specs),
        compiler_params=_cparams("arbitrary"),
        name="nsa_proj",
    )(x, mod3, g, w_main, w_gate, *rope_t)


def _compress_rows(load_rows, bdk_ref, bdv_ref, pek_ref, pev_ref, nc, perm_scr):
    half = nc // 2
    n_chunks = 2 * KV_WIDTH // V7X_LANES
    outs = []
    for c in range(n_chunks):
        pe_ref, w_ref = (pek_ref, bdk_ref) if c < n_chunks // 2 else (pev_ref, bdv_ref)
        cols = [(load_rows(c, l) + pe_ref[l:l + 1, :]).astype(BF16) for l in range(CMP_BLOCK)]
        perm_scr[...] = _dot(jnp.concatenate(cols, axis=1), w_ref[...])
        outs.append(jnp.concatenate([perm_scr[pl.ds(0, half, stride=2), :], perm_scr[pl.ds(1, half, stride=2), :]],
                                    axis=0))
    return (jnp.concatenate(outs[0:n_chunks // 2], axis=1), jnp.concatenate(outs[n_chunks // 2:], axis=1))


def _compress_kernel(r0_ref, r1_ref, r2_ref, r3_ref, bdk_ref, bdv_ref, pek_ref, pev_ref, rc_ref, ra_ref, rb_ref,
                     ck_ref, cv_ref, perm_scr):
    nc = ck_ref.shape[1]

    chunk_refs = (r0_ref, r1_ref, r2_ref, r3_ref)

    def load_rows(c, l):
        return chunk_refs[c][0, pl.ds(l, nc, stride=CMP_BLOCK), :]

    acck, accv = _compress_rows(load_rows, bdk_ref, bdv_ref, pek_ref, pev_ref, nc, perm_scr)
    ck_ref[0] = _rope(acck, rc_ref[...], ra_ref[...], rb_ref[...]).astype(BF16)
    cv_ref[0] = accv.astype(BF16)


def _cmp_order_end_pos(nc):
    n = jnp.arange(nc)
    half = nc // 2
    blk = jnp.where(n < half, 2 * n, 2 * (n - half) + 1)
    return blk * CMP_BLOCK + (CMP_BLOCK - 1)


def _cmp_end_row(nc):
    half = nc // 2
    col = lax.broadcasted_iota(jnp.int32, (1, nc), 1)
    return jnp.where(col < half, 2 * CMP_BLOCK * col + (CMP_BLOCK - 1),
                     2 * CMP_BLOCK * (col - half) + (2 * CMP_BLOCK - 1))


def _compress(rows3, cw):
    b, tk, w = rows3.shape
    nc = tk // CMP_BLOCK
    assert nc % 2 == 0 and nc * CMP_BLOCK == tk
    rt = _rope_tables(_cmp_order_end_pos(nc), KV_WIDTH)
    return pl.pallas_call(
        _compress_kernel,
        out_shape=(jax.ShapeDtypeStruct((b, nc, KV_WIDTH), BF16),) * 2,
        grid=(b,),
        in_specs=[
            *[pl.BlockSpec((1, tk, V7X_LANES), functools.partial(lambda i, c: (i, 0, c), c=c))
              for c in range(w // V7X_LANES)],
            _full(cw["bdk"].shape), _full(cw["bdv"].shape), _full(cw["pek"].shape), _full(cw["pev"].shape),
            _full(rt[0].shape), _full(rt[1].shape), _full(rt[2].shape),
        ],
        out_specs=(pl.BlockSpec((1, nc, KV_WIDTH), lambda i: (i, 0, 0)),) * 2,
        scratch_shapes=[pltpu.VMEM((nc, V7X_LANES), F32)],
        compiler_params=_cparams("arbitrary"),
        name="compress",
    )(rows3, rows3, rows3, rows3, cw["bdk"], cw["bdv"], cw["pek"], cw["pev"], *rt)


def _compress_weights(w_ck, pe_ck, w_cv, pe_cv):
    pair = V7X_LANES // HEAD_DIM
    eye = jnp.eye(pair, dtype=F32)

    def bd(w):
        return jnp.einsum("gh,lde->lgdhe", eye, w).reshape(CMP_BLOCK * V7X_LANES, V7X_LANES).astype(BF16)

    return {"bdk": bd(w_ck), "bdv": bd(w_cv),
            "pek": jnp.tile(pe_ck, (1, pair)), "pev": jnp.tile(pe_cv, (1, pair))}


def _softmax_rows(s):
    m = jnp.max(s, axis=-1, keepdims=True)
    m = jnp.where(m == NEG_INF, 0.0, m)
    e = jnp.exp(s - m)
    return e / jnp.maximum(jnp.sum(e, axis=-1, keepdims=True), 1e-30)


def _masked_softmax(s, mask):
    return _softmax_rows(jnp.where(mask, s, NEG_INF))


def _select_blocks(imp, t_col, ns):
    blk = lax.broadcasted_iota(jnp.int32, (1, ns), 1)
    cur = t_col // SEL_BLOCK
    forced = jnp.where(blk == 0, 1.0, jnp.where(blk == cur, 1.0, 0.0))
    score = jnp.where(blk <= cur, imp + FORCE_BONUS * forced, NEG_INF)
    rank = jnp.zeros(score.shape, F32)
    for i in range(ns):
        ci = score[:, i:i + 1]
        later = jnp.where(blk > i, 1.0, 0.0)
        rank = rank + jnp.where(ci > score, 1.0, jnp.where(ci == score, later, 0.0))
    return jnp.where(rank < float(min(TOP_N, ns)), 1.0, 0.0)


def _select_blocks_t(imp_t, t_row, ns):
    blk = lax.broadcasted_iota(jnp.int32, (ns, 1), 0)
    cur = t_row // SEL_BLOCK
    forced = jnp.where(blk == 0, 1.0, jnp.where(blk == cur, 1.0, 0.0))
    score = jnp.where(blk <= cur, imp_t + FORCE_BONUS * forced, NEG_INF)
    rank = jnp.zeros(score.shape, F32)
    for i in range(ns):
        ci = score[i:i + 1, :]
        later = jnp.where(blk > i, 1.0, 0.0)
        rank = rank + jnp.where(ci > score, 1.0, jnp.where(ci == score, later, 0.0))
    return jnp.where(rank < float(min(TOP_N, ns)), 1.0, 0.0)


def _stack_heads(x, g):
    base = g * HEADS_PER_KV * HEAD_DIM
    return jnp.concatenate([x[:, base + r * HEAD_DIM: base + (r + 1) * HEAD_DIM] for r in range(HEADS_PER_KV)],
                           axis=0)


def _gate_cols(gates, g, branch):
    cols = [(g * HEADS_PER_KV + r) * N_BRANCH + branch for r in range(HEADS_PER_KV)]
    return jnp.concatenate([gates[:, c:c + 1] for c in cols], axis=0)


def _sum_rows(n):
    row = lax.broadcasted_iota(jnp.int32, (HEAD_DIM, n), 0)
    return jnp.where(row == 0, 1.0, 0.0).astype(BF16)


def _pad_to(x, rows, cols):
    r, c = x.shape
    if c < cols:
        x = jnp.concatenate([x, jnp.zeros((r, cols - c), x.dtype)], axis=1)
    if r < rows:
        x = jnp.concatenate([x, jnp.zeros((rows - r, x.shape[1]), x.dtype)], axis=0)
    return x


WIN_CHUNK = V7X_LANES
MASK_BIG = 2.0 ** 100


def _attn_prompt_kernel(q_ref, gate_ref, ck_ref, cv_ref, selt_ref, wint_ref, o_ref,
                        ks_scr, vs_scr, kw_scr, vw_scr, *, tk):
    tq = q_ref.shape[0]
    s_len = selt_ref.shape[2]
    nc = ck_ref.shape[1]
    half = nc // 2
    ns = s_len // SEL_BLOCK
    qi = pl.program_id(1)
    k_pad = ks_scr.shape[2] - HEAD_DIM - ns

    @pl.when(qi == 0)
    def _():
        for c in range(s_len // tk):
            kpos = c * tk + lax.broadcasted_iota(jnp.int32, (1, tk), 1)
            brow = lax.broadcasted_iota(jnp.int32, (ns + k_pad, 1), 0)
            expand = jnp.where(kpos // SEL_BLOCK == brow, 1.0, 0.0).astype(BF16)
            for g in range(KV_HEADS):
                k_g = selt_ref[0, g * HEAD_DIM:(g + 1) * HEAD_DIM, c * tk:(c + 1) * tk].astype(BF16)
                ks_scr[c, g] = jnp.concatenate([k_g, expand], axis=0)
                v_g = selt_ref[0, KV_WIDTH + g * HEAD_DIM:KV_WIDTH + (g + 1) * HEAD_DIM, c * tk:(c + 1) * tk]
                vs_scr[c, g] = jnp.concatenate([v_g.astype(BF16), _sum_rows(tk)], axis=0)
        for c in range(s_len // WIN_CHUNK):
            kw_scr[c] = wint_ref[0, 0:KV_WIDTH, c * WIN_CHUNK:(c + 1) * WIN_CHUNK].astype(BF16)
            vw_scr[c] = wint_ref[0, KV_WIDTH:2 * KV_WIDTH, c * WIN_CHUNK:(c + 1) * WIN_CHUNK].astype(BF16)

    q0 = qi * tq
    t_tile = q0 + lax.broadcasted_iota(jnp.int32, (tq, 1), 0)
    t_row = q0 + lax.broadcasted_iota(jnp.int32, (1, tq), 1)
    t_col4 = jnp.concatenate([t_tile] * HEADS_PER_KV, axis=0)
    q_all = q_ref[...]
    gates = gate_ref[...]
    c_end = _cmp_end_row(nc)
    last = q0 // tk
    kpos_last = last * tk + lax.broadcasted_iota(jnp.int32, (1, tk), 1)
    causal = jnp.where(kpos_last <= t_tile, 0.0, -MASK_BIG)
    causal4 = jnp.concatenate([causal] * HEADS_PER_KV, axis=0)

    wlen = min(WINDOW + tq, s_len)
    n_wc = wlen // WIN_CHUNK
    w0 = jnp.clip(q0 + tq - wlen, 0, s_len - wlen)
    wc0 = w0 // WIN_CHUNK
    wpos = w0 + lax.broadcasted_iota(jnp.int32, (1, wlen), 1)
    wbias = jnp.where(wpos <= t_tile, jnp.where(wpos >= t_tile - WINDOW, 0.0, NEG_INF), NEG_INF)
    wbias4 = jnp.concatenate([wbias] * HEADS_PER_KV, axis=0)

    for g in range(KV_HEADS):
        rows_g = slice(g * HEAD_DIM, (g + 1) * HEAD_DIM)
        q4 = _stack_heads(q_all, g)
        p_c = _masked_softmax(_dot_nt(q4, ck_ref[0, :, rows_g]), c_end <= t_col4)
        o_c = _dot(p_c.astype(BF16), cv_ref[0, :, rows_g])
        psum = p_c[0:tq]
        for r in range(1, HEADS_PER_KV):
            psum = psum + p_c[r * tq:(r + 1) * tq]
        psum_t = _pad_to(psum, tq, V7X_LANES).T
        imp_t = psum_t[0:half] + psum_t[half:nc]
        sel_t = _select_blocks_t(imp_t, t_row, ns)
        sel = _pad_to(sel_t, V7X_LANES, tq).T[:, 0:ns + k_pad]
        unchosen = ((sel - 1.0) * MASK_BIG).astype(BF16)
        q_ext = jnp.concatenate([q4, jnp.concatenate([unchosen] * HEADS_PER_KV, axis=0)], axis=1)

        kw = jnp.concatenate([kw_scr[wc0 + i, rows_g, :] for i in range(n_wc)], axis=1)
        vw = jnp.concatenate([vw_scr[wc0 + i, rows_g, :] for i in range(n_wc)], axis=1)
        vw = jnp.concatenate([vw, _sum_rows(wlen)], axis=0)

        def attend(c, carry, bias, width):
            m, acc = carry
            s = _dot(q_ext, ks_scr[c, g, :, 0:width])
            if bias is not None:
                s = s + bias[:, 0:width]
            m_new = jnp.maximum(m, jnp.max(s, axis=-1, keepdims=True))
            p = jnp.exp(s - m_new)
            acc = jnp.exp(m - m_new) * acc + _dot_nt(p.astype(BF16), vs_scr[c, g, :, 0:width])
            return m_new, acc

        rows = HEADS_PER_KV * tq
        init = (jnp.full((rows, 1), NEG_INF, F32), jnp.zeros((rows, 2 * HEAD_DIM), F32))
        carry = lax.fori_loop(0, last, lambda c, carry: attend(c, carry, None, tk), init)
        if tq < tk:
            _, acc_s = lax.cond(q0 % tk == 0, lambda cr: attend(last, cr, causal4, tq),
                                lambda cr: attend(last, cr, causal4, tk), carry)
        else:
            _, acc_s = attend(last, carry, causal4, tk)
        o_s = acc_s[:, 0:HEAD_DIM] / jnp.maximum(acc_s[:, HEAD_DIM:HEAD_DIM + 1], 1e-30)

        s_w = _dot(q4, kw) + wbias4
        m_w = jnp.max(s_w, axis=-1, keepdims=True)
        p_w = jnp.exp(s_w - jnp.where(m_w == NEG_INF, 0.0, m_w))
        acc_w = _dot_nt(p_w.astype(BF16), vw)
        o_w = acc_w[:, 0:HEAD_DIM] / jnp.maximum(acc_w[:, HEAD_DIM:HEAD_DIM + 1], 1e-30)

        o = _gate_cols(gates, g, 0) * o_c + _gate_cols(gates, g, 1) * o_s + _gate_cols(gates, g, 2) * o_w
        for r in range(HEADS_PER_KV):
            head = g * HEADS_PER_KV + r
            o_ref[:, head * HEAD_DIM:(head + 1) * HEAD_DIM] = o[r * tq:(r + 1) * tq].astype(BF16)


def _attn_prompt(q, gates, ck, cv, selt, wint, tq):
    b, kvw, s_len = selt.shape
    nc = ck.shape[1]
    hq = N_HEADS * HEAD_DIM
    nq = s_len // tq
    tk = min(512, s_len)
    assert s_len % tk == 0 and s_len % tq == 0 and tk % tq == 0 and tq % V7X_LANES == 0 and nc <= V7X_LANES
    assert min(WINDOW + tq, s_len) % WIN_CHUNK == 0 and HEAD_DIM + s_len // SEL_BLOCK <= V7X_LANES
    return pl.pallas_call(
        functools.partial(_attn_prompt_kernel, tk=tk),
        out_shape=jax.ShapeDtypeStruct((b * s_len, hq), BF16),
        grid=(b, nq),
        in_specs=[
            pl.BlockSpec((tq, hq), lambda i, j: (i * nq + j, 0)),
            pl.BlockSpec((tq, V7X_LANES), lambda i, j: (i * nq + j, 0)),
            pl.BlockSpec((1, nc, KV_WIDTH), lambda i, j: (i, 0, 0)),
            pl.BlockSpec((1, nc, KV_WIDTH), lambda i, j: (i, 0, 0)),
            pl.BlockSpec((1, kvw, s_len), lambda i, j: (i, 0, 0)),
            pl.BlockSpec((1, kvw, s_len), lambda i, j: (i, 0, 0)),
        ],
        out_specs=pl.BlockSpec((tq, hq), lambda i, j: (i * nq + j, 0)),
        scratch_shapes=[
            pltpu.VMEM((s_len // tk, KV_HEADS, V7X_LANES, tk), BF16),
            pltpu.VMEM((s_len // tk, KV_HEADS, 2 * HEAD_DIM, tk), BF16),
            pltpu.VMEM((s_len // WIN_CHUNK, KV_WIDTH, WIN_CHUNK), BF16),
            pltpu.VMEM((s_len // WIN_CHUNK, KV_WIDTH, WIN_CHUNK), BF16),
        ],
        compiler_params=_cparams("arbitrary", "arbitrary"),
        name="attn_prompt",
    )(q, gates, ck, cv, selt, wint)


PAGES_PER_STEP = 16
CMP_PITCH = CMP_BLOCK + V7X_SUBLANES
DECODE_TOK_PAD = V7X_SUBLANES


def _block_diag_q(q):
    parts = []
    for g in range(KV_HEADS):
        q4 = _stack_heads(q, g)
        z = jnp.zeros_like(q4)
        parts.append(jnp.concatenate([q4 if gg == g else z for gg in range(KV_HEADS)], axis=1))
    return jnp.concatenate(parts, axis=0)


def _attn_decode_kernel(pt_ref, q_ref, gate_ref, nsel_ref, nwin_ref, wbuf_ref, bdk_ref, bdv_ref, pek_ref, pev_ref,
                        rc_ref, ra_ref, rb_ref, *rest, page, n_pages):
    pps = PAGES_PER_STEP
    cpages = rest[:pps]
    spages = rest[pps:2 * pps]
    o_ref = rest[2 * pps]
    cmp_scr, s_scr, vs_scr, qbd_scr, perm_scr = rest[2 * pps + 1:]
    blocks_per_page = page // CMP_BLOCK
    del pt_ref
    j = pl.program_id(1)
    tq = q_ref.shape[1]
    past_len = n_pages * page
    n_lane_chunks = 2 * KV_WIDTH // V7X_LANES

    @pl.when(j == 0)
    def _():
        qbd_scr[...] = _block_diag_q(q_ref[0].astype(F32)).astype(BF16)

    qbd = qbd_scr[...]
    for p in range(pps):
        for c in range(n_lane_chunks):
            rows_pc = cpages[p][0, 0, c * V7X_LANES:(c + 1) * V7X_LANES, :].T
            for n in range(blocks_per_page):
                r0 = pl.multiple_of(((j * pps + p) * blocks_per_page + n) * CMP_PITCH, V7X_SUBLANES)
                cmp_scr[c, pl.ds(r0, CMP_BLOCK), :] = rows_pc[n * CMP_BLOCK:(n + 1) * CMP_BLOCK]
    k_step = jnp.concatenate([spages[p][0, 0, 0:KV_WIDTH, :] for p in range(pps)], axis=1).astype(BF16)
    s_scr[j] = _dot(qbd, k_step)
    vs_scr[j] = jnp.concatenate([spages[p][0, 0, KV_WIDTH:2 * KV_WIDTH, :] for p in range(pps)],
                                axis=1).astype(BF16)

    @pl.when(j == pl.num_programs(1) - 1)
    def _():
        tok = lax.broadcasted_iota(jnp.int32, (tq, 1), 0)
        t_rows = past_len + jnp.concatenate([tok] * N_HEADS, axis=0)
        nsel = _pad_to(nsel_ref[0], page, 2 * KV_WIDTH)
        s_new = _dot_nt(qbd, nsel[:, 0:KV_WIDTH].astype(BF16))

        nc = past_len // CMP_BLOCK
        half = nc // 2

        def load_rows(c, l):
            return cmp_scr[c, pl.ds(l, nc, stride=CMP_PITCH), :]

        acck, accv = _compress_rows(load_rows, bdk_ref, bdv_ref, pek_ref, pev_ref, nc, perm_scr)
        ck = _rope(acck, rc_ref[...], ra_ref[...], rb_ref[...]).astype(BF16)
        p_c = _masked_softmax(_dot_nt(qbd, ck), _cmp_end_row(nc) <= t_rows)
        o_c = _dot(p_c.astype(BF16), accv.astype(BF16))

        ns = n_pages * page // SEL_BLOCK + 1
        grp = HEADS_PER_KV * tq
        psum = []
        for g in range(KV_HEADS):
            pg_ = p_c[g * grp:g * grp + tq]
            for r in range(1, HEADS_PER_KV):
                pg_ = pg_ + p_c[g * grp + r * tq:g * grp + (r + 1) * tq]
            psum.append(pg_)
        psum = jnp.concatenate(psum, axis=0)
        imp = jnp.concatenate([psum[:, 0:half] + psum[:, half:nc], jnp.zeros((KV_HEADS * tq, 1), F32)], axis=1)
        selm = _select_blocks(imp, past_len + jnp.concatenate([tok] * KV_HEADS, axis=0), ns)
        selm_rows = jnp.concatenate(
            [selm[g * tq:(g + 1) * tq] for g in range(KV_HEADS) for _ in range(HEADS_PER_KV)], axis=0)
        nkeys = (n_pages + 1) * page
        kpos = lax.broadcasted_iota(jnp.int32, (1, nkeys), 1)
        brow = lax.broadcasted_iota(jnp.int32, (ns, 1), 0)
        expand = jnp.where(kpos // SEL_BLOCK == brow, 1.0, 0.0).astype(BF16)
        chosen = _dot(selm_rows.astype(BF16), expand)
        n_steps = n_pages // pps
        s_all = jnp.concatenate([s_scr[i] for i in range(n_steps)] + [s_new], axis=1)
        s_all = jnp.where(kpos <= t_rows, jnp.where(chosen > 0.5, s_all, NEG_INF), NEG_INF)
        p_s = _softmax_rows(s_all).astype(BF16)
        vt_all = jnp.concatenate([vs_scr[i] for i in range(n_steps)], axis=1)
        o_s = _dot_nt(p_s[:, 0:past_len], vt_all) + _dot(p_s[:, past_len:nkeys], nsel[:, KV_WIDTH:].astype(BF16))

        wb = wbuf_ref.shape[3]
        nwin = _pad_to(nwin_ref[0], page, 2 * KV_WIDTH)
        s_w = jnp.concatenate([_dot(qbd, wbuf_ref[0, 0, 0:KV_WIDTH, :].astype(BF16)),
                               _dot_nt(qbd, nwin[:, 0:KV_WIDTH].astype(BF16))], axis=1)
        wpos = past_len - wb + lax.broadcasted_iota(jnp.int32, (1, wb + page), 1)
        s_w = jnp.where(wpos <= t_rows, jnp.where(wpos >= t_rows - WINDOW, s_w, NEG_INF), NEG_INF)
        p_w = _softmax_rows(s_w).astype(BF16)
        o_w = (_dot_nt(p_w[:, 0:wb], wbuf_ref[0, 0, KV_WIDTH:2 * KV_WIDTH, :].astype(BF16))
               + _dot(p_w[:, wb:wb + page], nwin[:, KV_WIDTH:].astype(BF16)))

        gates = gate_ref[0]
        gcol = lambda br: jnp.concatenate([_gate_cols(gates, g, br) for g in range(KV_HEADS)], axis=0)
        o = gcol(0) * o_c + gcol(1) * o_s + gcol(2) * o_w
        for g in range(KV_HEADS):
            for r in range(HEADS_PER_KV):
                col = (g * HEADS_PER_KV + r) * HEAD_DIM
                o_ref[0, :, col:col + HEAD_DIM] = o[g * grp + r * tq:g * grp + (r + 1) * tq,
                                                    g * HEAD_DIM:(g + 1) * HEAD_DIM]


def _attn_decode(page_table, q3, gates3, nsel3, nwin3, cache_cmp_t, cache_sel_t, cache_win_t, layer, cw):
    b, tq, hq = q3.shape
    n_pages = page_table.shape[1]
    page = cache_cmp_t.shape[3]
    kvw = 2 * KV_WIDTH
    pps = PAGES_PER_STEP
    assert n_pages % pps == 0 and (n_pages * page) % (2 * CMP_BLOCK) == 0 and tq < CMP_BLOCK
    assert tq % V7X_SUBLANES == 0 and page == V7X_LANES
    nc = n_pages * page // CMP_BLOCK
    rt = _rope_tables(_cmp_order_end_pos(nc), KV_WIDTH)
    wb = cache_win_t.shape[3]
    full = lambda shp: pl.BlockSpec(shp, lambda *_: (0,) * len(shp))
    per_seq = lambda shp: pl.BlockSpec((1,) + shp, lambda i, j, pt: (i,) + (0,) * len(shp))
    page_spec = lambda p: pl.BlockSpec((1, 1, kvw, page), lambda i, j, pt: (layer, pt[i, j * pps + p], 0, 0))
    rows = N_HEADS * tq
    grid_spec = pltpu.PrefetchScalarGridSpec(
        num_scalar_prefetch=1,
        grid=(b, n_pages // pps),
        in_specs=[
            per_seq((tq, hq)), per_seq((tq, V7X_LANES)), per_seq((tq, kvw)), per_seq((tq, kvw)),
            pl.BlockSpec((1, 1, kvw, wb), lambda i, j, pt: (layer, i, 0, 0)),
            full(cw["bdk"].shape), full(cw["bdv"].shape), full(cw["pek"].shape), full(cw["pev"].shape),
            full(rt[0].shape), full(rt[1].shape), full(rt[2].shape),
            *[page_spec(p) for p in range(pps)],
            *[page_spec(p) for p in range(pps)],
        ],
        out_specs=per_seq((tq, hq)),
        scratch_shapes=[
            pltpu.VMEM((kvw // V7X_LANES, nc * CMP_PITCH, V7X_LANES), F32),
            pltpu.VMEM((n_pages // pps, rows, pps * page), F32),
            pltpu.VMEM((n_pages // pps, KV_WIDTH, pps * page), BF16),
            pltpu.VMEM((rows, KV_WIDTH), BF16),
            pltpu.VMEM((nc, V7X_LANES), F32),
        ],
    )
    return pl.pallas_call(
        functools.partial(_attn_decode_kernel, page=page, n_pages=n_pages),
        out_shape=jax.ShapeDtypeStruct((b, tq, hq), F32),
        grid_spec=grid_spec,
        compiler_params=_cparams("arbitrary", "arbitrary"),
        name="attn_decode",
    )(page_table, q3, gates3, nsel3, nwin3, cache_win_t, cw["bdk"], cw["bdv"], cw["pek"], cw["pev"], *rt,
      *([cache_cmp_t] * pps), *([cache_sel_t] * pps))


def _out_proj_kernel(o_ref, w_ref, x_ref, mod_ref, xo_ref):
    d = x_ref.shape[1]
    xo_ref[...] = x_ref[...] + mod_ref[0][:, 2 * d:3 * d] * _dot(o_ref[...], w_ref[...])


def _out_proj(o, w_bf16, x, mod3, tpb, tm):
    t, d = x.shape
    k = o.shape[1]
    return pl.pallas_call(
        _out_proj_kernel,
        out_shape=jax.ShapeDtypeStruct((t, d), F32),
        grid=(t // tm,),
        in_specs=[
            pl.BlockSpec((tm, k), lambda i: (i, 0)),
            pl.BlockSpec((k, d), lambda i: (0, 0)),
            pl.BlockSpec((tm, d), lambda i: (i, 0)),
            _mod_spec(mod3, tpb),
        ],
        out_specs=pl.BlockSpec((tm, d), lambda i: (i, 0)),
        compiler_params=_cparams("arbitrary"),
        name="out_proj",
    )(o, w_bf16, x, mod3)


def _ffn_kernel(x_ref, mod_ref, g_ref, wg_ref, wu_ref, wd_ref, o_ref, h_scr):
    d = x_ref.shape[1]
    f = pl.program_id(1)

    @pl.when(f == 0)
    def _():
        mod = mod_ref[0]
        h_scr[...] = _norm_mod(x_ref[...], g_ref[...], mod[:, 4 * d:5 * d], mod[:, 3 * d:4 * d]).astype(BF16)
        o_ref[...] = jnp.zeros(o_ref.shape, F32)

    h = h_scr[...]
    a = _dot(h, wg_ref[0].astype(BF16))
    u = _dot(h, wu_ref[0].astype(BF16))
    act = (a * _sigmoid(a) * u).astype(BF16)
    o_ref[...] += _dot(act, wd_ref[0].astype(BF16))

    @pl.when(f == pl.num_programs(1) - 1)
    def _():
        o_ref[...] = x_ref[...] + mod_ref[0][:, 5 * d:6 * d] * o_ref[...]


def _ffn(x, mod3, tpb, g, w_gu_all, w_down_all, layer, tm, tf):
    t, d = x.shape
    ff = w_down_all.shape[1]
    nf = ff // tf
    return pl.pallas_call(
        _ffn_kernel,
        out_shape=jax.ShapeDtypeStruct((t, d), F32),
        grid=(t // tm, nf),
        in_specs=[
            pl.BlockSpec((tm, d), lambda i, f: (i, 0)),
            _mod_spec(mod3, tpb),
            pl.BlockSpec((1, d), lambda i, f: (0, 0)),
            pl.BlockSpec((1, d, tf), lambda i, f: (layer, 0, f)),
            pl.BlockSpec((1, d, tf), lambda i, f: (layer, 0, nf + f)),
            pl.BlockSpec((1, tf, d), lambda i, f: (layer, f, 0)),
        ],
        out_specs=pl.BlockSpec((tm, d), lambda i, f: (i, 0)),
        scratch_shapes=[pltpu.VMEM((tm, d), BF16)],
        compiler_params=_cparams("arbitrary", "arbitrary"),
        name="ffn",
    )(x, mod3, g, w_gu_all, w_gu_all, w_down_all)


def _conv_in(x_ref, mod_ref, g_ref, win_ref):
    d = x_ref.shape[1]
    mod = mod_ref[0]
    h = _norm_mod(x_ref[...], g_ref[...], mod[:, d:2 * d], mod[:, 0:d]).astype(BF16)
    bg = _dot(h, win_ref[:, 0:d])
    u = _dot(h, win_ref[:, d:2 * d]) * _dot(h, win_ref[:, 2 * d:3 * d])
    return bg, u


def _conv_out(bg, conv, x_ref, mod_ref, wout_ref, xo_ref):
    d = x_ref.shape[1]
    y = _dot((bg * conv).astype(BF16), wout_ref[...])
    xo_ref[...] = x_ref[...] + mod_ref[0][:, 2 * d:3 * d] * y


def _conv_seq_kernel(x_ref, mod_ref, g_ref, win_ref, wc_ref, wout_ref, st_ref, xo_ref, nst_ref, carry_scr):
    tm = x_ref.shape[0]
    bg, u = _conv_in(x_ref, mod_ref, g_ref, win_ref)

    @pl.when(pl.program_id(1) == 0)
    def _():
        carry_scr[6:8, :] = st_ref[0]

    ue = jnp.concatenate([carry_scr[...], u], axis=0)
    conv = wc_ref[0:1, :] * ue[6:6 + tm] + wc_ref[1:2, :] * ue[7:7 + tm] + wc_ref[2:3, :] * u
    carry_scr[...] = u[tm - 8:tm]
    nst_ref[0] = u[tm - 2:tm]
    _conv_out(bg, conv, x_ref, mod_ref, wout_ref, xo_ref)


def _conv_tok_kernel(x_ref, mod_ref, g_ref, win_ref, wc_ref, wout_ref, p1_ref, p2_ref, xo_ref, u_ref, *, t_seq):
    tm = x_ref.shape[0]
    bg, u = _conv_in(x_ref, mod_ref, g_ref, win_ref)
    row = lax.broadcasted_iota(jnp.int32, (tm, 1), 0) % t_seq
    prev1 = jnp.where(row >= 1, pltpu.roll(u, 1, 0), p1_ref[...])
    prev2 = jnp.where(row >= 2, pltpu.roll(u, 2, 0), p2_ref[...])
    conv = wc_ref[0:1, :] * prev2 + wc_ref[1:2, :] * prev1 + wc_ref[2:3, :] * u
    u_ref[...] = u
    _conv_out(bg, conv, x_ref, mod_ref, wout_ref, xo_ref)


def _conv_mixer(x, mod3, tpb, g, w_in, w_conv, w_out, state, b, t_seq, tm):
    t, d = x.shape
    weights = [_full((1, d)), _full(w_in.shape), _full(w_conv.shape), _full(w_out.shape)]
    if state is None:
        state = jnp.zeros((b, 2, d), F32)
    if t_seq >= tm:
        nt = t_seq // tm
        xo, nst = pl.pallas_call(
            _conv_seq_kernel,
            out_shape=(jax.ShapeDtypeStruct((t, d), F32), jax.ShapeDtypeStruct((b, 2, d), F32)),
            grid=(b, nt),
            in_specs=[pl.BlockSpec((tm, d), lambda i, j: (i * nt + j, 0)),
                      pl.BlockSpec((1, 1, mod3.shape[-1]), lambda i, j: (i, 0, 0)),
                      *weights,
                      pl.BlockSpec((1, 2, d), lambda i, j: (i, 0, 0))],
            out_specs=(pl.BlockSpec((tm, d), lambda i, j: (i * nt + j, 0)),
                       pl.BlockSpec((1, 2, d), lambda i, j: (i, 0, 0))),
            scratch_shapes=[pltpu.VMEM((8, d), F32)],
            compiler_params=_cparams("arbitrary", "arbitrary"),
            name="conv_seq",
        )(x, mod3, g, w_in, w_conv, w_out, state)
        return xo, nst
    assert t == tm and t_seq >= 2
    p1 = jnp.zeros((b, t_seq, d), F32).at[:, 0].set(state[:, 1]).reshape(t, d)
    p2 = jnp.zeros((b, t_seq, d), F32).at[:, 0].set(state[:, 0]).at[:, 1].set(state[:, 1]).reshape(t, d)
    xo, u = pl.pallas_call(
        functools.partial(_conv_tok_kernel, t_seq=t_seq),
        out_shape=(jax.ShapeDtypeStruct((t, d), F32), jax.ShapeDtypeStruct((t, d), F32)),
        grid=(1,),
        in_specs=[_full((t, d)), _full(mod3.shape), *weights, _full((t, d)), _full((t, d))],
        out_specs=(_full((t, d)), _full((t, d))),
        compiler_params=_cparams("arbitrary"),
        name="conv_tok",
    )(x, mod3, g, w_in, w_conv, w_out, p1, p2)
    return xo, u.reshape(b, t_seq, d)[:, t_seq - 2:]


MOE_ROW_UNIT = 128
MOE_BOUNDS_ROWS = 16
MOE_ARM_MAX = 512
MOE_ARM_MIN = 128
MOE_GATHER_BLOCKS = 4
MOE_SCATTER_UNITS = 2


def _split_bf16(x):
    hi = x.astype(BF16)
    return hi, (x - hi.astype(F32)).astype(BF16)


def _moe_route_kernel(x_ref, mod_ref, g_ref, wr_ref, br_ref, h_ref, pos_ref, comb_ref, post_ref, bnd_ref, *, blk):
    tm, d = x_ref.shape
    nb = tm // blk
    mod = mod_ref[0]
    h32 = _norm_mod(x_ref[...], g_ref[...], mod[:, 4 * d:5 * d], mod[:, 3 * d:4 * d])
    h_ref[...] = h32.astype(BF16)
    lane = lax.broadcasted_iota(jnp.int32, (1, V7X_LANES), 1)
    h_hi, h_lo = _split_bf16(h32)
    w_hi, w_lo = _split_bf16(wr_ref[...])
    logits = _dot(h_hi, w_hi) + (_dot(h_hi, w_lo) + _dot(h_lo, w_hi)) + br_ref[...]
    logits = jnp.where(lane < N_EXPERTS, logits, NEG_INF)
    m1 = jnp.max(logits, axis=-1, keepdims=True)
    i1 = jnp.min(jnp.where(logits == m1, lane, V7X_LANES), axis=-1, keepdims=True)
    rest = jnp.where(lane == i1, NEG_INF, logits)
    m2 = jnp.max(rest, axis=-1, keepdims=True)
    i2 = jnp.min(jnp.where(rest == m2, lane, V7X_LANES), axis=-1, keepdims=True)
    e2 = jnp.exp(m2 - m1)
    den = 1.0 + e2
    comb_ref[...] = jnp.where(lane == i1, 1.0 / den, jnp.where(lane == i2, e2 / den, 0.0))
    mask = jnp.where(lane == i1, 1.0, jnp.where(lane == i2, 1.0, 0.0))
    r = lax.broadcasted_iota(jnp.int32, (blk, blk), 0)
    c = lax.broadcasted_iota(jnp.int32, (blk, blk), 1)
    lower = jnp.where(c < r, 1.0, 0.0).astype(BF16)
    upper = jnp.where(r < c, 1.0, 0.0).astype(BF16)
    carry = jnp.zeros((1, V7X_LANES), F32)
    bnd_ref[0] = jnp.zeros((MOE_BOUNDS_ROWS, V7X_LANES), F32)
    for k in range(nb):
        mb = mask[k * blk:(k + 1) * blk]
        pos_ref[k * blk:(k + 1) * blk, :] = jnp.where(mb > 0.5, _dot(lower, mb.astype(BF16)) + carry, -1.0)
        carry = carry + jnp.sum(mb, axis=0, keepdims=True)
        bnd_ref[0, k + 1:k + 2, :] = carry
    eye = jnp.where(lax.broadcasted_iota(jnp.int32, (V7X_LANES, V7X_LANES), 0)
                    == lax.broadcasted_iota(jnp.int32, (V7X_LANES, V7X_LANES), 1), 1.0, 0.0).astype(BF16)
    mask_t = _dot_nt(eye, mask.astype(BF16))[0:N_EXPERTS]
    carry_t = jnp.zeros((N_EXPERTS, 1), F32)
    for k in range(nb):
        mb = mask_t[:, k * blk:(k + 1) * blk]
        post_ref[0, k] = jnp.where(mb > 0.5, _dot(mb.astype(BF16), upper) + carry_t, -1.0)
        carry_t = carry_t + jnp.sum(mb, axis=1, keepdims=True)


def _moe_route(x, mod3, tpb, g, w_router, b_router, tm, blk):
    t, d = x.shape
    ne = w_router.shape[1]
    nt = t // tm
    nb = tm // blk
    assert nb + 1 <= MOE_BOUNDS_ROWS
    wr = jnp.pad(w_router, ((0, 0), (0, V7X_LANES - ne)))
    br = jnp.pad(b_router, (0, V7X_LANES - ne)).reshape(1, V7X_LANES)
    tok = lambda w: pl.BlockSpec((tm, w), lambda i: (i, 0))
    return pl.pallas_call(
        functools.partial(_moe_route_kernel, blk=blk),
        out_shape=(
            jax.ShapeDtypeStruct((t, d), BF16),
            jax.ShapeDtypeStruct((t, V7X_LANES), F32),
            jax.ShapeDtypeStruct((t, V7X_LANES), F32),
            jax.ShapeDtypeStruct((nt, nb, N_EXPERTS, blk), F32),
            jax.ShapeDtypeStruct((nt, MOE_BOUNDS_ROWS, V7X_LANES), F32),
        ),
        grid=(nt,),
        in_specs=[tok(d), _mod_spec(mod3, tpb), _full((1, d)), _full((d, V7X_LANES)), _full((1, V7X_LANES))],
        out_specs=(tok(d), tok(V7X_LANES), tok(V7X_LANES),
                   pl.BlockSpec((1, nb, N_EXPERTS, blk), lambda i: (i, 0, 0, 0)),
                   pl.BlockSpec((1, MOE_BOUNDS_ROWS, V7X_LANES), lambda i: (i, 0, 0))),
        compiler_params=_cparams("arbitrary"),
        name="moe_route",
    )(x, mod3, g, wr, br)


def _moe_expert_kernel(bnd_ref, h_ref, pos_ref, comb_ref, post_ref, wg_ref, wu_ref, wd_ref, o_ref,
                       hc_scr, acc_scr, posc_scr, combc_scr, *, blk):
    tm, d = h_ref.shape
    nb = tm // blk
    unit = MOE_ROW_UNIT
    i = pl.program_id(0)
    e = pl.program_id(1)
    f = pl.program_id(2)
    nf = pl.num_programs(2)
    base = (i * N_EXPERTS + e) * MOE_BOUNDS_ROWS
    cnt = bnd_ref[base + nb]
    n_units = (cnt + unit - 1) // unit

    def span(u):
        start = u * unit
        stop = jnp.minimum(start + unit, cnt)
        lo = jnp.int32(0)
        hi = jnp.int32(0)
        for k in range(1, nb):
            bk = bnd_ref[base + k]
            lo = lo + (bk <= start).astype(jnp.int32)
            hi = hi + (bk < stop).astype(jnp.int32)
        return lo, hi + 1

    @pl.when((i == 0) & (e == 0) & (f == 0))
    def _():
        acc_scr[...] = jnp.zeros(acc_scr.shape, F32)

    @pl.when((e == 0) & (f == 0))
    def _():
        o_ref[...] = jnp.zeros(o_ref.shape, F32)

    gw = min(MOE_GATHER_BLOCKS, nb)

    @pl.when(f == 0)
    def _():
        def per_unit(u, _):
            lo, hi = span(u)
            r0 = pl.multiple_of(u * unit, unit)
            want = (r0 + lax.broadcasted_iota(jnp.int32, (unit, 1), 0)).astype(F32)
            lo_w = jnp.minimum(lo, nb - gw)
            pos_t = jnp.concatenate([post_ref[0, lo_w + k, pl.ds(e, 1), :] for k in range(gw)], axis=1)
            onehot = jnp.where(pos_t == want, 1.0, 0.0).astype(BF16)
            t0 = pl.multiple_of(lo_w * blk, blk)
            hc_scr[pl.ds(r0, unit), :] = _dot(onehot, h_ref[pl.ds(t0, gw * blk), :]).astype(BF16)

            def per_blk(k, _):
                t1 = pl.multiple_of(k * blk, blk)
                oh = jnp.where(post_ref[0, k, pl.ds(e, 1), :] == want, 1.0, 0.0).astype(BF16)
                hc_scr[pl.ds(r0, unit), :] += _dot(oh, h_ref[pl.ds(t1, blk), :]).astype(BF16)
                return 0

            lax.fori_loop(lo_w + gw, hi, per_blk, 0)
            acc_scr[pl.ds(r0, unit), :] = jnp.zeros((unit, d), F32)
            return 0

        lax.fori_loop(0, n_units, per_unit, 0)

    def ffn_rows(r0, m):
        hc = hc_scr[pl.ds(r0, m), :]
        a = _dot(hc, wg_ref[0, 0])
        act = (a * _sigmoid(a) * _dot(hc, wu_ref[0, 0])).astype(BF16)
        acc_scr[pl.ds(r0, m), :] += _dot(act, wd_ref[0, 0])

    big = min(MOE_ARM_MAX, tm)
    n_min = (cnt + MOE_ARM_MIN - 1) // MOE_ARM_MIN
    per_big = big // MOE_ARM_MIN
    n_big = n_min // per_big
    rem = n_min - per_big * n_big

    def big_arm(k, _):
        ffn_rows(pl.multiple_of(k * big, big), big)
        return 0

    lax.fori_loop(0, n_big, big_arm, 0)
    r_arm = n_big * big
    size = big // 2
    while size >= MOE_ARM_MIN:
        bit = size // MOE_ARM_MIN
        if size <= tm:
            taken = (rem // bit) % 2 == 1

            @pl.when(taken)
            def _(r_arm=r_arm, size=size):
                ffn_rows(pl.multiple_of(r_arm, size), size)

            r_arm = r_arm + jnp.where(taken, size, 0)
        size //= 2

    @pl.when(f == nf - 1)
    def _():
        lane = lax.broadcasted_iota(jnp.int32, (1, V7X_LANES), 1)
        posc_scr[...] = jnp.sum(jnp.where(lane == e, pos_ref[...], 0.0), axis=-1, keepdims=True)
        combc_scr[...] = jnp.sum(jnp.where(lane == e, comb_ref[...], 0.0), axis=-1, keepdims=True)

        sw = min(MOE_SCATTER_UNITS * unit, tm)

        def per_blk(k, _):
            start = bnd_ref[base + k]
            stop = bnd_ref[base + k + 1]
            t0 = pl.multiple_of(k * blk, blk)
            r0 = pl.multiple_of(jnp.minimum((start // unit) * unit, tm - sw), unit)
            pc = posc_scr[pl.ds(t0, blk), :]
            cc = combc_scr[pl.ds(t0, blk), :]
            want = (r0 + lax.broadcasted_iota(jnp.int32, (1, sw), 1)).astype(F32)
            onehot = jnp.where(pc == want, 1.0, 0.0).astype(BF16)
            o_ref[pl.ds(t0, blk), :] += cc * _dot(onehot, acc_scr[pl.ds(r0, sw), :].astype(BF16))

            def per_unit(u, _):
                r1 = pl.multiple_of(u * unit, unit)
                want1 = (r1 + lax.broadcasted_iota(jnp.int32, (1, unit), 1)).astype(F32)
                oh = jnp.where(pc == want1, 1.0, 0.0).astype(BF16)
                o_ref[pl.ds(t0, blk), :] += cc * _dot(oh, acc_scr[pl.ds(r1, unit), :].astype(BF16))
                return 0

            lax.fori_loop((r0 + sw) // unit, (stop + unit - 1) // unit, per_unit, 0)
            return 0

        lax.fori_loop(0, nb, per_blk, 0)


def _moe_experts(h, pos, comb, post, bounds, w_gu_all, w_down_all, layer, tm, tf, blk):
    t, d = h.shape
    ne, ff = w_down_all.shape[1], w_down_all.shape[2]
    nf = ff // tf
    nt = t // tm
    nb = tm // blk
    bnd = jnp.transpose(bounds[:, :, :ne], (0, 2, 1)).astype(jnp.int32).reshape(nt * ne * MOE_BOUNDS_ROWS)
    once = pl.Buffered(1)
    tok = lambda w: pl.BlockSpec((tm, w), lambda i, e, f, b_: (i, 0), pipeline_mode=once)
    grid_spec = pltpu.PrefetchScalarGridSpec(
        num_scalar_prefetch=1,
        grid=(nt, ne, nf),
        in_specs=[
            tok(d), tok(V7X_LANES), tok(V7X_LANES),
            pl.BlockSpec((1, nb, ne, blk), lambda i, e, f, b_: (i, 0, 0, 0)),
            pl.BlockSpec((1, 1, d, tf), lambda i, e, f, b_: (layer, e, 0, f)),
            pl.BlockSpec((1, 1, d, tf), lambda i, e, f, b_: (layer, e, 0, nf + f)),
            pl.BlockSpec((1, 1, tf, d), lambda i, e, f, b_: (layer, e, f, 0)),
        ],
        out_specs=pl.BlockSpec((tm, d), lambda i, e, f, b_: (i, 0), pipeline_mode=once),
        scratch_shapes=[
            pltpu.VMEM((tm, d), BF16),
            pltpu.VMEM((tm, d), F32),
            pltpu.VMEM((tm, 1), F32),
            pltpu.VMEM((tm, 1), F32),
        ],
    )
    return pl.pallas_call(
        functools.partial(_moe_expert_kernel, blk=blk),
        out_shape=jax.ShapeDtypeStruct((t, d), F32),
        grid_spec=grid_spec,
        compiler_params=_cparams("arbitrary", "arbitrary", "arbitrary"),
        name="moe_experts",
    )(bnd, h, pos, comb, post, w_gu_all, w_gu_all, w_down_all)


def _residual_kernel(x_ref, f_ref, mod_ref, g_ref, o_ref, *, final_norm):
    d = x_ref.shape[1]
    x = x_ref[...] + mod_ref[0][:, 5 * d:6 * d] * f_ref[...]
    if final_norm:
        x = (x * lax.rsqrt(jnp.mean(x * x, axis=-1, keepdims=True) + EPS)) * g_ref[...]
    o_ref[...] = x


def _residual(x, f, mod3, tpb, g_final, tm, final_norm):
    t, d = x.shape
    tok = pl.BlockSpec((tm, d), lambda i: (i, 0))
    return pl.pallas_call(
        functools.partial(_residual_kernel, final_norm=final_norm),
        out_shape=jax.ShapeDtypeStruct((t, d), F32),
        grid=(t // tm,),
        in_specs=[tok, tok, _mod_spec(mod3, tpb), _full((1, d))],
        out_specs=tok,
        compiler_params=_cparams("arbitrary"),
        name="residual",
    )(x, f, mod3, g_final)


def _moe(x, mod3_of, g, w_router, b_router, w_gu_all, w_down_all, layer, g_final, tm, final_norm):
    blk = min(256, tm)
    h, pos, comb, post, bounds = _moe_route(x, *mod3_of(tm), g, w_router, b_router, tm, blk)
    f = _moe_experts(h, pos, comb, post, bounds, w_gu_all, w_down_all, layer, tm, 512, blk)
    tm_res = min(1024, tm)
    return _residual(x, f, *mod3_of(tm_res), g_final, tm_res, final_norm)


def _final_norm_kernel(x_ref, g_ref, o_ref):
    x = x_ref[...]
    o_ref[...] = (x * lax.rsqrt(jnp.mean(x * x, axis=-1, keepdims=True) + EPS)) * g_ref[...]


def _final_norm(x, g, tm):
    t, d = x.shape
    return pl.pallas_call(
        _final_norm_kernel,
        out_shape=jax.ShapeDtypeStruct((t, d), F32),
        grid=(t // tm,),
        in_specs=[pl.BlockSpec((tm, d), lambda i: (i, 0)), pl.BlockSpec((1, d), lambda i: (0, 0))],
        out_specs=pl.BlockSpec((tm, d), lambda i: (i, 0)),
        compiler_params=_cparams("arbitrary"),
        name="final_norm",
    )(x, g)


def _nsa_layer_weights(w_in, w_out, w_ck, pe_ck, w_cv, pe_cv):
    n_main = N_HEADS * HEAD_DIM + 6 * KV_WIDTH
    w_gate = jnp.pad(w_in[:, n_main:], ((0, 0), (0, V7X_LANES - N_HEADS * N_BRANCH)))
    return {"w_main": w_in[:, :n_main].astype(BF16), "w_gate": w_gate.astype(BF16),
            "w_out": w_out.astype(BF16), "cw": _compress_weights(w_ck, pe_ck, w_cv, pe_cv)}


def _rows_from_feature_major(xt):
    b, _, s = xt.shape
    return jnp.transpose(xt.reshape(b, 2, KV_HEADS, HEAD_DIM, s), (0, 4, 1, 2, 3))


def _feature_major_from_rows(x, lead):
    n = len(lead)
    perm = tuple(range(n)) + (n + 1, n + 2, n + 3, n)
    xt = jnp.transpose(x, perm)
    return xt.reshape(lead + (2 * KV_WIDTH, x.shape[n]))


def _nsa_prompt_layer(x, mod, g, lw, b, s_len):
    tm = min(512, s_len)
    mod3 = mod.reshape(b, 1, mod.shape[-1])
    rope_t = _rope_tables(jnp.arange(s_len), V7X_LANES)
    q, gates, cmp_rows, cmpt, selt, wint = _nsa_proj(x, mod3, s_len // tm, g, lw["w_main"], lw["w_gate"], rope_t,
                                                     s_len, tm, True)
    ck, cv = _compress(cmp_rows.reshape(b, s_len, 2 * KV_WIDTH), lw["cw"])
    o = _attn_prompt(q, gates, ck, cv, selt, wint, tq=min(256, s_len))
    x = _out_proj(o, lw["w_out"], x, mod3, s_len // tm, tm)
    return x, cmpt, selt, wint


def _nsa_decode_layer(x, mod3, g, lw, b, t_seq, caches, layer):
    t = x.shape[0]
    page_table = caches["page_table"]
    past_len = page_table.shape[1] * caches["cmp"].shape[3]
    rope_t = _rope_tables(past_len + jnp.arange(t) % t_seq, V7X_LANES)
    q, gates, cmp_rows, sel_rows, win_rows = _nsa_proj(x, mod3, 1, g, lw["w_main"], lw["w_gate"], rope_t, t, t, False)
    pad3 = lambda a: jnp.pad(a.reshape(b, t_seq, a.shape[-1]), ((0, 0), (0, DECODE_TOK_PAD - t_seq), (0, 0)))
    o = _attn_decode(page_table, pad3(q), pad3(gates), pad3(sel_rows), pad3(win_rows),
                     caches["cmp"], caches["sel"], caches["win"], layer, lw["cw"])
    o = o[:, :t_seq].reshape(t, o.shape[-1]).astype(BF16)
    x = _out_proj(o, lw["w_out"], x, mod3, 1, t)
    return x, cmp_rows, sel_rows, win_rows


def _trunk(x3, mod, caches, p):
    b, t_seq, d = x3.shape
    t = b * t_seq
    kvw = 2 * KV_WIDTH
    prompt = caches is None
    x = x3.reshape(t, d)
    new_cmp, new_sel, new_win, new_conv = [], [], [], []
    depth = mod.shape[0]
    g_final = p["g_norm_final"].reshape(1, d)
    finished = False
    for i in range(depth):
        j = i // 2
        if prompt:
            mod_i = mod[i].reshape(b, 1, 6 * d)
            mod3_of = lambda tm, mod_i=mod_i: (mod_i, t_seq // tm)
            tm_small, tm_big, tm_moe = min(512, t_seq), min(1024, t_seq), min(2048, t_seq)
        else:
            mod_i = jnp.repeat(mod[i], t_seq, axis=0).reshape(1, t, 6 * d)
            mod3_of = lambda tm, mod_i=mod_i: (mod_i, 1)
            tm_small = tm_big = tm_moe = t
        g_mix = p["g_norm_mix"][i].reshape(1, d)
        g_ffn = p["g_norm_ffn"][i].reshape(1, d)
        if i % 2 == 0:
            lw = _nsa_layer_weights(p["w_nsa_in"][j], p["w_nsa_out"][j], p["w_cmp_k"][j], p["pe_cmp_k"][j],
                                    p["w_cmp_v"][j], p["pe_cmp_v"][j])
            if prompt:
                x, cmpt, selt, wint = _nsa_prompt_layer(x, mod[i], g_mix, lw, b, t_seq)
                new_cmp.append(_rows_from_feature_major(cmpt))
                new_sel.append(_rows_from_feature_major(selt))
                new_win.append(_rows_from_feature_major(wint[:, :, t_seq - min(WINDOW, t_seq):]))
            else:
                x, c_rows, s_rows, w_rows = _nsa_decode_layer(x, mod_i, g_mix, lw, b, t_seq, caches, j)
                rows5 = lambda r: r.reshape(b, t_seq, 2, KV_HEADS, HEAD_DIM)
                new_cmp.append(rows5(c_rows))
                new_sel.append(rows5(s_rows))
                new_win.append(jnp.concatenate([caches["win_rows"][j], rows5(w_rows)], axis=1)[:, t_seq:])
            x = _ffn(x, *mod3_of(tm_big), g_ffn, p["w_ff_gu"], p["w_ff_down"], j, tm_big, 512)
        else:
            state = None if prompt else caches["conv"][j]
            x, st = _conv_mixer(x, *mod3_of(tm_small), g_mix, p["w_conv_in"][j].astype(BF16), p["w_conv"][j],
                                p["w_conv_out"][j].astype(BF16), state, b, t_seq, tm_small)
            new_conv.append(st)
            finished = i == depth - 1
            x = _moe(x, mod3_of, g_ffn, p["w_router"][j], p["b_router"][j], p["w_moe_gu"], p["w_moe_down"], j,
                     g_final, tm_moe, finished)
    y = x if finished else _final_norm(x, g_final, tm_big)
    return (y.reshape(b, t_seq, d), jnp.stack(new_cmp), jnp.stack(new_sel), jnp.stack(new_win), jnp.stack(new_conv))


def kernel(x_prompt, x_sample, cache_cmp_kv, cache_sel_kv, cache_win_kv, state_conv, page_table, c_prompt, c_sample,
           w_ada, b_ada, g_norm_mix, g_norm_ffn, g_norm_final, w_nsa_in, w_nsa_out, w_cmp_k, pe_cmp_k, w_cmp_v,
           pe_cmp_v, w_conv_in, w_conv, w_conv_out, w_ff_gu, w_ff_down, w_router, b_router, w_moe_gu, w_moe_down):
    p = {"g_norm_mix": g_norm_mix, "g_norm_ffn": g_norm_ffn, "g_norm_final": g_norm_final,
         "w_nsa_in": w_nsa_in, "w_nsa_out": w_nsa_out, "w_cmp_k": w_cmp_k, "pe_cmp_k": pe_cmp_k, "w_cmp_v": w_cmp_v,
         "pe_cmp_v": pe_cmp_v, "w_conv_in": w_conv_in, "w_conv": w_conv, "w_conv_out": w_conv_out,
         "w_ff_gu": w_ff_gu, "w_ff_down": w_ff_down, "w_router": w_router, "b_router": b_router,
         "w_moe_gu": w_moe_gu.astype(BF16), "w_moe_down": w_moe_down.astype(BF16)}
    nb = x_prompt.shape[0]
    mod = _ada(jnp.concatenate([c_prompt, c_sample], axis=0), w_ada, b_ada)
    y_p, cmp_p, sel_p, win_p, conv_p = _trunk(x_prompt, mod[:, :nb], None, p)
    caches = {"cmp": _feature_major_from_rows(cache_cmp_kv, cache_cmp_kv.shape[:2]),
              "sel": _feature_major_from_rows(cache_sel_kv, cache_sel_kv.shape[:2]),
              "win": _feature_major_from_rows(cache_win_kv, cache_win_kv.shape[:2]),
              "win_rows": cache_win_kv, "conv": state_conv, "page_table": page_table}
    y_s, cmp_s, sel_s, win_s, conv_s = _trunk(x_sample, mod[:, nb:], caches, p)
    return (y_p, y_s, cmp_p, cmp_s, sel_p, sel_s, win_p, win_s, conv_p, conv_s)
```

```python
import functools

import jax
import jax.numpy as jnp
from jax import lax
from jax.experimental import pallas as pl
from jax.experimental.pallas import tpu as pltpu

F32 = jnp.float32
BF16 = jnp.bfloat16

N_HEADS = 16
HEAD_DIM = 64
KV_HEADS = 4
HEADS_PER_KV = N_HEADS // KV_HEADS
KV_WIDTH = KV_HEADS * HEAD_DIM
ROT_DIM = HEAD_DIM // 4
ROPE_THETA = 500000.0
CMP_BLOCK = 32
SEL_BLOCK = 64
TOP_N = 8
WINDOW = 512
N_BRANCH = 3
FORCE_BONUS = float(HEADS_PER_KV + 1)
N_EXPERTS = 8
EPS = 1e-6
NEG_INF = float("-inf")

V7X_LANES = 128
V7X_SUBLANES = 8
V7X_VMEM_LIMIT_BYTES = 58 * 1024 * 1024


def _cparams(*sem):
    return pltpu.CompilerParams(dimension_semantics=sem, vmem_limit_bytes=V7X_VMEM_LIMIT_BYTES)


def _dot(a, b):
    return jnp.dot(a, b, preferred_element_type=F32)


def _dot_nt(a, b):
    return lax.dot_general(a, b, (((1,), (1,)), ((), ())), preferred_element_type=F32)


def _norm_mod(x, g, sc, sh):
    y = x * lax.rsqrt(jnp.mean(x * x, axis=-1, keepdims=True) + EPS)
    return (y * g) * (1.0 + sc) + sh


def _sigmoid(x):
    return 1.0 / (1.0 + jnp.exp(-x))


def _mod_spec(mod3, tiles_per_block):
    _, r, w = mod3.shape
    return pl.BlockSpec((1, r, w), lambda i, *_: (i // tiles_per_block, 0, 0))


def _full(shape):
    return pl.BlockSpec(shape, lambda *_: (0,) * len(shape))


def _ada_kernel(c_ref, w_ref, b_ref, o_ref):
    c = c_ref[...]
    cs = (c * _sigmoid(c)).astype(BF16)
    o_ref[0] = _dot(cs, w_ref[0].astype(BF16)) + b_ref[0]


def _ada(c, w_ada, b_ada):
    depth, d, n = w_ada.shape
    nb = c.shape[0]
    tn = 1536
    return pl.pallas_call(
        _ada_kernel,
        out_shape=jax.ShapeDtypeStruct((depth, nb, n), F32),
        grid=(depth, n // tn),
        in_specs=[
            pl.BlockSpec((nb, d), lambda i, j: (0, 0)),
            pl.BlockSpec((1, d, tn), lambda i, j: (i, 0, j)),
            pl.BlockSpec((1, 1, tn), lambda i, j: (i, 0, j)),
        ],
        out_specs=pl.BlockSpec((1, nb, tn), lambda i, j: (i, 0, j)),
        compiler_params=_cparams("arbitrary", "arbitrary"),
        name="ada",
    )(c, w_ada, b_ada.reshape(depth, 1, n))


def _rope_tables(pos, width):
    half = ROT_DIM // 2
    inv = 1.0 / (ROPE_THETA ** (jnp.arange(half, dtype=F32) * (2.0 / ROT_DIM)))
    ang = pos.astype(F32)[:, None] * inv[None, :]
    cos, sin = jnp.cos(ang), jnp.sin(ang)
    n = pos.shape[0]
    z8 = jnp.zeros((n, half), F32)
    rest0 = jnp.zeros((n, HEAD_DIM - ROT_DIM), F32)
    c = jnp.concatenate([cos, cos, rest0 + 1.0], axis=1)
    a = jnp.concatenate([-sin, z8, rest0], axis=1)
    b = jnp.concatenate([z8, sin, rest0], axis=1)
    rep = width // HEAD_DIM
    return tuple(jnp.tile(t, (1, rep)) for t in (c, a, b))


def _rope(v, c, a, b):
    w = v.shape[-1]
    half = ROT_DIM // 2
    return v * c + pltpu.roll(v, w - half, 1) * a + pltpu.roll(v, half, 1) * b


def _widen(t, width):
    return jnp.concatenate([t] * (width // t.shape[-1]), axis=1)


def _nsa_proj_kernel(x_ref, mod_ref, g_ref, w_ref, wg_ref, rc_ref, ra_ref, rb_ref, q_ref, gate_ref, *kv_refs,
                     feature_major):
    d = x_ref.shape[1]
    mod = mod_ref[0]
    h = _norm_mod(x_ref[...], g_ref[...], mod[:, d:2 * d], mod[:, 0:d]).astype(BF16)
    kvw = 2 * KV_WIDTH
    c1, a1, b1 = rc_ref[...], ra_ref[...], rb_ref[...]
    c2, a2, b2 = (_widen(t, kvw) for t in (c1, a1, b1))
    scale = HEAD_DIM ** -0.5
    hq = N_HEADS * HEAD_DIM
    for j in range(hq // kvw):
        p = _dot(h, w_ref[:, j * kvw:(j + 1) * kvw])
        q_ref[:, j * kvw:(j + 1) * kvw] = (_rope(p, c2, a2, b2) * scale).astype(BF16)
    gate_ref[...] = _sigmoid(_dot(h, wg_ref[...]))
    one = jnp.ones_like(c1)
    zero = jnp.zeros_like(c1)
    ck = jnp.concatenate([_widen(c1, KV_WIDTH), _widen(one, KV_WIDTH)], axis=1)
    ak = jnp.concatenate([_widen(a1, KV_WIDTH), _widen(zero, KV_WIDTH)], axis=1)
    bk = jnp.concatenate([_widen(b1, KV_WIDTH), _widen(zero, KV_WIDTH)], axis=1)
    cmp_rows = _dot(h, w_ref[:, hq:hq + kvw])
    sel_rows = _rope(_dot(h, w_ref[:, hq + kvw:hq + 2 * kvw]), ck, ak, bk)
    win_rows = _rope(_dot(h, w_ref[:, hq + 2 * kvw:hq + 3 * kvw]), ck, ak, bk)
    if feature_major:
        cmp_ref, cmpt_ref, selt_ref, wint_ref = kv_refs
        cmp_ref[...] = cmp_rows
        cmpt_ref[0] = cmp_rows.T
        selt_ref[0] = sel_rows.T
        wint_ref[0] = win_rows.T
    else:
        cmp_ref, sel_ref, win_ref = kv_refs
        cmp_ref[...] = cmp_rows
        sel_ref[...] = sel_rows
        win_ref[...] = win_rows


def _nsa_proj(x, mod3, tpb, g, w_main, w_gate, rope_t, rows_per_seq, tm, feature_major):
    t, d = x.shape
    nw = w_main.shape[1]
    kvw = 2 * KV_WIDTH
    hq = N_HEADS * HEAD_DIM
    nrt = rows_per_seq // tm
    rspec = pl.BlockSpec((tm, V7X_LANES), lambda i: (i % nrt, 0))
    tok = lambda w: pl.BlockSpec((tm, w), lambda i: (i, 0))
    out_shape = [jax.ShapeDtypeStruct((t, hq), BF16), jax.ShapeDtypeStruct((t, V7X_LANES), F32)]
    out_specs = [tok(hq), tok(V7X_LANES)]
    if feature_major:
        b = t // rows_per_seq
        fm = pl.BlockSpec((1, kvw, tm), lambda i: (i // nrt, 0, i % nrt))
        out_shape += [jax.ShapeDtypeStruct((t, kvw), F32)] + [jax.ShapeDtypeStruct((b, kvw, rows_per_seq), F32)] * 3
        out_specs += [tok(kvw), fm, fm, fm]
    else:
        out_shape += [jax.ShapeDtypeStruct((t, kvw), F32)] * 3
        out_specs += [tok(kvw)] * 3
    return pl.pallas_call(
        functools.partial(_nsa_proj_kernel, feature_major=feature_major),
        out_shape=tuple(out_shape),
        grid=(t // tm,),
        in_specs=[tok(d), _mod_spec(mod3, tpb), _full((1, d)), _full((d, nw)), _full((d, V7X_LANES)),
                  rspec, rspec, rspec],
        out_specs=tuple(out_specs),
        compiler_params=_cparams("arbitrary"),
        name="nsa_proj",
    )(x, mod3, g, w_main, w_gate, *rope_t)


def _compress_rows(load_rows, bdk_ref, bdv_ref, pek_ref, pev_ref, nc, perm_scr):
    half = nc // 2
    n_chunks = 2 * KV_WIDTH // V7X_LANES
    outs = []
    for c in range(n_chunks):
        pe_ref, w_ref = (pek_ref, bdk_ref) if c < n_chunks // 2 else (pev_ref, bdv_ref)
        cols = [(load_rows(c, l) + pe_ref[l:l + 1, :]).astype(BF16) for l in range(CMP_BLOCK)]
        perm_scr[...] = _dot(jnp.concatenate(cols, axis=1), w_ref[...])
        outs.append(jnp.concatenate([perm_scr[pl.ds(0, half, stride=2), :], perm_scr[pl.ds(1, half, stride=2), :]],
                                    axis=0))
    return (jnp.concatenate(outs[0:n_chunks // 2], axis=1), jnp.concatenate(outs[n_chunks // 2:], axis=1))


def _compress_kernel(r0_ref, r1_ref, r2_ref, r3_ref, bdk_ref, bdv_ref, pek_ref, pev_ref, rc_ref, ra_ref, rb_ref,
                     ck_ref, cv_ref, perm_scr):
    nc = ck_ref.shape[1]

    chunk_refs = (r0_ref, r1_ref, r2_ref, r3_ref)

    def load_rows(c, l):
        return chunk_refs[c][0, pl.ds(l, nc, stride=CMP_BLOCK), :]

    acck, accv = _compress_rows(load_rows, bdk_ref, bdv_ref, pek_ref, pev_ref, nc, perm_scr)
    ck_ref[0] = _rope(acck, rc_ref[...], ra_ref[...], rb_ref[...]).astype(BF16)
    cv_ref[0] = accv.astype(BF16)


def _cmp_order_end_pos(nc):
    n = jnp.arange(nc)
    half = nc // 2
    blk = jnp.where(n < half, 2 * n, 2 * (n - half) + 1)
    return blk * CMP_BLOCK + (CMP_BLOCK - 1)


def _cmp_end_row(nc):
    half = nc // 2
    col = lax.broadcasted_iota(jnp.int32, (1, nc), 1)
    return jnp.where(col < half, 2 * CMP_BLOCK * col + (CMP_BLOCK - 1),
                     2 * CMP_BLOCK * (col - half) + (2 * CMP_BLOCK - 1))


def _compress(rows3, cw):
    b, tk, w = rows3.shape
    nc = tk // CMP_BLOCK
    assert nc % 2 == 0 and nc * CMP_BLOCK == tk
    rt = _rope_tables(_cmp_order_end_pos(nc), KV_WIDTH)
    return pl.pallas_call(
        _compress_kernel,
        out_shape=(jax.ShapeDtypeStruct((b, nc, KV_WIDTH), BF16),) * 2,
        grid=(b,),
        in_specs=[
            *[pl.BlockSpec((1, tk, V7X_LANES), functools.partial(lambda i, c: (i, 0, c), c=c))
              for c in range(w // V7X_LANES)],
            _full(cw["bdk"].shape), _full(cw["bdv"].shape), _full(cw["pek"].shape), _full(cw["pev"].shape),
            _full(rt[0].shape), _full(rt[1].shape), _full(rt[2].shape),
        ],
        out_specs=(pl.BlockSpec((1, nc, KV_WIDTH), lambda i: (i, 0, 0)),) * 2,
        scratch_shapes=[pltpu.VMEM((nc, V7X_LANES), F32)],
        compiler_params=_cparams("arbitrary"),
        name="compress",
    )(rows3, rows3, rows3, rows3, cw["bdk"], cw["bdv"], cw["pek"], cw["pev"], *rt)


def _compress_weights(w_ck, pe_ck, w_cv, pe_cv):
    pair = V7X_LANES // HEAD_DIM
    eye = jnp.eye(pair, dtype=F32)

    def bd(w):
        return jnp.einsum("gh,lde->lgdhe", eye, w).reshape(CMP_BLOCK * V7X_LANES, V7X_LANES).astype(BF16)

    return {"bdk": bd(w_ck), "bdv": bd(w_cv),
            "pek": jnp.tile(pe_ck, (1, pair)), "pev": jnp.tile(pe_cv, (1, pair))}


def _softmax_rows(s):
    m = jnp.max(s, axis=-1, keepdims=True)
    m = jnp.where(m == NEG_INF, 0.0, m)
    e = jnp.exp(s - m)
    return e / jnp.maximum(jnp.sum(e, axis=-1, keepdims=True), 1e-30)


def _masked_softmax(s, mask):
    return _softmax_rows(jnp.where(mask, s, NEG_INF))


def _select_blocks(imp, t_col, ns):
    blk = lax.broadcasted_iota(jnp.int32, (1, ns), 1)
    cur = t_col // SEL_BLOCK
    forced = jnp.where(blk == 0, 1.0, jnp.where(blk == cur, 1.0, 0.0))
    score = jnp.where(blk <= cur, imp + FORCE_BONUS * forced, NEG_INF)
    rank = jnp.zeros(score.shape, F32)
    for i in range(ns):
        ci = score[:, i:i + 1]
        later = jnp.where(blk > i, 1.0, 0.0)
        rank = rank + jnp.where(ci > score, 1.0, jnp.where(ci == score, later, 0.0))
    return jnp.where(rank < float(min(TOP_N, ns)), 1.0, 0.0)


def _select_blocks_t(imp_t, t_row, ns):
    blk = lax.broadcasted_iota(jnp.int32, (ns, 1), 0)
    cur = t_row // SEL_BLOCK
    forced = jnp.where(blk == 0, 1.0, jnp.where(blk == cur, 1.0, 0.0))
    score = jnp.where(blk <= cur, imp_t + FORCE_BONUS * forced, NEG_INF)
    rank = jnp.zeros(score.shape, F32)
    for i in range(ns):
        ci = score[i:i + 1, :]
        later = jnp.where(blk > i, 1.0, 0.0)
        rank = rank + jnp.where(ci > score, 1.0, jnp.where(ci == score, later, 0.0))
    return jnp.where(rank < float(min(TOP_N, ns)), 1.0, 0.0)


def _stack_heads(x, g):
    base = g * HEADS_PER_KV * HEAD_DIM
    return jnp.concatenate([x[:, base + r * HEAD_DIM: base + (r + 1) * HEAD_DIM] for r in range(HEADS_PER_KV)],
                           axis=0)


def _gate_cols(gates, g, branch):
    cols = [(g * HEADS_PER_KV + r) * N_BRANCH + branch for r in range(HEADS_PER_KV)]
    return jnp.concatenate([gates[:, c:c + 1] for c in cols], axis=0)


def _sum_rows(n):
    row = lax.broadcasted_iota(jnp.int32, (HEAD_DIM, n), 0)
    return jnp.where(row == 0, 1.0, 0.0).astype(BF16)


def _pad_to(x, rows, cols):
    r, c = x.shape
    if c < cols:
        x = jnp.concatenate([x, jnp.zeros((r, cols - c), x.dtype)], axis=1)
    if r < rows:
        x = jnp.concatenate([x, jnp.zeros((rows - r, x.shape[1]), x.dtype)], axis=0)
    return x


WIN_CHUNK = V7X_LANES
MASK_BIG = 2.0 ** 100


def _attn_prompt_kernel(q_ref, gate_ref, ck_ref, cv_ref, selt_ref, wint_ref, o_ref,
                        ks_scr, vs_scr, kw_scr, vw_scr, *, tk):
    tq = q_ref.shape[0]
    s_len = selt_ref.shape[2]
    nc = ck_ref.shape[1]
    half = nc // 2
    ns = s_len // SEL_BLOCK
    qi = pl.program_id(1)
    k_pad = ks_scr.shape[2] - HEAD_DIM - ns

    @pl.when(qi == 0)
    def _():
        for c in range(s_len // tk):
            kpos = c * tk + lax.broadcasted_iota(jnp.int32, (1, tk), 1)
            brow = lax.broadcasted_iota(jnp.int32, (ns + k_pad, 1), 0)
            expand = jnp.where(kpos // SEL_BLOCK == brow, 1.0, 0.0).astype(BF16)
            for g in range(KV_HEADS):
                k_g = selt_ref[0, g * HEAD_DIM:(g + 1) * HEAD_DIM, c * tk:(c + 1) * tk].astype(BF16)
                ks_scr[c, g] = jnp.concatenate([k_g, expand], axis=0)
                v_g = selt_ref[0, KV_WIDTH + g * HEAD_DIM:KV_WIDTH + (g + 1) * HEAD_DIM, c * tk:(c + 1) * tk]
                vs_scr[c, g] = jnp.concatenate([v_g.astype(BF16), _sum_rows(tk)], axis=0)
        for c in range(s_len // WIN_CHUNK):
            kw_scr[c] = wint_ref[0, 0:KV_WIDTH, c * WIN_CHUNK:(c + 1) * WIN_CHUNK].astype(BF16)
            vw_scr[c] = wint_ref[0, KV_WIDTH:2 * KV_WIDTH, c * WIN_CHUNK:(c + 1) * WIN_CHUNK].astype(BF16)

    q0 = qi * tq
    t_tile = q0 + lax.broadcasted_iota(jnp.int32, (tq, 1), 0)
    t_row = q0 + lax.broadcasted_iota(jnp.int32, (1, tq), 1)
    t_col4 = jnp.concatenate([t_tile] * HEADS_PER_KV, axis=0)
    q_all = q_ref[...]
    gates = gate_ref[...]
    c_end = _cmp_end_row(nc)
    last = q0 // tk
    kpos_last = last * tk + lax.broadcasted_iota(jnp.int32, (1, tk), 1)
    causal = jnp.where(kpos_last <= t_tile, 0.0, -MASK_BIG)
    causal4 = jnp.concatenate([causal] * HEADS_PER_KV, axis=0)

    wlen = min(WINDOW + tq, s_len)
    n_wc = wlen // WIN_CHUNK
    w0 = jnp.clip(q0 + tq - wlen, 0, s_len - wlen)
    wc0 = w0 // WIN_CHUNK
    wpos = w0 + lax.broadcasted_iota(jnp.int32, (1, wlen), 1)
    wbias = jnp.where(wpos <= t_tile, jnp.where(wpos >= t_tile - WINDOW, 0.0, NEG_INF), NEG_INF)
    wbias4 = jnp.concatenate([wbias] * HEADS_PER_KV, axis=0)

    for g in range(KV_HEADS):
        rows_g = slice(g * HEAD_DIM, (g + 1) * HEAD_DIM)
        q4 = _stack_heads(q_all, g)
        p_c = _masked_softmax(_dot_nt(q4, ck_ref[0, :, rows_g]), c_end <= t_col4)
        o_c = _dot(p_c.astype(BF16), cv_ref[0, :, rows_g])
        psum = p_c[0:tq]
        for r in range(1, HEADS_PER_KV):
            psum = psum + p_c[r * tq:(r + 1) * tq]
        psum_t = _pad_to(psum, tq, V7X_LANES).T
        imp_t = psum_t[0:half] + psum_t[half:nc]
        sel_t = _select_blocks_t(imp_t, t_row, ns)
        sel = _pad_to(sel_t, V7X_LANES, tq).T[:, 0:ns + k_pad]
        unchosen = ((sel - 1.0) * MASK_BIG).astype(BF16)
        q_ext = jnp.concatenate([q4, jnp.concatenate([unchosen] * HEADS_PER_KV, axis=0)], axis=1)

        kw = jnp.concatenate([kw_scr[wc0 + i, rows_g, :] for i in range(n_wc)], axis=1)
        vw = jnp.concatenate([vw_scr[wc0 + i, rows_g, :] for i in range(n_wc)], axis=1)
        vw = jnp.concatenate([vw, _sum_rows(wlen)], axis=0)

        def attend(c, carry, bias):
            m, acc = carry
            s = _dot(q_ext, ks_scr[c, g])
            if bias is not None:
                s = s + bias
            m_new = jnp.maximum(m, jnp.max(s, axis=-1, keepdims=True))
            p = jnp.exp(s - m_new)
            acc = jnp.exp(m - m_new) * acc + _dot_nt(p.astype(BF16), vs_scr[c, g])
            return m_new, acc

        rows = HEADS_PER_KV * tq
        init = (jnp.full((rows, 1), NEG_INF, F32), jnp.zeros((rows, 2 * HEAD_DIM), F32))
        carry = lax.fori_loop(0, last, lambda c, carry: attend(c, carry, None), init)
        _, acc_s = attend(last, carry, causal4)
        o_s = acc_s[:, 0:HEAD_DIM] / jnp.maximum(acc_s[:, HEAD_DIM:HEAD_DIM + 1], 1e-30)

        s_w = _dot(q4, kw) + wbias4
        m_w = jnp.max(s_w, axis=-1, keepdims=True)
        p_w = jnp.exp(s_w - jnp.where(m_w == NEG_INF, 0.0, m_w))
        acc_w = _dot_nt(p_w.astype(BF16), vw)
        o_w = acc_w[:, 0:HEAD_DIM] / jnp.maximum(acc_w[:, HEAD_DIM:HEAD_DIM + 1], 1e-30)

        o = _gate_cols(gates, g, 0) * o_c + _gate_cols(gates, g, 1) * o_s + _gate_cols(gates, g, 2) * o_w
        for r in range(HEADS_PER_KV):
            head = g * HEADS_PER_KV + r
            o_ref[:, head * HEAD_DIM:(head + 1) * HEAD_DIM] = o[r * tq:(r + 1) * tq].astype(BF16)


def _attn_prompt(q, gates, ck, cv, selt, wint, tq):
    b, kvw, s_len = selt.shape
    nc = ck.shape[1]
    hq = N_HEADS * HEAD_DIM
    nq = s_len // tq
    tk = min(512, s_len)
    assert s_len % tk == 0 and s_len % tq == 0 and tk % tq == 0 and tq % V7X_LANES == 0 and nc <= V7X_LANES
    assert min(WINDOW + tq, s_len) % WIN_CHUNK == 0 and HEAD_DIM + s_len // SEL_BLOCK <= V7X_LANES
    return pl.pallas_call(
        functools.partial(_attn_prompt_kernel, tk=tk),
        out_shape=jax.ShapeDtypeStruct((b * s_len, hq), BF16),
        grid=(b, nq),
        in_specs=[
            pl.BlockSpec((tq, hq), lambda i, j: (i * nq + j, 0)),
            pl.BlockSpec((tq, V7X_LANES), lambda i, j: (i * nq + j, 0)),
            pl.BlockSpec((1, nc, KV_WIDTH), lambda i, j: (i, 0, 0)),
            pl.BlockSpec((1, nc, KV_WIDTH), lambda i, j: (i, 0, 0)),
            pl.BlockSpec((1, kvw, s_len), lambda i, j: (i, 0, 0)),
            pl.BlockSpec((1, kvw, s_len), lambda i, j: (i, 0, 0)),
        ],
        out_specs=pl.BlockSpec((tq, hq), lambda i, j: (i * nq + j, 0)),
        scratch_shapes=[
            pltpu.VMEM((s_len // tk, KV_HEADS, V7X_LANES, tk), BF16),
            pltpu.VMEM((s_len // tk, KV_HEADS, 2 * HEAD_DIM, tk), BF16),
            pltpu.VMEM((s_len // WIN_CHUNK, KV_WIDTH, WIN_CHUNK), BF16),
            pltpu.VMEM((s_len // WIN_CHUNK, KV_WIDTH, WIN_CHUNK), BF16),
        ],
        compiler_params=_cparams("arbitrary", "arbitrary"),
        name="attn_prompt",
    )(q, gates, ck, cv, selt, wint)


PAGES_PER_STEP = 16
CMP_PITCH = CMP_BLOCK + V7X_SUBLANES
DECODE_TOK_PAD = V7X_SUBLANES


def _block_diag_q(q):
    parts = []
    for g in range(KV_HEADS):
        q4 = _stack_heads(q, g)
        z = jnp.zeros_like(q4)
        parts.append(jnp.concatenate([q4 if gg == g else z for gg in range(KV_HEADS)], axis=1))
    return jnp.concatenate(parts, axis=0)


def _attn_decode_kernel(pt_ref, q_ref, gate_ref, nsel_ref, nwin_ref, wbuf_ref, bdk_ref, bdv_ref, pek_ref, pev_ref,
                        rc_ref, ra_ref, rb_ref, *rest, page, n_pages):
    pps = PAGES_PER_STEP
    cpages = rest[:pps]
    spages = rest[pps:2 * pps]
    o_ref = rest[2 * pps]
    cmp_scr, s_scr, vs_scr, qbd_scr, perm_scr = rest[2 * pps + 1:]
    blocks_per_page = page // CMP_BLOCK
    del pt_ref
    j = pl.program_id(1)
    tq = q_ref.shape[1]
    past_len = n_pages * page
    n_lane_chunks = 2 * KV_WIDTH // V7X_LANES

    @pl.when(j == 0)
    def _():
        qbd_scr[...] = _block_diag_q(q_ref[0].astype(F32)).astype(BF16)

    qbd = qbd_scr[...]
    for p in range(pps):
        for c in range(n_lane_chunks):
            rows_pc = cpages[p][0, 0, c * V7X_LANES:(c + 1) * V7X_LANES, :].T
            for n in range(blocks_per_page):
                r0 = pl.multiple_of(((j * pps + p) * blocks_per_page + n) * CMP_PITCH, V7X_SUBLANES)
                cmp_scr[c, pl.ds(r0, CMP_BLOCK), :] = rows_pc[n * CMP_BLOCK:(n + 1) * CMP_BLOCK]
    k_step = jnp.concatenate([spages[p][0, 0, 0:KV_WIDTH, :] for p in range(pps)], axis=1).astype(BF16)
    s_scr[j] = _dot(qbd, k_step)
    vs_scr[j] = jnp.concatenate([spages[p][0, 0, KV_WIDTH:2 * KV_WIDTH, :] for p in range(pps)],
                                axis=1).astype(BF16)

    @pl.when(j == pl.num_programs(1) - 1)
    def _():
        tok = lax.broadcasted_iota(jnp.int32, (tq, 1), 0)
        t_rows = past_len + jnp.concatenate([tok] * N_HEADS, axis=0)
        nsel = _pad_to(nsel_ref[0], page, 2 * KV_WIDTH)
        s_new = _dot_nt(qbd, nsel[:, 0:KV_WIDTH].astype(BF16))

        nc = past_len // CMP_BLOCK
        half = nc // 2

        def load_rows(c, l):
            return cmp_scr[c, pl.ds(l, nc, stride=CMP_PITCH), :]

        acck, accv = _compress_rows(load_rows, bdk_ref, bdv_ref, pek_ref, pev_ref, nc, perm_scr)
        ck = _rope(acck, rc_ref[...], ra_ref[...], rb_ref[...]).astype(BF16)
        p_c = _masked_softmax(_dot_nt(qbd, ck), _cmp_end_row(nc) <= t_rows)
        o_c = _dot(p_c.astype(BF16), accv.astype(BF16))

        ns = n_pages * page // SEL_BLOCK + 1
        grp = HEADS_PER_KV * tq
        psum = []
        for g in range(KV_HEADS):
            pg_ = p_c[g * grp:g * grp + tq]
            for r in range(1, HEADS_PER_KV):
                pg_ = pg_ + p_c[g * grp + r * tq:g * grp + (r + 1) * tq]
            psum.append(pg_)
        psum = jnp.concatenate(psum, axis=0)
        imp = jnp.concatenate([psum[:, 0:half] + psum[:, half:nc], jnp.zeros((KV_HEADS * tq, 1), F32)], axis=1)
        selm = _select_blocks(imp, past_len + jnp.concatenate([tok] * KV_HEADS, axis=0), ns)
        selm_rows = jnp.concatenate(
            [selm[g * tq:(g + 1) * tq] for g in range(KV_HEADS) for _ in range(HEADS_PER_KV)], axis=0)
        nkeys = (n_pages + 1) * page
        kpos = lax.broadcasted_iota(jnp.int32, (1, nkeys), 1)
        brow = lax.broadcasted_iota(jnp.int32, (ns, 1), 0)
        expand = jnp.where(kpos // SEL_BLOCK == brow, 1.0, 0.0).astype(BF16)
        chosen = _dot(selm_rows.astype(BF16), expand)
        n_steps = n_pages // pps
        s_all = jnp.concatenate([s_scr[i] for i in range(n_steps)] + [s_new], axis=1)
        s_all = jnp.where(kpos <= t_rows, jnp.where(chosen > 0.5, s_all, NEG_INF), NEG_INF)
        p_s = _softmax_rows(s_all).astype(BF16)
        vt_all = jnp.concatenate([vs_scr[i] for i in range(n_steps)], axis=1)
        o_s = _dot_nt(p_s[:, 0:past_len], vt_all) + _dot(p_s[:, past_len:nkeys], nsel[:, KV_WIDTH:].astype(BF16))

        wb = wbuf_ref.shape[3]
        nwin = _pad_to(nwin_ref[0], page, 2 * KV_WIDTH)
        s_w = jnp.concatenate([_dot(qbd, wbuf_ref[0, 0, 0:KV_WIDTH, :].astype(BF16)),
                               _dot_nt(qbd, nwin[:, 0:KV_WIDTH].astype(BF16))], axis=1)
        wpos = past_len - wb + lax.broadcasted_iota(jnp.int32, (1, wb + page), 1)
        s_w = jnp.where(wpos <= t_rows, jnp.where(wpos >= t_rows - WINDOW, s_w, NEG_INF), NEG_INF)
        p_w = _softmax_rows(s_w).astype(BF16)
        o_w = (_dot_nt(p_w[:, 0:wb], wbuf_ref[0, 0, KV_WIDTH:2 * KV_WIDTH, :].astype(BF16))
               + _dot(p_w[:, wb:wb + page], nwin[:, KV_WIDTH:].astype(BF16)))

        gates = gate_ref[0]
        gcol = lambda br: jnp.concatenate([_gate_cols(gates, g, br) for g in range(KV_HEADS)], axis=0)
        o = gcol(0) * o_c + gcol(1) * o_s + gcol(2) * o_w
        for g in range(KV_HEADS):
            for r in range(HEADS_PER_KV):
                col = (g * HEADS_PER_KV + r) * HEAD_DIM
                o_ref[0, :, col:col + HEAD_DIM] = o[g * grp + r * tq:g * grp + (r + 1) * tq,
                                                    g * HEAD_DIM:(g + 1) * HEAD_DIM]


def _attn_decode(page_table, q3, gates3, nsel3, nwin3, cache_cmp_t, cache_sel_t, cache_win_t, layer, cw):
    b, tq, hq = q3.shape
    n_pages = page_table.shape[1]
    page = cache_cmp_t.shape[3]
    kvw = 2 * KV_WIDTH
    pps = PAGES_PER_STEP
    assert n_pages % pps == 0 and (n_pages * page) % (2 * CMP_BLOCK) == 0 and tq < CMP_BLOCK
    assert tq % V7X_SUBLANES == 0 and page == V7X_LANES
    nc = n_pages * page // CMP_BLOCK
    rt = _rope_tables(_cmp_order_end_pos(nc), KV_WIDTH)
    wb = cache_win_t.shape[3]
    full = lambda shp: pl.BlockSpec(shp, lambda *_: (0,) * len(shp))
    per_seq = lambda shp: pl.BlockSpec((1,) + shp, lambda i, j, pt: (i,) + (0,) * len(shp))
    page_spec = lambda p: pl.BlockSpec((1, 1, kvw, page), lambda i, j, pt: (layer, pt[i, j * pps + p], 0, 0))
    rows = N_HEADS * tq
    grid_spec = pltpu.PrefetchScalarGridSpec(
        num_scalar_prefetch=1,
        grid=(b, n_pages // pps),
        in_specs=[
            per_seq((tq, hq)), per_seq((tq, V7X_LANES)), per_seq((tq, kvw)), per_seq((tq, kvw)),
            pl.BlockSpec((1, 1, kvw, wb), lambda i, j, pt: (layer, i, 0, 0)),
            full(cw["bdk"].shape), full(cw["bdv"].shape), full(cw["pek"].shape), full(cw["pev"].shape),
            full(rt[0].shape), full(rt[1].shape), full(rt[2].shape),
            *[page_spec(p) for p in range(pps)],
            *[page_spec(p) for p in range(pps)],
        ],
        out_specs=per_seq((tq, hq)),
        scratch_shapes=[
            pltpu.VMEM((kvw // V7X_LANES, nc * CMP_PITCH, V7X_LANES), F32),
            pltpu.VMEM((n_pages // pps, rows, pps * page), F32),
            pltpu.VMEM((n_pages // pps, KV_WIDTH, pps * page), BF16),
            pltpu.VMEM((rows, KV_WIDTH), BF16),
            pltpu.VMEM((nc, V7X_LANES), F32),
        ],
    )
    return pl.pallas_call(
        functools.partial(_attn_decode_kernel, page=page, n_pages=n_pages),
        out_shape=jax.ShapeDtypeStruct((b, tq, hq), F32),
        grid_spec=grid_spec,
        compiler_params=_cparams("arbitrary", "arbitrary"),
        name="attn_decode",
    )(page_table, q3, gates3, nsel3, nwin3, cache_win_t, cw["bdk"], cw["bdv"], cw["pek"], cw["pev"], *rt,
      *([cache_cmp_t] * pps), *([cache_sel_t] * pps))


def _out_proj_kernel(o_ref, w_ref, x_ref, mod_ref, xo_ref):
    d = x_ref.shape[1]
    xo_ref[...] = x_ref[...] + mod_ref[0][:, 2 * d:3 * d] * _dot(o_ref[...], w_ref[...])


def _out_proj(o, w_bf16, x, mod3, tpb, tm):
    t, d = x.shape
    k = o.shape[1]
    return pl.pallas_call(
        _out_proj_kernel,
        out_shape=jax.ShapeDtypeStruct((t, d), F32),
        grid=(t // tm,),
        in_specs=[
            pl.BlockSpec((tm, k), lambda i: (i, 0)),
            pl.BlockSpec((k, d), lambda i: (0, 0)),
            pl.BlockSpec((tm, d), lambda i: (i, 0)),
            _mod_spec(mod3, tpb),
        ],
        out_specs=pl.BlockSpec((tm, d), lambda i: (i, 0)),
        compiler_params=_cparams("arbitrary"),
        name="out_proj",
    )(o, w_bf16, x, mod3)


def _ffn_kernel(x_ref, mod_ref, g_ref, wg_ref, wu_ref, wd_ref, o_ref, h_scr):
    d = x_ref.shape[1]
    f = pl.program_id(1)

    @pl.when(f == 0)
    def _():
        mod = mod_ref[0]
        h_scr[...] = _norm_mod(x_ref[...], g_ref[...], mod[:, 4 * d:5 * d], mod[:, 3 * d:4 * d]).astype(BF16)
        o_ref[...] = jnp.zeros(o_ref.shape, F32)

    h = h_scr[...]
    a = _dot(h, wg_ref[0].astype(BF16))
    u = _dot(h, wu_ref[0].astype(BF16))
    act = (a * _sigmoid(a) * u).astype(BF16)
    o_ref[...] += _dot(act, wd_ref[0].astype(BF16))

    @pl.when(f == pl.num_programs(1) - 1)
    def _():
        o_ref[...] = x_ref[...] + mod_ref[0][:, 5 * d:6 * d] * o_ref[...]


def _ffn(x, mod3, tpb, g, w_gu_all, w_down_all, layer, tm, tf):
    t, d = x.shape
    ff = w_down_all.shape[1]
    nf = ff // tf
    return pl.pallas_call(
        _ffn_kernel,
        out_shape=jax.ShapeDtypeStruct((t, d), F32),
        grid=(t // tm, nf),
        in_specs=[
            pl.BlockSpec((tm, d), lambda i, f: (i, 0)),
            _mod_spec(mod3, tpb),
            pl.BlockSpec((1, d), lambda i, f: (0, 0)),
            pl.BlockSpec((1, d, tf), lambda i, f: (layer, 0, f)),
            pl.BlockSpec((1, d, tf), lambda i, f: (layer, 0, nf + f)),
            pl.BlockSpec((1, tf, d), lambda i, f: (layer, f, 0)),
        ],
        out_specs=pl.BlockSpec((tm, d), lambda i, f: (i, 0)),
        scratch_shapes=[pltpu.VMEM((tm, d), BF16)],
        compiler_params=_cparams("arbitrary", "arbitrary"),
        name="ffn",
    )(x, mod3, g, w_gu_all, w_gu_all, w_down_all)


def _conv_in(x_ref, mod_ref, g_ref, win_ref):
    d = x_ref.shape[1]
    mod = mod_ref[0]
    h = _norm_mod(x_ref[...], g_ref[...], mod[:, d:2 * d], mod[:, 0:d]).astype(BF16)
    bg = _dot(h, win_ref[:, 0:d])
    u = _dot(h, win_ref[:, d:2 * d]) * _dot(h, win_ref[:, 2 * d:3 * d])
    return bg, u


def _conv_out(bg, conv, x_ref, mod_ref, wout_ref, xo_ref):
    d = x_ref.shape[1]
    y = _dot((bg * conv).astype(BF16), wout_ref[...])
    xo_ref[...] = x_ref[...] + mod_ref[0][:, 2 * d:3 * d] * y


def _conv_seq_kernel(x_ref, mod_ref, g_ref, win_ref, wc_ref, wout_ref, st_ref, xo_ref, nst_ref, carry_scr):
    tm = x_ref.shape[0]
    bg, u = _conv_in(x_ref, mod_ref, g_ref, win_ref)

    @pl.when(pl.program_id(1) == 0)
    def _():
        carry_scr[6:8, :] = st_ref[0]

    ue = jnp.concatenate([carry_scr[...], u], axis=0)
    conv = wc_ref[0:1, :] * ue[6:6 + tm] + wc_ref[1:2, :] * ue[7:7 + tm] + wc_ref[2:3, :] * u
    carry_scr[...] = u[tm - 8:tm]
    nst_ref[0] = u[tm - 2:tm]
    _conv_out(bg, conv, x_ref, mod_ref, wout_ref, xo_ref)


def _conv_tok_kernel(x_ref, mod_ref, g_ref, win_ref, wc_ref, wout_ref, p1_ref, p2_ref, xo_ref, u_ref, *, t_seq):
    tm = x_ref.shape[0]
    bg, u = _conv_in(x_ref, mod_ref, g_ref, win_ref)
    row = lax.broadcasted_iota(jnp.int32, (tm, 1), 0) % t_seq
    prev1 = jnp.where(row >= 1, pltpu.roll(u, 1, 0), p1_ref[...])
    prev2 = jnp.where(row >= 2, pltpu.roll(u, 2, 0), p2_ref[...])
    conv = wc_ref[0:1, :] * prev2 + wc_ref[1:2, :] * prev1 + wc_ref[2:3, :] * u
    u_ref[...] = u
    _conv_out(bg, conv, x_ref, mod_ref, wout_ref, xo_ref)


def _conv_mixer(x, mod3, tpb, g, w_in, w_conv, w_out, state, b, t_seq, tm):
    t, d = x.shape
    weights = [_full((1, d)), _full(w_in.shape), _full(w_conv.shape), _full(w_out.shape)]
    if state is None:
        state = jnp.zeros((b, 2, d), F32)
    if t_seq >= tm:
        nt = t_seq // tm
        xo, nst = pl.pallas_call(
            _conv_seq_kernel,
            out_shape=(jax.ShapeDtypeStruct((t, d), F32), jax.ShapeDtypeStruct((b, 2, d), F32)),
            grid=(b, nt),
            in_specs=[pl.BlockSpec((tm, d), lambda i, j: (i * nt + j, 0)),
                      pl.BlockSpec((1, 1, mod3.shape[-1]), lambda i, j: (i, 0, 0)),
                      *weights,
                      pl.BlockSpec((1, 2, d), lambda i, j: (i, 0, 0))],
            out_specs=(pl.BlockSpec((tm, d), lambda i, j: (i * nt + j, 0)),
                       pl.BlockSpec((1, 2, d), lambda i, j: (i, 0, 0))),
            scratch_shapes=[pltpu.VMEM((8, d), F32)],
            compiler_params=_cparams("arbitrary", "arbitrary"),
            name="conv_seq",
        )(x, mod3, g, w_in, w_conv, w_out, state)
        return xo, nst
    assert t == tm and t_seq >= 2
    p1 = jnp.zeros((b, t_seq, d), F32).at[:, 0].set(state[:, 1]).reshape(t, d)
    p2 = jnp.zeros((b, t_seq, d), F32).at[:, 0].set(state[:, 0]).at[:, 1].set(state[:, 1]).reshape(t, d)
    xo, u = pl.pallas_call(
        functools.partial(_conv_tok_kernel, t_seq=t_seq),
        out_shape=(jax.ShapeDtypeStruct((t, d), F32), jax.ShapeDtypeStruct((t, d), F32)),
        grid=(1,),
        in_specs=[_full((t, d)), _full(mod3.shape), *weights, _full((t, d)), _full((t, d))],
        out_specs=(_full((t, d)), _full((t, d))),
        compiler_params=_cparams("arbitrary"),
        name="conv_tok",
    )(x, mod3, g, w_in, w_conv, w_out, p1, p2)
    return xo, u.reshape(b, t_seq, d)[:, t_seq - 2:]


MOE_ROW_UNIT = 128
MOE_BOUNDS_ROWS = 16
MOE_FF_CHUNK = 1792
MOE_ARM_MAX = 512
MOE_ARM_MIN = 128
MOE_GATHER_BLOCKS = 4
MOE_SCATTER_UNITS = 2


def _split_bf16(x):
    hi = x.astype(BF16)
    return hi, (x - hi.astype(F32)).astype(BF16)


def _moe_route_kernel(x_ref, mod_ref, g_ref, wr_ref, br_ref, h_ref, pos_ref, comb_ref, post_ref, bnd_ref, *, blk):
    tm, d = x_ref.shape
    nb = tm // blk
    mod = mod_ref[0]
    h32 = _norm_mod(x_ref[...], g_ref[...], mod[:, 4 * d:5 * d], mod[:, 3 * d:4 * d])
    h_ref[...] = h32.astype(BF16)
    lane = lax.broadcasted_iota(jnp.int32, (1, V7X_LANES), 1)
    h_hi, h_lo = _split_bf16(h32)
    w_hi, w_lo = _split_bf16(wr_ref[...])
    logits = _dot(h_hi, w_hi) + (_dot(h_hi, w_lo) + _dot(h_lo, w_hi)) + br_ref[...]
    logits = jnp.where(lane < N_EXPERTS, logits, NEG_INF)
    m1 = jnp.max(logits, axis=-1, keepdims=True)
    i1 = jnp.min(jnp.where(logits == m1, lane, V7X_LANES), axis=-1, keepdims=True)
    rest = jnp.where(lane == i1, NEG_INF, logits)
    m2 = jnp.max(rest, axis=-1, keepdims=True)
    i2 = jnp.min(jnp.where(rest == m2, lane, V7X_LANES), axis=-1, keepdims=True)
    e2 = jnp.exp(m2 - m1)
    den = 1.0 + e2
    comb_ref[...] = jnp.where(lane == i1, 1.0 / den, jnp.where(lane == i2, e2 / den, 0.0))
    mask = jnp.where(lane == i1, 1.0, jnp.where(lane == i2, 1.0, 0.0))
    r = lax.broadcasted_iota(jnp.int32, (blk, blk), 0)
    c = lax.broadcasted_iota(jnp.int32, (blk, blk), 1)
    lower = jnp.where(c < r, 1.0, 0.0).astype(BF16)
    upper = jnp.where(r < c, 1.0, 0.0).astype(BF16)
    carry = jnp.zeros((1, V7X_LANES), F32)
    bnd_ref[0] = jnp.zeros((MOE_BOUNDS_ROWS, V7X_LANES), F32)
    for k in range(nb):
        mb = mask[k * blk:(k + 1) * blk]
        pos_ref[k * blk:(k + 1) * blk, :] = jnp.where(mb > 0.5, _dot(lower, mb.astype(BF16)) + carry, -1.0)
        carry = carry + jnp.sum(mb, axis=0, keepdims=True)
        bnd_ref[0, k + 1:k + 2, :] = carry
    eye = jnp.where(lax.broadcasted_iota(jnp.int32, (V7X_LANES, V7X_LANES), 0)
                    == lax.broadcasted_iota(jnp.int32, (V7X_LANES, V7X_LANES), 1), 1.0, 0.0).astype(BF16)
    mask_t = _dot_nt(eye, mask.astype(BF16))[0:N_EXPERTS]
    carry_t = jnp.zeros((N_EXPERTS, 1), F32)
    for k in range(nb):
        mb = mask_t[:, k * blk:(k + 1) * blk]
        post_ref[0, k] = jnp.where(mb > 0.5, _dot(mb.astype(BF16), upper) + carry_t, -1.0)
        carry_t = carry_t + jnp.sum(mb, axis=1, keepdims=True)


def _moe_route(x, mod3, tpb, g, w_router, b_router, tm, blk):
    t, d = x.shape
    ne = w_router.shape[1]
    nt = t // tm
    nb = tm // blk
    assert nb + 1 <= MOE_BOUNDS_ROWS
    wr = jnp.pad(w_router, ((0, 0), (0, V7X_LANES - ne)))
    br = jnp.pad(b_router, (0, V7X_LANES - ne)).reshape(1, V7X_LANES)
    tok = lambda w: pl.BlockSpec((tm, w), lambda i: (i, 0))
    return pl.pallas_call(
        functools.partial(_moe_route_kernel, blk=blk),
        out_shape=(
            jax.ShapeDtypeStruct((t, d), BF16),
            jax.ShapeDtypeStruct((t, V7X_LANES), F32),
            jax.ShapeDtypeStruct((t, V7X_LANES), F32),
            jax.ShapeDtypeStruct((nt, nb, N_EXPERTS, blk), F32),
            jax.ShapeDtypeStruct((nt, MOE_BOUNDS_ROWS, V7X_LANES), F32),
        ),
        grid=(nt,),
        in_specs=[tok(d), _mod_spec(mod3, tpb), _full((1, d)), _full((d, V7X_LANES)), _full((1, V7X_LANES))],
        out_specs=(tok(d), tok(V7X_LANES), tok(V7X_LANES),
                   pl.BlockSpec((1, nb, N_EXPERTS, blk), lambda i: (i, 0, 0, 0)),
                   pl.BlockSpec((1, MOE_BOUNDS_ROWS, V7X_LANES), lambda i: (i, 0, 0))),
        compiler_params=_cparams("arbitrary"),
        name="moe_route",
    )(x, mod3, g, wr, br)


def _moe_expert_kernel(bnd_ref, h_ref, pos_ref, comb_ref, post_ref, wg_ref, wu_ref, wd_ref, o_ref,
                       hc_scr, acc_scr, posc_scr, combc_scr, *, blk):
    tm, d = h_ref.shape
    nb = tm // blk
    unit = MOE_ROW_UNIT
    i = pl.program_id(0)
    e = pl.program_id(1)
    f = pl.program_id(2)
    nf = pl.num_programs(2)
    base = (i * N_EXPERTS + e) * MOE_BOUNDS_ROWS
    cnt = bnd_ref[base + nb]
    n_units = (cnt + unit - 1) // unit

    def span(u):
        start = u * unit
        stop = jnp.minimum(start + unit, cnt)
        lo = jnp.int32(0)
        hi = jnp.int32(0)
        for k in range(1, nb):
            bk = bnd_ref[base + k]
            lo = lo + (bk <= start).astype(jnp.int32)
            hi = hi + (bk < stop).astype(jnp.int32)
        return lo, hi + 1

    @pl.when((i == 0) & (e == 0) & (f == 0))
    def _():
        acc_scr[...] = jnp.zeros(acc_scr.shape, F32)

    @pl.when((e == 0) & (f == 0))
    def _():
        o_ref[...] = jnp.zeros(o_ref.shape, F32)

    gw = min(MOE_GATHER_BLOCKS, nb)

    @pl.when(f == 0)
    def _():
        def per_unit(u, _):
            lo, hi = span(u)
            r0 = pl.multiple_of(u * unit, unit)
            want = (r0 + lax.broadcasted_iota(jnp.int32, (unit, 1), 0)).astype(F32)
            lo_w = jnp.minimum(lo, nb - gw)
            pos_t = jnp.concatenate([post_ref[0, lo_w + k, pl.ds(e, 1), :] for k in range(gw)], axis=1)
            onehot = jnp.where(pos_t == want, 1.0, 0.0).astype(BF16)
            t0 = pl.multiple_of(lo_w * blk, blk)
            hc_scr[pl.ds(r0, unit), :] = _dot(onehot, h_ref[pl.ds(t0, gw * blk), :]).astype(BF16)

            def per_blk(k, _):
                t1 = pl.multiple_of(k * blk, blk)
                oh = jnp.where(post_ref[0, k, pl.ds(e, 1), :] == want, 1.0, 0.0).astype(BF16)
                hc_scr[pl.ds(r0, unit), :] += _dot(oh, h_ref[pl.ds(t1, blk), :]).astype(BF16)
                return 0

            lax.fori_loop(lo_w + gw, hi, per_blk, 0)
            acc_scr[pl.ds(r0, unit), :] = jnp.zeros((unit, d), F32)
            return 0

        lax.fori_loop(0, n_units, per_unit, 0)

    def ffn_rows(r0, m):
        hc = hc_scr[pl.ds(r0, m), :]
        a = _dot(hc, wg_ref[0, 0])
        act = (a * _sigmoid(a) * _dot(hc, wu_ref[0, 0])).astype(BF16)
        acc_scr[pl.ds(r0, m), :] += _dot(act, wd_ref[0, 0])

    big = min(MOE_ARM_MAX, tm)
    n_min = (cnt + MOE_ARM_MIN - 1) // MOE_ARM_MIN
    per_big = big // MOE_ARM_MIN
    n_big = n_min // per_big
    rem = n_min - per_big * n_big

    def big_arm(k, _):
        ffn_rows(pl.multiple_of(k * big, big), big)
        return 0

    lax.fori_loop(0, n_big, big_arm, 0)
    r_arm = n_big * big
    size = big // 2
    while size >= MOE_ARM_MIN:
        bit = size // MOE_ARM_MIN
        if size <= tm:
            taken = (rem // bit) % 2 == 1

            @pl.when(taken)
            def _(r_arm=r_arm, size=size):
                ffn_rows(pl.multiple_of(r_arm, size), size)

            r_arm = r_arm + jnp.where(taken, size, 0)
        size //= 2

    @pl.when(f == nf - 1)
    def _():
        lane = lax.broadcasted_iota(jnp.int32, (1, V7X_LANES), 1)
        posc_scr[...] = jnp.sum(jnp.where(lane == e, pos_ref[...], 0.0), axis=-1, keepdims=True)
        combc_scr[...] = jnp.sum(jnp.where(lane == e, comb_ref[...], 0.0), axis=-1, keepdims=True)

        sw = min(MOE_SCATTER_UNITS * unit, tm)

        def per_blk(k, _):
            start = bnd_ref[base + k]
            stop = bnd_ref[base + k + 1]
            t0 = pl.multiple_of(k * blk, blk)
            r0 = pl.multiple_of(jnp.minimum((start // unit) * unit, tm - sw), unit)
            pc = posc_scr[pl.ds(t0, blk), :]
            cc = combc_scr[pl.ds(t0, blk), :]
            want = (r0 + lax.broadcasted_iota(jnp.int32, (1, sw), 1)).astype(F32)
            onehot = jnp.where(pc == want, 1.0, 0.0).astype(BF16)
            o_ref[pl.ds(t0, blk), :] += cc * _dot(onehot, acc_scr[pl.ds(r0, sw), :].astype(BF16))

            def per_unit(u, _):
                r1 = pl.multiple_of(u * unit, unit)
                want1 = (r1 + lax.broadcasted_iota(jnp.int32, (1, unit), 1)).astype(F32)
                oh = jnp.where(pc == want1, 1.0, 0.0).astype(BF16)
                o_ref[pl.ds(t0, blk), :] += cc * _dot(oh, acc_scr[pl.ds(r1, unit), :].astype(BF16))
                return 0

            lax.fori_loop((r0 + sw) // unit, (stop + unit - 1) // unit, per_unit, 0)
            return 0

        lax.fori_loop(0, nb, per_blk, 0)


def _moe_experts(h, pos, comb, post, bounds, w_gu_all, w_down_all, layer, tm, tf, blk):
    t, d = h.shape
    ne, ff = w_down_all.shape[1], w_down_all.shape[2]
    nf = ff // tf
    nt = t // tm
    nb = tm // blk
    bnd = jnp.transpose(bounds[:, :, :ne], (0, 2, 1)).astype(jnp.int32).reshape(nt * ne * MOE_BOUNDS_ROWS)
    once = pl.Buffered(1)
    tok = lambda w: pl.BlockSpec((tm, w), lambda i, e, f, b_: (i, 0), pipeline_mode=once)
    grid_spec = pltpu.PrefetchScalarGridSpec(
        num_scalar_prefetch=1,
        grid=(nt, ne, nf),
        in_specs=[
            tok(d), tok(V7X_LANES), tok(V7X_LANES),
            pl.BlockSpec((1, nb, ne, blk), lambda i, e, f, b_: (i, 0, 0, 0)),
            pl.BlockSpec((1, 1, d, tf), lambda i, e, f, b_: (layer, e, 0, f)),
            pl.BlockSpec((1, 1, d, tf), lambda i, e, f, b_: (layer, e, 0, nf + f)),
            pl.BlockSpec((1, 1, tf, d), lambda i, e, f, b_: (layer, e, f, 0)),
        ],
        out_specs=pl.BlockSpec((tm, d), lambda i, e, f, b_: (i, 0), pipeline_mode=once),
        scratch_shapes=[
            pltpu.VMEM((tm, d), BF16),
            pltpu.VMEM((tm, d), F32),
            pltpu.VMEM((tm, 1), F32),
            pltpu.VMEM((tm, 1), F32),
        ],
    )
    return pl.pallas_call(
        functools.partial(_moe_expert_kernel, blk=blk),
        out_shape=jax.ShapeDtypeStruct((t, d), F32),
        grid_spec=grid_spec,
        compiler_params=_cparams("arbitrary", "arbitrary", "arbitrary"),
        name="moe_experts",
    )(bnd, h, pos, comb, post, w_gu_all, w_gu_all, w_down_all)


def _residual_kernel(x_ref, f_ref, mod_ref, g_ref, o_ref, *, final_norm):
    d = x_ref.shape[1]
    x = x_ref[...] + mod_ref[0][:, 5 * d:6 * d] * f_ref[...]
    if final_norm:
        x = (x * lax.rsqrt(jnp.mean(x * x, axis=-1, keepdims=True) + EPS)) * g_ref[...]
    o_ref[...] = x


def _residual(x, f, mod3, tpb, g_final, tm, final_norm):
    t, d = x.shape
    tok = pl.BlockSpec((tm, d), lambda i: (i, 0))
    return pl.pallas_call(
        functools.partial(_residual_kernel, final_norm=final_norm),
        out_shape=jax.ShapeDtypeStruct((t, d), F32),
        grid=(t // tm,),
        in_specs=[tok, tok, _mod_spec(mod3, tpb), _full((1, d))],
        out_specs=tok,
        compiler_params=_cparams("arbitrary"),
        name="residual",
    )(x, f, mod3, g_final)


def _moe(x, mod3_of, g, w_router, b_router, w_gu_all, w_down_all, layer, g_final, tm, final_norm):
    blk = min(256, tm)
    h, pos, comb, post, bounds = _moe_route(x, *mod3_of(tm), g, w_router, b_router, tm, blk)
    f = _moe_experts(h, pos, comb, post, bounds, w_gu_all, w_down_all, layer, tm, MOE_FF_CHUNK, blk)
    tm_res = min(1024, tm)
    return _residual(x, f, *mod3_of(tm_res), g_final, tm_res, final_norm)


def _final_norm_kernel(x_ref, g_ref, o_ref):
    x = x_ref[...]
    o_ref[...] = (x * lax.rsqrt(jnp.mean(x * x, axis=-1, keepdims=True) + EPS)) * g_ref[...]


def _final_norm(x, g, tm):
    t, d = x.shape
    return pl.pallas_call(
        _final_norm_kernel,
        out_shape=jax.ShapeDtypeStruct((t, d), F32),
        grid=(t // tm,),
        in_specs=[pl.BlockSpec((tm, d), lambda i: (i, 0)), pl.BlockSpec((1, d), lambda i: (0, 0))],
        out_specs=pl.BlockSpec((tm, d), lambda i: (i, 0)),
        compiler_params=_cparams("arbitrary"),
        name="final_norm",
    )(x, g)


def _nsa_layer_weights(w_in, w_out, w_ck, pe_ck, w_cv, pe_cv):
    n_main = N_HEADS * HEAD_DIM + 6 * KV_WIDTH
    w_gate = jnp.pad(w_in[:, n_main:], ((0, 0), (0, V7X_LANES - N_HEADS * N_BRANCH)))
    return {"w_main": w_in[:, :n_main].astype(BF16), "w_gate": w_gate.astype(BF16),
            "w_out": w_out.astype(BF16), "cw": _compress_weights(w_ck, pe_ck, w_cv, pe_cv)}


def _rows_from_feature_major(xt):
    b, _, s = xt.shape
    return jnp.transpose(xt.reshape(b, 2, KV_HEADS, HEAD_DIM, s), (0, 4, 1, 2, 3))


def _feature_major_from_rows(x, lead):
    n = len(lead)
    perm = tuple(range(n)) + (n + 1, n + 2, n + 3, n)
    xt = jnp.transpose(x, perm)
    return xt.reshape(lead + (2 * KV_WIDTH, x.shape[n]))


def _nsa_prompt_layer(x, mod, g, lw, b, s_len):
    tm = min(512, s_len)
    mod3 = mod.reshape(b, 1, mod.shape[-1])
    rope_t = _rope_tables(jnp.arange(s_len), V7X_LANES)
    q, gates, cmp_rows, cmpt, selt, wint = _nsa_proj(x, mod3, s_len // tm, g, lw["w_main"], lw["w_gate"], rope_t,
                                                     s_len, tm, True)
    ck, cv = _compress(cmp_rows.reshape(b, s_len, 2 * KV_WIDTH), lw["cw"])
    o = _attn_prompt(q, gates, ck, cv, selt, wint, tq=min(256, s_len))
    x = _out_proj(o, lw["w_out"], x, mod3, s_len // tm, tm)
    return x, cmpt, selt, wint


def _nsa_decode_layer(x, mod3, g, lw, b, t_seq, caches, layer):
    t = x.shape[0]
    page_table = caches["page_table"]
    past_len = page_table.shape[1] * caches["cmp"].shape[3]
    rope_t = _rope_tables(past_len + jnp.arange(t) % t_seq, V7X_LANES)
    q, gates, cmp_rows, sel_rows, win_rows = _nsa_proj(x, mod3, 1, g, lw["w_main"], lw["w_gate"], rope_t, t, t, False)
    pad3 = lambda a: jnp.pad(a.reshape(b, t_seq, a.shape[-1]), ((0, 0), (0, DECODE_TOK_PAD - t_seq), (0, 0)))
    o = _attn_decode(page_table, pad3(q), pad3(gates), pad3(sel_rows), pad3(win_rows),
                     caches["cmp"], caches["sel"], caches["win"], layer, lw["cw"])
    o = o[:, :t_seq].reshape(t, o.shape[-1]).astype(BF16)
    x = _out_proj(o, lw["w_out"], x, mod3, 1, t)
    return x, cmp_rows, sel_rows, win_rows


def _trunk(x3, mod, caches, p):
    b, t_seq, d = x3.shape
    t = b * t_seq
    kvw = 2 * KV_WIDTH
    prompt = caches is None
    x = x3.reshape(t, d)
    new_cmp, new_sel, new_win, new_conv = [], [], [], []
    depth = mod.shape[0]
    g_final = p["g_norm_final"].reshape(1, d)
    finished = False
    for i in range(depth):
        j = i // 2
        if prompt:
            mod_i = mod[i].reshape(b, 1, 6 * d)
            mod3_of = lambda tm, mod_i=mod_i: (mod_i, t_seq // tm)
            tm_small, tm_big, tm_moe = min(512, t_seq), min(1024, t_seq), min(2048, t_seq)
        else:
            mod_i = jnp.repeat(mod[i], t_seq, axis=0).reshape(1, t, 6 * d)
            mod3_of = lambda tm, mod_i=mod_i: (mod_i, 1)
            tm_small = tm_big = tm_moe = t
        g_mix = p["g_norm_mix"][i].reshape(1, d)
        g_ffn = p["g_norm_ffn"][i].reshape(1, d)
        if i % 2 == 0:
            lw = _nsa_layer_weights(p["w_nsa_in"][j], p["w_nsa_out"][j], p["w_cmp_k"][j], p["pe_cmp_k"][j],
                                    p["w_cmp_v"][j], p["pe_cmp_v"][j])
            if prompt:
                x, cmpt, selt, wint = _nsa_prompt_layer(x, mod[i], g_mix, lw, b, t_seq)
                new_cmp.append(_rows_from_feature_major(cmpt))
                new_sel.append(_rows_from_feature_major(selt))
                new_win.append(_rows_from_feature_major(wint[:, :, t_seq - min(WINDOW, t_seq):]))
            else:
                x, c_rows, s_rows, w_rows = _nsa_decode_layer(x, mod_i, g_mix, lw, b, t_seq, caches, j)
                rows5 = lambda r: r.reshape(b, t_seq, 2, KV_HEADS, HEAD_DIM)
                new_cmp.append(rows5(c_rows))
                new_sel.append(rows5(s_rows))
                new_win.append(jnp.concatenate([caches["win_rows"][j], rows5(w_rows)], axis=1)[:, t_seq:])
            x = _ffn(x, *mod3_of(tm_big), g_ffn, p["w_ff_gu"], p["w_ff_down"], j, tm_big, 512)
        else:
            state = None if prompt else caches["conv"][j]
            x, st = _conv_mixer(x, *mod3_of(tm_small), g_mix, p["w_conv_in"][j].astype(BF16), p["w_conv"][j],
                                p["w_conv_out"][j].astype(BF16), state, b, t_seq, tm_small)
            new_conv.append(st)
            finished = i == depth - 1
            x = _moe(x, mod3_of, g_ffn, p["w_router"][j], p["b_router"][j], p["w_moe_gu"], p["w_moe_down"], j,
                     g_final, tm_moe, finished)
    y = x if finished else _final_norm(x, g_final, tm_big)
    return (y.reshape(b, t_seq, d), jnp.stack(new_cmp), jnp.stack(new_sel), jnp.stack(new_win), jnp.stack(new_conv))


def kernel(x_prompt, x_sample, cache_cmp_kv, cache_sel_kv, cache_win_kv, state_conv, page_table, c_prompt, c_sample,
           w_ada, b_ada, g_norm_mix, g_norm_ffn, g_norm_final, w_nsa_in, w_nsa_out, w_cmp_k, pe_cmp_k, w_cmp_v,
           pe_cmp_v, w_conv_in, w_conv, w_conv_out, w_ff_gu, w_ff_down, w_router, b_router, w_moe_gu, w_moe_down):
    p = {"g_norm_mix": g_norm_mix, "g_norm_ffn": g_norm_ffn, "g_norm_final": g_norm_final,
         "w_nsa_in": w_nsa_in, "w_nsa_out": w_nsa_out, "w_cmp_k": w_cmp_k, "pe_cmp_k": pe_cmp_k, "w_cmp_v": w_cmp_v,
         "pe_cmp_v": pe_cmp_v, "w_conv_in": w_conv_in, "w_conv": w_conv, "w_conv_out": w_conv_out,
         "w_ff_gu": w_ff_gu, "w_ff_down": w_ff_down, "w_router": w_router, "b_router": b_router,
         "w_moe_gu": w_moe_gu.astype(BF16), "w_moe_down": w_moe_down.astype(BF16)}
    nb = x_prompt.shape[0]
    mod = _ada(jnp.concatenate([c_prompt, c_sample], axis=0), w_ada, b_ada)
    y_p, cmp_p, sel_p, win_p, conv_p = _trunk(x_prompt, mod[:, :nb], None, p)
    caches = {"cmp": _feature_major_from_rows(cache_cmp_kv, cache_cmp_kv.shape[:2]),
              "sel": _feature_major_from_rows(cache_sel_kv, cache_sel_kv.shape[:2]),
              "win": _feature_major_from_rows(cache_win_kv, cache_win_kv.shape[:2]),
              "win_rows": cache_win_kv, "conv": state_conv, "page_table": page_table}
    y_s, cmp_s, sel_s, win_s, conv_s = _trunk(x_sample, mod[:, nb:], caches, p)
    return (y_p, y_s, cmp_p, cmp_s, sel_p, sel_s, win_p, win_s, conv_p, conv_s)
```

```python
import functools

import jax
import jax.numpy as jnp
from jax import lax
from jax.experimental import pallas as pl
from jax.experimental.pallas import tpu as pltpu

F32 = jnp.float32
BF16 = jnp.bfloat16

N_HEADS = 16
HEAD_DIM = 64
KV_HEADS = 4
HEADS_PER_KV = N_HEADS // KV_HEADS
KV_WIDTH = KV_HEADS * HEAD_DIM
ROT_DIM = HEAD_DIM // 4
ROPE_THETA = 500000.0
CMP_BLOCK = 32
SEL_BLOCK = 64
TOP_N = 8
WINDOW = 512
N_BRANCH = 3
FORCE_BONUS = float(HEADS_PER_KV + 1)
N_EXPERTS = 8
EPS = 1e-6
NEG_INF = float("-inf")

V7X_LANES = 128
V7X_SUBLANES = 8
V7X_VMEM_LIMIT_BYTES = 58 * 1024 * 1024


def _cparams(*sem):
    return pltpu.CompilerParams(dimension_semantics=sem, vmem_limit_bytes=V7X_VMEM_LIMIT_BYTES)


def _dot(a, b):
    return jnp.dot(a, b, preferred_element_type=F32)


def _dot_nt(a, b):
    return lax.dot_general(a, b, (((1,), (1,)), ((), ())), preferred_element_type=F32)


def _norm_mod(x, g, sc, sh):
    y = x * lax.rsqrt(jnp.mean(x * x, axis=-1, keepdims=True) + EPS)
    return (y * g) * (1.0 + sc) + sh


def _sigmoid(x):
    return 1.0 / (1.0 + jnp.exp(-x))


def _mod_spec(mod3, tiles_per_block):
    _, r, w = mod3.shape
    return pl.BlockSpec((1, r, w), lambda i, *_: (i // tiles_per_block, 0, 0))


def _full(shape):
    return pl.BlockSpec(shape, lambda *_: (0,) * len(shape))


def _ada_kernel(c_ref, w_ref, b_ref, o_ref):
    c = c_ref[...]
    cs = (c * _sigmoid(c)).astype(BF16)
    o_ref[0] = _dot(cs, w_ref[0].astype(BF16)) + b_ref[0]


def _ada(c, w_ada, b_ada):
    depth, d, n = w_ada.shape
    nb = c.shape[0]
    tn = 1536
    return pl.pallas_call(
        _ada_kernel,
        out_shape=jax.ShapeDtypeStruct((depth, nb, n), F32),
        grid=(depth, n // tn),
        in_specs=[
            pl.BlockSpec((nb, d), lambda i, j: (0, 0)),
            pl.BlockSpec((1, d, tn), lambda i, j: (i, 0, j)),
            pl.BlockSpec((1, 1, tn), lambda i, j: (i, 0, j)),
        ],
        out_specs=pl.BlockSpec((1, nb, tn), lambda i, j: (i, 0, j)),
        compiler_params=_cparams("arbitrary", "arbitrary"),
        name="ada",
    )(c, w_ada, b_ada.reshape(depth, 1, n))


def _rope_tables(pos, width):
    half = ROT_DIM // 2
    inv = 1.0 / (ROPE_THETA ** (jnp.arange(half, dtype=F32) * (2.0 / ROT_DIM)))
    ang = pos.astype(F32)[:, None] * inv[None, :]
    cos, sin = jnp.cos(ang), jnp.sin(ang)
    n = pos.shape[0]
    z8 = jnp.zeros((n, half), F32)
    rest0 = jnp.zeros((n, HEAD_DIM - ROT_DIM), F32)
    c = jnp.concatenate([cos, cos, rest0 + 1.0], axis=1)
    a = jnp.concatenate([-sin, z8, rest0], axis=1)
    b = jnp.concatenate([z8, sin, rest0], axis=1)
    rep = width // HEAD_DIM
    return tuple(jnp.tile(t, (1, rep)) for t in (c, a, b))


def _rope(v, c, a, b):
    w = v.shape[-1]
    half = ROT_DIM // 2
    return v * c + pltpu.roll(v, w - half, 1) * a + pltpu.roll(v, half, 1) * b


def _widen(t, width):
    return jnp.concatenate([t] * (width // t.shape[-1]), axis=1)


def _nsa_proj_kernel(x_ref, mod_ref, g_ref, w_ref, wg_ref, rc_ref, ra_ref, rb_ref, q_ref, gate_ref, *kv_refs,
                     feature_major):
    d = x_ref.shape[1]
    mod = mod_ref[0]
    h = _norm_mod(x_ref[...], g_ref[...], mod[:, d:2 * d], mod[:, 0:d]).astype(BF16)
    kvw = 2 * KV_WIDTH
    c1, a1, b1 = rc_ref[...], ra_ref[...], rb_ref[...]
    c2, a2, b2 = (_widen(t, kvw) for t in (c1, a1, b1))
    scale = HEAD_DIM ** -0.5
    hq = N_HEADS * HEAD_DIM
    for j in range(hq // kvw):
        p = _dot(h, w_ref[:, j * kvw:(j + 1) * kvw])
        q_ref[:, j * kvw:(j + 1) * kvw] = (_rope(p, c2, a2, b2) * scale).astype(BF16)
    gate_ref[...] = _sigmoid(_dot(h, wg_ref[...]))
    one = jnp.ones_like(c1)
    zero = jnp.zeros_like(c1)
    ck = jnp.concatenate([_widen(c1, KV_WIDTH), _widen(one, KV_WIDTH)], axis=1)
    ak = jnp.concatenate([_widen(a1, KV_WIDTH), _widen(zero, KV_WIDTH)], axis=1)
    bk = jnp.concatenate([_widen(b1, KV_WIDTH), _widen(zero, KV_WIDTH)], axis=1)
    cmp_rows = _dot(h, w_ref[:, hq:hq + kvw])
    sel_rows = _rope(_dot(h, w_ref[:, hq + kvw:hq + 2 * kvw]), ck, ak, bk)
    win_rows = _rope(_dot(h, w_ref[:, hq + 2 * kvw:hq + 3 * kvw]), ck, ak, bk)
    if feature_major:
        cmp_ref, cmpt_ref, selt_ref, wint_ref = kv_refs
        cmp_ref[...] = cmp_rows
        cmpt_ref[0] = cmp_rows.T
        selt_ref[0] = sel_rows.T
        wint_ref[0] = win_rows.T
    else:
        cmp_ref, sel_ref, win_ref = kv_refs
        cmp_ref[...] = cmp_rows
        sel_ref[...] = sel_rows
        win_ref[...] = win_rows


def _nsa_proj(x, mod3, tpb, g, w_main, w_gate, rope_t, rows_per_seq, tm, feature_major):
    t, d = x.shape
    nw = w_main.shape[1]
    kvw = 2 * KV_WIDTH
    hq = N_HEADS * HEAD_DIM
    nrt = rows_per_seq // tm
    rspec = pl.BlockSpec((tm, V7X_LANES), lambda i: (i % nrt, 0))
    tok = lambda w: pl.BlockSpec((tm, w), lambda i: (i, 0))
    out_shape = [jax.ShapeDtypeStruct((t, hq), BF16), jax.ShapeDtypeStruct((t, V7X_LANES), F32)]
    out_specs = [tok(hq), tok(V7X_LANES)]
    if feature_major:
        b = t // rows_per_seq
        fm = pl.BlockSpec((1, kvw, tm), lambda i: (i // nrt, 0, i % nrt))
        out_shape += [jax.ShapeDtypeStruct((t, kvw), F32)] + [jax.ShapeDtypeStruct((b, kvw, rows_per_seq), F32)] * 3
        out_specs += [tok(kvw), fm, fm, fm]
    else:
        out_shape += [jax.ShapeDtypeStruct((t, kvw), F32)] * 3
        out_specs += [tok(kvw)] * 3
    return pl.pallas_call(
        functools.partial(_nsa_proj_kernel, feature_major=feature_major),
        out_shape=tuple(out_shape),
        grid=(t // tm,),
        in_specs=[tok(d), _mod_spec(mod3, tpb), _full((1, d)), _full((d, nw)), _full((d, V7X_LANES)),
                  rspec, rspec, rspec],
        out_specs=tuple(out_specs),
        compiler_params=_cparams("arbitrary"),
        name="nsa_proj",
    )(x, mod3, g, w_main, w_gate, *rope_t)


def _compress_rows(load_rows, bdk_ref, bdv_ref, pek_ref, pev_ref, nc, perm_scr):
    half = nc // 2
    n_chunks = 2 * KV_WIDTH // V7X_LANES
    outs = []
    for c in range(n_chunks):
        pe_ref, w_ref = (pek_ref, bdk_ref) if c < n_chunks // 2 else (pev_ref, bdv_ref)
        cols = [(load_rows(c, l) + pe_ref[l:l + 1, :]).astype(BF16) for l in range(CMP_BLOCK)]
        perm_scr[...] = _dot(jnp.concatenate(cols, axis=1), w_ref[...])
        outs.append(jnp.concatenate([perm_scr[pl.ds(0, half, stride=2), :], perm_scr[pl.ds(1, half, stride=2), :]],
                                    axis=0))
    return (jnp.concatenate(outs[0:n_chunks // 2], axis=1), jnp.concatenate(outs[n_chunks // 2:], axis=1))


def _compress_kernel(r0_ref, r1_ref, r2_ref, r3_ref, bdk_ref, bdv_ref, pek_ref, pev_ref, rc_ref, ra_ref, rb_ref,
                     ck_ref, cv_ref, perm_scr):
    nc = ck_ref.shape[1]

    chunk_refs = (r0_ref, r1_ref, r2_ref, r3_ref)

    def load_rows(c, l):
        return chunk_refs[c][0, pl.ds(l, nc, stride=CMP_BLOCK), :]

    acck, accv = _compress_rows(load_rows, bdk_ref, bdv_ref, pek_ref, pev_ref, nc, perm_scr)
    ck_ref[0] = _rope(acck, rc_ref[...], ra_ref[...], rb_ref[...]).astype(BF16)
    cv_ref[0] = accv.astype(BF16)


def _cmp_order_end_pos(nc):
    n = jnp.arange(nc)
    half = nc // 2
    blk = jnp.where(n < half, 2 * n, 2 * (n - half) + 1)
    return blk * CMP_BLOCK + (CMP_BLOCK - 1)


def _cmp_end_row(nc):
    half = nc // 2
    col = lax.broadcasted_iota(jnp.int32, (1, nc), 1)
    return jnp.where(col < half, 2 * CMP_BLOCK * col + (CMP_BLOCK - 1),
                     2 * CMP_BLOCK * (col - half) + (2 * CMP_BLOCK - 1))


def _compress(rows3, cw):
    b, tk, w = rows3.shape
    nc = tk // CMP_BLOCK
    assert nc % 2 == 0 and nc * CMP_BLOCK == tk
    rt = _rope_tables(_cmp_order_end_pos(nc), KV_WIDTH)
    return pl.pallas_call(
        _compress_kernel,
        out_shape=(jax.ShapeDtypeStruct((b, nc, KV_WIDTH), BF16),) * 2,
        grid=(b,),
        in_specs=[
            *[pl.BlockSpec((1, tk, V7X_LANES), functools.partial(lambda i, c: (i, 0, c), c=c))
              for c in range(w // V7X_LANES)],
            _full(cw["bdk"].shape), _full(cw["bdv"].shape), _full(cw["pek"].shape), _full(cw["pev"].shape),
            _full(rt[0].shape), _full(rt[1].shape), _full(rt[2].shape),
        ],
        out_specs=(pl.BlockSpec((1, nc, KV_WIDTH), lambda i: (i, 0, 0)),) * 2,
        scratch_shapes=[pltpu.VMEM((nc, V7X_LANES), F32)],
        compiler_params=_cparams("arbitrary"),
        name="compress",
    )(rows3, rows3, rows3, rows3, cw["bdk"], cw["bdv"], cw["pek"], cw["pev"], *rt)


def _compress_weights(w_ck, pe_ck, w_cv, pe_cv):
    pair = V7X_LANES // HEAD_DIM
    eye = jnp.eye(pair, dtype=F32)

    def bd(w):
        return jnp.einsum("gh,lde->lgdhe", eye, w).reshape(CMP_BLOCK * V7X_LANES, V7X_LANES).astype(BF16)

    return {"bdk": bd(w_ck), "bdv": bd(w_cv),
            "pek": jnp.tile(pe_ck, (1, pair)), "pev": jnp.tile(pe_cv, (1, pair))}


def _softmax_rows(s):
    m = jnp.max(s, axis=-1, keepdims=True)
    m = jnp.where(m == NEG_INF, 0.0, m)
    e = jnp.exp(s - m)
    return e / jnp.maximum(jnp.sum(e, axis=-1, keepdims=True), 1e-30)


def _masked_softmax(s, mask):
    return _softmax_rows(jnp.where(mask, s, NEG_INF))


def _select_blocks(imp, t_col, ns):
    blk = lax.broadcasted_iota(jnp.int32, (1, ns), 1)
    cur = t_col // SEL_BLOCK
    forced = jnp.where(blk == 0, 1.0, jnp.where(blk == cur, 1.0, 0.0))
    score = jnp.where(blk <= cur, imp + FORCE_BONUS * forced, NEG_INF)
    rank = jnp.zeros(score.shape, F32)
    for i in range(ns):
        ci = score[:, i:i + 1]
        later = jnp.where(blk > i, 1.0, 0.0)
        rank = rank + jnp.where(ci > score, 1.0, jnp.where(ci == score, later, 0.0))
    return jnp.where(rank < float(min(TOP_N, ns)), 1.0, 0.0)


def _select_blocks_t(imp_t, t_row, ns):
    blk = lax.broadcasted_iota(jnp.int32, (ns, 1), 0)
    cur = t_row // SEL_BLOCK
    forced = jnp.where(blk == 0, 1.0, jnp.where(blk == cur, 1.0, 0.0))
    score = jnp.where(blk <= cur, imp_t + FORCE_BONUS * forced, NEG_INF)
    rank = jnp.zeros(score.shape, F32)
    for i in range(ns):
        ci = score[i:i + 1, :]
        later = jnp.where(blk > i, 1.0, 0.0)
        rank = rank + jnp.where(ci > score, 1.0, jnp.where(ci == score, later, 0.0))
    return jnp.where(rank < float(min(TOP_N, ns)), 1.0, 0.0)


def _stack_heads(x, g):
    base = g * HEADS_PER_KV * HEAD_DIM
    return jnp.concatenate([x[:, base + r * HEAD_DIM: base + (r + 1) * HEAD_DIM] for r in range(HEADS_PER_KV)],
                           axis=0)


def _gate_cols(gates, g, branch):
    cols = [(g * HEADS_PER_KV + r) * N_BRANCH + branch for r in range(HEADS_PER_KV)]
    return jnp.concatenate([gates[:, c:c + 1] for c in cols], axis=0)


def _sum_rows(n):
    row = lax.broadcasted_iota(jnp.int32, (HEAD_DIM, n), 0)
    return jnp.where(row == 0, 1.0, 0.0).astype(BF16)


def _pad_to(x, rows, cols):
    r, c = x.shape
    if c < cols:
        x = jnp.concatenate([x, jnp.zeros((r, cols - c), x.dtype)], axis=1)
    if r < rows:
        x = jnp.concatenate([x, jnp.zeros((rows - r, x.shape[1]), x.dtype)], axis=0)
    return x


WIN_CHUNK = V7X_LANES
MASK_BIG = 2.0 ** 100


def _attn_prompt_kernel(q_ref, gate_ref, ck_ref, cv_ref, selt_ref, wint_ref, o_ref,
                        ks_scr, vs_scr, kw_scr, vw_scr, *, tk):
    tq = q_ref.shape[0]
    s_len = selt_ref.shape[2]
    nc = ck_ref.shape[1]
    half = nc // 2
    ns = s_len // SEL_BLOCK
    qi = pl.program_id(1)
    k_pad = ks_scr.shape[2] - HEAD_DIM - ns

    @pl.when(qi == 0)
    def _():
        for c in range(s_len // tk):
            kpos = c * tk + lax.broadcasted_iota(jnp.int32, (1, tk), 1)
            brow = lax.broadcasted_iota(jnp.int32, (ns + k_pad, 1), 0)
            expand = jnp.where(kpos // SEL_BLOCK == brow, 1.0, 0.0).astype(BF16)
            for g in range(KV_HEADS):
                k_g = selt_ref[0, g * HEAD_DIM:(g + 1) * HEAD_DIM, c * tk:(c + 1) * tk].astype(BF16)
                ks_scr[c, g] = jnp.concatenate([k_g, expand], axis=0)
                v_g = selt_ref[0, KV_WIDTH + g * HEAD_DIM:KV_WIDTH + (g + 1) * HEAD_DIM, c * tk:(c + 1) * tk]
                vs_scr[c, g] = jnp.concatenate([v_g.astype(BF16), _sum_rows(tk)], axis=0)
        for c in range(s_len // WIN_CHUNK):
            kw_scr[c] = wint_ref[0, 0:KV_WIDTH, c * WIN_CHUNK:(c + 1) * WIN_CHUNK].astype(BF16)
            vw_scr[c] = wint_ref[0, KV_WIDTH:2 * KV_WIDTH, c * WIN_CHUNK:(c + 1) * WIN_CHUNK].astype(BF16)

    q0 = qi * tq
    t_tile = q0 + lax.broadcasted_iota(jnp.int32, (tq, 1), 0)
    t_row = q0 + lax.broadcasted_iota(jnp.int32, (1, tq), 1)
    t_col4 = jnp.concatenate([t_tile] * HEADS_PER_KV, axis=0)
    q_all = q_ref[...]
    gates = gate_ref[...]
    c_end = _cmp_end_row(nc)
    last = q0 // tk
    kpos_last = last * tk + lax.broadcasted_iota(jnp.int32, (1, tk), 1)
    causal = jnp.where(kpos_last <= t_tile, 0.0, -MASK_BIG)
    causal4 = jnp.concatenate([causal] * HEADS_PER_KV, axis=0)

    wlen = min(WINDOW + tq, s_len)
    n_wc = wlen // WIN_CHUNK
    w0 = jnp.clip(q0 + tq - wlen, 0, s_len - wlen)
    wc0 = w0 // WIN_CHUNK
    wpos = w0 + lax.broadcasted_iota(jnp.int32, (1, wlen), 1)
    wbias = jnp.where(wpos <= t_tile, jnp.where(wpos >= t_tile - WINDOW, 0.0, NEG_INF), NEG_INF)
    wbias4 = jnp.concatenate([wbias] * HEADS_PER_KV, axis=0)

    for g in range(KV_HEADS):
        rows_g = slice(g * HEAD_DIM, (g + 1) * HEAD_DIM)
        q4 = _stack_heads(q_all, g)
        p_c = _masked_softmax(_dot_nt(q4, ck_ref[0, :, rows_g]), c_end <= t_col4)
        o_c = _dot(p_c.astype(BF16), cv_ref[0, :, rows_g])
        psum = p_c[0:tq]
        for r in range(1, HEADS_PER_KV):
            psum = psum + p_c[r * tq:(r + 1) * tq]
        psum_t = _pad_to(psum, tq, V7X_LANES).T
        imp_t = psum_t[0:half] + psum_t[half:nc]
        sel_t = _select_blocks_t(imp_t, t_row, ns)
        sel = _pad_to(sel_t, V7X_LANES, tq).T[:, 0:ns + k_pad]
        unchosen = ((sel - 1.0) * MASK_BIG).astype(BF16)
        q_ext = jnp.concatenate([q4, jnp.concatenate([unchosen] * HEADS_PER_KV, axis=0)], axis=1)

        kw = jnp.concatenate([kw_scr[wc0 + i, rows_g, :] for i in range(n_wc)], axis=1)
        vw = jnp.concatenate([vw_scr[wc0 + i, rows_g, :] for i in range(n_wc)], axis=1)
        vw = jnp.concatenate([vw, _sum_rows(wlen)], axis=0)

        def attend(c, carry, bias):
            m, acc = carry
            s = _dot(q_ext, ks_scr[c, g])
            if bias is not None:
                s = s + bias
            m_new = jnp.maximum(m, jnp.max(s, axis=-1, keepdims=True))
            p = jnp.exp(s - m_new)
            acc = jnp.exp(m - m_new) * acc + _dot_nt(p.astype(BF16), vs_scr[c, g])
            return m_new, acc

        rows = HEADS_PER_KV * tq
        init = (jnp.full((rows, 1), NEG_INF, F32), jnp.zeros((rows, 2 * HEAD_DIM), F32))
        carry = lax.fori_loop(0, last, lambda c, carry: attend(c, carry, None), init)
        _, acc_s = attend(last, carry, causal4)
        o_s = acc_s[:, 0:HEAD_DIM] / jnp.maximum(acc_s[:, HEAD_DIM:HEAD_DIM + 1], 1e-30)

        s_w = _dot(q4, kw) + wbias4
        m_w = jnp.max(s_w, axis=-1, keepdims=True)
        p_w = jnp.exp(s_w - jnp.where(m_w == NEG_INF, 0.0, m_w))
        acc_w = _dot_nt(p_w.astype(BF16), vw)
        o_w = acc_w[:, 0:HEAD_DIM] / jnp.maximum(acc_w[:, HEAD_DIM:HEAD_DIM + 1], 1e-30)

        o = _gate_cols(gates, g, 0) * o_c + _gate_cols(gates, g, 1) * o_s + _gate_cols(gates, g, 2) * o_w
        for r in range(HEADS_PER_KV):
            head = g * HEADS_PER_KV + r
            o_ref[:, head * HEAD_DIM:(head + 1) * HEAD_DIM] = o[r * tq:(r + 1) * tq].astype(BF16)


def _attn_prompt(q, gates, ck, cv, selt, wint, tq):
    b, kvw, s_len = selt.shape
    nc = ck.shape[1]
    hq = N_HEADS * HEAD_DIM
    nq = s_len // tq
    tk = min(512, s_len)
    assert s_len % tk == 0 and s_len % tq == 0 and tk % tq == 0 and tq % V7X_LANES == 0 and nc <= V7X_LANES
    assert min(WINDOW + tq, s_len) % WIN_CHUNK == 0 and HEAD_DIM + s_len // SEL_BLOCK <= V7X_LANES
    return pl.pallas_call(
        functools.partial(_attn_prompt_kernel, tk=tk),
        out_shape=jax.ShapeDtypeStruct((b * s_len, hq), BF16),
        grid=(b, nq),
        in_specs=[
            pl.BlockSpec((tq, hq), lambda i, j: (i * nq + j, 0)),
            pl.BlockSpec((tq, V7X_LANES), lambda i, j: (i * nq + j, 0)),
            pl.BlockSpec((1, nc, KV_WIDTH), lambda i, j: (i, 0, 0)),
            pl.BlockSpec((1, nc, KV_WIDTH), lambda i, j: (i, 0, 0)),
            pl.BlockSpec((1, kvw, s_len), lambda i, j: (i, 0, 0)),
            pl.BlockSpec((1, kvw, s_len), lambda i, j: (i, 0, 0)),
        ],
        out_specs=pl.BlockSpec((tq, hq), lambda i, j: (i * nq + j, 0)),
        scratch_shapes=[
            pltpu.VMEM((s_len // tk, KV_HEADS, V7X_LANES, tk), BF16),
            pltpu.VMEM((s_len // tk, KV_HEADS, 2 * HEAD_DIM, tk), BF16),
            pltpu.VMEM((s_len // WIN_CHUNK, KV_WIDTH, WIN_CHUNK), BF16),
            pltpu.VMEM((s_len // WIN_CHUNK, KV_WIDTH, WIN_CHUNK), BF16),
        ],
        compiler_params=_cparams("arbitrary", "arbitrary"),
        name="attn_prompt",
    )(q, gates, ck, cv, selt, wint)


PAGES_PER_STEP = 16
CMP_PITCH = CMP_BLOCK + V7X_SUBLANES
DECODE_TOK_PAD = V7X_SUBLANES


def _block_diag_q(q):
    parts = []
    for g in range(KV_HEADS):
        q4 = _stack_heads(q, g)
        z = jnp.zeros_like(q4)
        parts.append(jnp.concatenate([q4 if gg == g else z for gg in range(KV_HEADS)], axis=1))
    return jnp.concatenate(parts, axis=0)


def _attn_decode_kernel(pt_ref, q_ref, gate_ref, nsel_ref, nwin_ref, wbuf_ref, bdk_ref, bdv_ref, pek_ref, pev_ref,
                        rc_ref, ra_ref, rb_ref, *rest, page, n_pages):
    pps = PAGES_PER_STEP
    cpages = rest[:pps]
    spages = rest[pps:2 * pps]
    o_ref = rest[2 * pps]
    cmp_scr, s_scr, vs_scr, qbd_scr, perm_scr = rest[2 * pps + 1:]
    blocks_per_page = page // CMP_BLOCK
    del pt_ref
    j = pl.program_id(1)
    tq = q_ref.shape[1]
    past_len = n_pages * page
    n_lane_chunks = 2 * KV_WIDTH // V7X_LANES

    @pl.when(j == 0)
    def _():
        qbd_scr[...] = _block_diag_q(q_ref[0].astype(F32)).astype(BF16)

    qbd = qbd_scr[...]
    for p in range(pps):
        for c in range(n_lane_chunks):
            rows_pc = cpages[p][0, 0, c * V7X_LANES:(c + 1) * V7X_LANES, :].T
            for n in range(blocks_per_page):
                r0 = pl.multiple_of(((j * pps + p) * blocks_per_page + n) * CMP_PITCH, V7X_SUBLANES)
                cmp_scr[c, pl.ds(r0, CMP_BLOCK), :] = rows_pc[n * CMP_BLOCK:(n + 1) * CMP_BLOCK]
    k_step = jnp.concatenate([spages[p][0, 0, 0:KV_WIDTH, :] for p in range(pps)], axis=1).astype(BF16)
    s_scr[j] = _dot(qbd, k_step)
    vs_scr[j] = jnp.concatenate([spages[p][0, 0, KV_WIDTH:2 * KV_WIDTH, :] for p in range(pps)],
                                axis=1).astype(BF16)

    @pl.when(j == pl.num_programs(1) - 1)
    def _():
        tok = lax.broadcasted_iota(jnp.int32, (tq, 1), 0)
        t_rows = past_len + jnp.concatenate([tok] * N_HEADS, axis=0)
        nsel = _pad_to(nsel_ref[0], page, 2 * KV_WIDTH)
        s_new = _dot_nt(qbd, nsel[:, 0:KV_WIDTH].astype(BF16))

        nc = past_len // CMP_BLOCK
        half = nc // 2

        def load_rows(c, l):
            return cmp_scr[c, pl.ds(l, nc, stride=CMP_PITCH), :]

        acck, accv = _compress_rows(load_rows, bdk_ref, bdv_ref, pek_ref, pev_ref, nc, perm_scr)
        ck = _rope(acck, rc_ref[...], ra_ref[...], rb_ref[...]).astype(BF16)
        p_c = _masked_softmax(_dot_nt(qbd, ck), _cmp_end_row(nc) <= t_rows)
        o_c = _dot(p_c.astype(BF16), accv.astype(BF16))

        ns = n_pages * page // SEL_BLOCK + 1
        grp = HEADS_PER_KV * tq
        psum = []
        for g in range(KV_HEADS):
            pg_ = p_c[g * grp:g * grp + tq]
            for r in range(1, HEADS_PER_KV):
                pg_ = pg_ + p_c[g * grp + r * tq:g * grp + (r + 1) * tq]
            psum.append(pg_)
        psum = jnp.concatenate(psum, axis=0)
        imp = jnp.concatenate([psum[:, 0:half] + psum[:, half:nc], jnp.zeros((KV_HEADS * tq, 1), F32)], axis=1)
        selm = _select_blocks(imp, past_len + jnp.concatenate([tok] * KV_HEADS, axis=0), ns)
        selm_rows = jnp.concatenate(
            [selm[g * tq:(g + 1) * tq] for g in range(KV_HEADS) for _ in range(HEADS_PER_KV)], axis=0)
        nkeys = (n_pages + 1) * page
        kpos = lax.broadcasted_iota(jnp.int32, (1, nkeys), 1)
        brow = lax.broadcasted_iota(jnp.int32, (ns, 1), 0)
        expand = jnp.where(kpos // SEL_BLOCK == brow, 1.0, 0.0).astype(BF16)
        chosen = _dot(selm_rows.astype(BF16), expand)
        n_steps = n_pages // pps
        s_all = jnp.concatenate([s_scr[i] for i in range(n_steps)] + [s_new], axis=1)
        s_all = jnp.where(kpos <= t_rows, jnp.where(chosen > 0.5, s_all, NEG_INF), NEG_INF)
        p_s = _softmax_rows(s_all).astype(BF16)
        vt_all = jnp.concatenate([vs_scr[i] for i in range(n_steps)], axis=1)
        o_s = _dot_nt(p_s[:, 0:past_len], vt_all) + _dot(p_s[:, past_len:nkeys], nsel[:, KV_WIDTH:].astype(BF16))

        wb = wbuf_ref.shape[3]
        nwin = _pad_to(nwin_ref[0], page, 2 * KV_WIDTH)
        s_w = jnp.concatenate([_dot(qbd, wbuf_ref[0, 0, 0:KV_WIDTH, :].astype(BF16)),
                               _dot_nt(qbd, nwin[:, 0:KV_WIDTH].astype(BF16))], axis=1)
        wpos = past_len - wb + lax.broadcasted_iota(jnp.int32, (1, wb + page), 1)
        s_w = jnp.where(wpos <= t_rows, jnp.where(wpos >= t_rows - WINDOW, s_w, NEG_INF), NEG_INF)
        p_w = _softmax_rows(s_w).astype(BF16)
        o_w = (_dot_nt(p_w[:, 0:wb], wbuf_ref[0, 0, KV_WIDTH:2 * KV_WIDTH, :].astype(BF16))
               + _dot(p_w[:, wb:wb + page], nwin[:, KV_WIDTH:].astype(BF16)))

        gates = gate_ref[0]
        gcol = lambda br: jnp.concatenate([_gate_cols(gates, g, br) for g in range(KV_HEADS)], axis=0)
        o = gcol(0) * o_c + gcol(1) * o_s + gcol(2) * o_w
        for g in range(KV_HEADS):
            for r in range(HEADS_PER_KV):
                col = (g * HEADS_PER_KV + r) * HEAD_DIM
                o_ref[0, :, col:col + HEAD_DIM] = o[g * grp + r * tq:g * grp + (r + 1) * tq,
                                                    g * HEAD_DIM:(g + 1) * HEAD_DIM]


def _attn_decode(page_table, q3, gates3, nsel3, nwin3, cache_cmp_t, cache_sel_t, cache_win_t, layer, cw):
    b, tq, hq = q3.shape
    n_pages = page_table.shape[1]
    page = cache_cmp_t.shape[3]
    kvw = 2 * KV_WIDTH
    pps = PAGES_PER_STEP
    assert n_pages % pps == 0 and (n_pages * page) % (2 * CMP_BLOCK) == 0 and tq < CMP_BLOCK
    assert tq % V7X_SUBLANES == 0 and page == V7X_LANES
    nc = n_pages * page // CMP_BLOCK
    rt = _rope_tables(_cmp_order_end_pos(nc), KV_WIDTH)
    wb = cache_win_t.shape[3]
    full = lambda shp: pl.BlockSpec(shp, lambda *_: (0,) * len(shp))
    per_seq = lambda shp: pl.BlockSpec((1,) + shp, lambda i, j, pt: (i,) + (0,) * len(shp))
    page_spec = lambda p: pl.BlockSpec((1, 1, kvw, page), lambda i, j, pt: (layer, pt[i, j * pps + p], 0, 0))
    rows = N_HEADS * tq
    grid_spec = pltpu.PrefetchScalarGridSpec(
        num_scalar_prefetch=1,
        grid=(b, n_pages // pps),
        in_specs=[
            per_seq((tq, hq)), per_seq((tq, V7X_LANES)), per_seq((tq, kvw)), per_seq((tq, kvw)),
            pl.BlockSpec((1, 1, kvw, wb), lambda i, j, pt: (layer, i, 0, 0)),
            full(cw["bdk"].shape), full(cw["bdv"].shape), full(cw["pek"].shape), full(cw["pev"].shape),
            full(rt[0].shape), full(rt[1].shape), full(rt[2].shape),
            *[page_spec(p) for p in range(pps)],
            *[page_spec(p) for p in range(pps)],
        ],
        out_specs=per_seq((tq, hq)),
        scratch_shapes=[
            pltpu.VMEM((kvw // V7X_LANES, nc * CMP_PITCH, V7X_LANES), F32),
            pltpu.VMEM((n_pages // pps, rows, pps * page), F32),
            pltpu.VMEM((n_pages // pps, KV_WIDTH, pps * page), BF16),
            pltpu.VMEM((rows, KV_WIDTH), BF16),
            pltpu.VMEM((nc, V7X_LANES), F32),
        ],
    )
    return pl.pallas_call(
        functools.partial(_attn_decode_kernel, page=page, n_pages=n_pages),
        out_shape=jax.ShapeDtypeStruct((b, tq, hq), F32),
        grid_spec=grid_spec,
        compiler_params=_cparams("arbitrary", "arbitrary"),
        name="attn_decode",
    )(page_table, q3, gates3, nsel3, nwin3, cache_win_t, cw["bdk"], cw["bdv"], cw["pek"], cw["pev"], *rt,
      *([cache_cmp_t] * pps), *([cache_sel_t] * pps))


def _out_proj_kernel(o_ref, w_ref, x_ref, mod_ref, xo_ref):
    d = x_ref.shape[1]
    xo_ref[...] = x_ref[...] + mod_ref[0][:, 2 * d:3 * d] * _dot(o_ref[...], w_ref[...])


def _out_proj(o, w_bf16, x, mod3, tpb, tm):
    t, d = x.shape
    k = o.shape[1]
    return pl.pallas_call(
        _out_proj_kernel,
        out_shape=jax.ShapeDtypeStruct((t, d), F32),
        grid=(t // tm,),
        in_specs=[
            pl.BlockSpec((tm, k), lambda i: (i, 0)),
            pl.BlockSpec((k, d), lambda i: (0, 0)),
            pl.BlockSpec((tm, d), lambda i: (i, 0)),
            _mod_spec(mod3, tpb),
        ],
        out_specs=pl.BlockSpec((tm, d), lambda i: (i, 0)),
        compiler_params=_cparams("arbitrary"),
        name="out_proj",
    )(o, w_bf16, x, mod3)


FFN_FF_CHUNK = 1792
FFN_ROW_ARM = 512


def _ffn_kernel(x_ref, mod_ref, g_ref, wg_ref, wu_ref, wd_ref, o_ref, h_scr):
    d = x_ref.shape[1]
    f = pl.program_id(1)

    @pl.when(f == 0)
    def _():
        mod = mod_ref[0]
        h_scr[...] = _norm_mod(x_ref[...], g_ref[...], mod[:, 4 * d:5 * d], mod[:, 3 * d:4 * d]).astype(BF16)
        o_ref[...] = jnp.zeros(o_ref.shape, F32)

    rows = min(FFN_ROW_ARM, x_ref.shape[0])

    def arm(k, _):
        r0 = pl.multiple_of(k * rows, rows)
        h = h_scr[pl.ds(r0, rows), :]
        a = _dot(h, wg_ref[0])
        act = (a * _sigmoid(a) * _dot(h, wu_ref[0])).astype(BF16)
        o_ref[pl.ds(r0, rows), :] += _dot(act, wd_ref[0])
        return 0

    lax.fori_loop(0, x_ref.shape[0] // rows, arm, 0)

    @pl.when(f == pl.num_programs(1) - 1)
    def _():
        o_ref[...] = x_ref[...] + mod_ref[0][:, 5 * d:6 * d] * o_ref[...]


def _ffn(x, mod3, tpb, g, w_gu_all, w_down_all, layer, tm, tf):
    t, d = x.shape
    ff = w_down_all.shape[1]
    nf = ff // tf
    return pl.pallas_call(
        _ffn_kernel,
        out_shape=jax.ShapeDtypeStruct((t, d), F32),
        grid=(t // tm, nf),
        in_specs=[
            pl.BlockSpec((tm, d), lambda i, f: (i, 0)),
            _mod_spec(mod3, tpb),
            pl.BlockSpec((1, d), lambda i, f: (0, 0)),
            pl.BlockSpec((1, d, tf), lambda i, f: (layer, 0, f)),
            pl.BlockSpec((1, d, tf), lambda i, f: (layer, 0, nf + f)),
            pl.BlockSpec((1, tf, d), lambda i, f: (layer, f, 0)),
        ],
        out_specs=pl.BlockSpec((tm, d), lambda i, f: (i, 0)),
        scratch_shapes=[pltpu.VMEM((tm, d), BF16)],
        compiler_params=_cparams("arbitrary", "arbitrary"),
        name="ffn",
    )(x, mod3, g, w_gu_all, w_gu_all, w_down_all)


def _conv_in(x_ref, mod_ref, g_ref, win_ref):
    d = x_ref.shape[1]
    mod = mod_ref[0]
    h = _norm_mod(x_ref[...], g_ref[...], mod[:, d:2 * d], mod[:, 0:d]).astype(BF16)
    bg = _dot(h, win_ref[:, 0:d])
    u = _dot(h, win_ref[:, d:2 * d]) * _dot(h, win_ref[:, 2 * d:3 * d])
    return bg, u


def _conv_out(bg, conv, x_ref, mod_ref, wout_ref, xo_ref):
    d = x_ref.shape[1]
    y = _dot((bg * conv).astype(BF16), wout_ref[...])
    xo_ref[...] = x_ref[...] + mod_ref[0][:, 2 * d:3 * d] * y


def _conv_seq_kernel(x_ref, mod_ref, g_ref, win_ref, wc_ref, wout_ref, st_ref, xo_ref, nst_ref, carry_scr):
    tm = x_ref.shape[0]
    bg, u = _conv_in(x_ref, mod_ref, g_ref, win_ref)

    @pl.when(pl.program_id(1) == 0)
    def _():
        carry_scr[6:8, :] = st_ref[0]

    ue = jnp.concatenate([carry_scr[...], u], axis=0)
    conv = wc_ref[0:1, :] * ue[6:6 + tm] + wc_ref[1:2, :] * ue[7:7 + tm] + wc_ref[2:3, :] * u
    carry_scr[...] = u[tm - 8:tm]
    nst_ref[0] = u[tm - 2:tm]
    _conv_out(bg, conv, x_ref, mod_ref, wout_ref, xo_ref)


def _conv_tok_kernel(x_ref, mod_ref, g_ref, win_ref, wc_ref, wout_ref, p1_ref, p2_ref, xo_ref, u_ref, *, t_seq):
    tm = x_ref.shape[0]
    bg, u = _conv_in(x_ref, mod_ref, g_ref, win_ref)
    row = lax.broadcasted_iota(jnp.int32, (tm, 1), 0) % t_seq
    prev1 = jnp.where(row >= 1, pltpu.roll(u, 1, 0), p1_ref[...])
    prev2 = jnp.where(row >= 2, pltpu.roll(u, 2, 0), p2_ref[...])
    conv = wc_ref[0:1, :] * prev2 + wc_ref[1:2, :] * prev1 + wc_ref[2:3, :] * u
    u_ref[...] = u
    _conv_out(bg, conv, x_ref, mod_ref, wout_ref, xo_ref)


def _conv_mixer(x, mod3, tpb, g, w_in, w_conv, w_out, state, b, t_seq, tm):
    t, d = x.shape
    weights = [_full((1, d)), _full(w_in.shape), _full(w_conv.shape), _full(w_out.shape)]
    if state is None:
        state = jnp.zeros((b, 2, d), F32)
    if t_seq >= tm:
        nt = t_seq // tm
        xo, nst = pl.pallas_call(
            _conv_seq_kernel,
            out_shape=(jax.ShapeDtypeStruct((t, d), F32), jax.ShapeDtypeStruct((b, 2, d), F32)),
            grid=(b, nt),
            in_specs=[pl.BlockSpec((tm, d), lambda i, j: (i * nt + j, 0)),
                      pl.BlockSpec((1, 1, mod3.shape[-1]), lambda i, j: (i, 0, 0)),
                      *weights,
                      pl.BlockSpec((1, 2, d), lambda i, j: (i, 0, 0))],
            out_specs=(pl.BlockSpec((tm, d), lambda i, j: (i * nt + j, 0)),
                       pl.BlockSpec((1, 2, d), lambda i, j: (i, 0, 0))),
            scratch_shapes=[pltpu.VMEM((8, d), F32)],
            compiler_params=_cparams("arbitrary", "arbitrary"),
            name="conv_seq",
        )(x, mod3, g, w_in, w_conv, w_out, state)
        return xo, nst
    assert t == tm and t_seq >= 2
    p1 = jnp.zeros((b, t_seq, d), F32).at[:, 0].set(state[:, 1]).reshape(t, d)
    p2 = jnp.zeros((b, t_seq, d), F32).at[:, 0].set(state[:, 0]).at[:, 1].set(state[:, 1]).reshape(t, d)
    xo, u = pl.pallas_call(
        functools.partial(_conv_tok_kernel, t_seq=t_seq),
        out_shape=(jax.ShapeDtypeStruct((t, d), F32), jax.ShapeDtypeStruct((t, d), F32)),
        grid=(1,),
        in_specs=[_full((t, d)), _full(mod3.shape), *weights, _full((t, d)), _full((t, d))],
        out_specs=(_full((t, d)), _full((t, d))),
        compiler_params=_cparams("arbitrary"),
        name="conv_tok",
    )(x, mod3, g, w_in, w_conv, w_out, p1, p2)
    return xo, u.reshape(b, t_seq, d)[:, t_seq - 2:]


MOE_ROW_UNIT = 128
MOE_BOUNDS_ROWS = 16
MOE_FF_CHUNK = 1792
MOE_ARM_MAX = 512
MOE_ARM_MIN = 128
MOE_GATHER_BLOCKS = 4
MOE_SCATTER_UNITS = 2


def _split_bf16(x):
    hi = x.astype(BF16)
    return hi, (x - hi.astype(F32)).astype(BF16)


def _moe_route_kernel(x_ref, mod_ref, g_ref, wr_ref, br_ref, h_ref, pos_ref, comb_ref, post_ref, bnd_ref, *, blk):
    tm, d = x_ref.shape
    nb = tm // blk
    mod = mod_ref[0]
    h32 = _norm_mod(x_ref[...], g_ref[...], mod[:, 4 * d:5 * d], mod[:, 3 * d:4 * d])
    h_ref[...] = h32.astype(BF16)
    lane = lax.broadcasted_iota(jnp.int32, (1, V7X_LANES), 1)
    h_hi, h_lo = _split_bf16(h32)
    w_hi, w_lo = _split_bf16(wr_ref[...])
    logits = _dot(h_hi, w_hi) + (_dot(h_hi, w_lo) + _dot(h_lo, w_hi)) + br_ref[...]
    logits = jnp.where(lane < N_EXPERTS, logits, NEG_INF)
    m1 = jnp.max(logits, axis=-1, keepdims=True)
    i1 = jnp.min(jnp.where(logits == m1, lane, V7X_LANES), axis=-1, keepdims=True)
    rest = jnp.where(lane == i1, NEG_INF, logits)
    m2 = jnp.max(rest, axis=-1, keepdims=True)
    i2 = jnp.min(jnp.where(rest == m2, lane, V7X_LANES), axis=-1, keepdims=True)
    e2 = jnp.exp(m2 - m1)
    den = 1.0 + e2
    comb_ref[...] = jnp.where(lane == i1, 1.0 / den, jnp.where(lane == i2, e2 / den, 0.0))
    mask = jnp.where(lane == i1, 1.0, jnp.where(lane == i2, 1.0, 0.0))
    r = lax.broadcasted_iota(jnp.int32, (blk, blk), 0)
    c = lax.broadcasted_iota(jnp.int32, (blk, blk), 1)
    lower = jnp.where(c < r, 1.0, 0.0).astype(BF16)
    upper = jnp.where(r < c, 1.0, 0.0).astype(BF16)
    carry = jnp.zeros((1, V7X_LANES), F32)
    bnd_ref[0] = jnp.zeros((MOE_BOUNDS_ROWS, V7X_LANES), F32)
    for k in range(nb):
        mb = mask[k * blk:(k + 1) * blk]
        pos_ref[k * blk:(k + 1) * blk, :] = jnp.where(mb > 0.5, _dot(lower, mb.astype(BF16)) + carry, -1.0)
        carry = carry + jnp.sum(mb, axis=0, keepdims=True)
        bnd_ref[0, k + 1:k + 2, :] = carry
    eye = jnp.where(lax.broadcasted_iota(jnp.int32, (V7X_LANES, V7X_LANES), 0)
                    == lax.broadcasted_iota(jnp.int32, (V7X_LANES, V7X_LANES), 1), 1.0, 0.0).astype(BF16)
    mask_t = _dot_nt(eye, mask.astype(BF16))[0:N_EXPERTS]
    carry_t = jnp.zeros((N_EXPERTS, 1), F32)
    for k in range(nb):
        mb = mask_t[:, k * blk:(k + 1) * blk]
        post_ref[0, k] = jnp.where(mb > 0.5, _dot(mb.astype(BF16), upper) + carry_t, -1.0)
        carry_t = carry_t + jnp.sum(mb, axis=1, keepdims=True)


def _moe_route(x, mod3, tpb, g, w_router, b_router, tm, blk):
    t, d = x.shape
    ne = w_router.shape[1]
    nt = t // tm
    nb = tm // blk
    assert nb + 1 <= MOE_BOUNDS_ROWS
    wr = jnp.pad(w_router, ((0, 0), (0, V7X_LANES - ne)))
    br = jnp.pad(b_router, (0, V7X_LANES - ne)).reshape(1, V7X_LANES)
    tok = lambda w: pl.BlockSpec((tm, w), lambda i: (i, 0))
    return pl.pallas_call(
        functools.partial(_moe_route_kernel, blk=blk),
        out_shape=(
            jax.ShapeDtypeStruct((t, d), BF16),
            jax.ShapeDtypeStruct((t, V7X_LANES), F32),
            jax.ShapeDtypeStruct((t, V7X_LANES), F32),
            jax.ShapeDtypeStruct((nt, nb, N_EXPERTS, blk), F32),
            jax.ShapeDtypeStruct((nt, MOE_BOUNDS_ROWS, V7X_LANES), F32),
        ),
        grid=(nt,),
        in_specs=[tok(d), _mod_spec(mod3, tpb), _full((1, d)), _full((d, V7X_LANES)), _full((1, V7X_LANES))],
        out_specs=(tok(d), tok(V7X_LANES), tok(V7X_LANES),
                   pl.BlockSpec((1, nb, N_EXPERTS, blk), lambda i: (i, 0, 0, 0)),
                   pl.BlockSpec((1, MOE_BOUNDS_ROWS, V7X_LANES), lambda i: (i, 0, 0))),
        compiler_params=_cparams("arbitrary"),
        name="moe_route",
    )(x, mod3, g, wr, br)


def _moe_expert_kernel(bnd_ref, h_ref, pos_ref, comb_ref, post_ref, wg_ref, wu_ref, wd_ref, o_ref,
                       hc_scr, acc_scr, posc_scr, combc_scr, *, blk):
    tm, d = h_ref.shape
    nb = tm // blk
    unit = MOE_ROW_UNIT
    i = pl.program_id(0)
    e = pl.program_id(1)
    f = pl.program_id(2)
    nf = pl.num_programs(2)
    base = (i * N_EXPERTS + e) * MOE_BOUNDS_ROWS
    cnt = bnd_ref[base + nb]
    n_units = (cnt + unit - 1) // unit

    def span(u):
        start = u * unit
        stop = jnp.minimum(start + unit, cnt)
        lo = jnp.int32(0)
        hi = jnp.int32(0)
        for k in range(1, nb):
            bk = bnd_ref[base + k]
            lo = lo + (bk <= start).astype(jnp.int32)
            hi = hi + (bk < stop).astype(jnp.int32)
        return lo, hi + 1

    @pl.when((i == 0) & (e == 0) & (f == 0))
    def _():
        acc_scr[...] = jnp.zeros(acc_scr.shape, F32)

    @pl.when((e == 0) & (f == 0))
    def _():
        o_ref[...] = jnp.zeros(o_ref.shape, F32)

    gw = min(MOE_GATHER_BLOCKS, nb)

    @pl.when(f == 0)
    def _():
        def per_unit(u, _):
            lo, hi = span(u)
            r0 = pl.multiple_of(u * unit, unit)
            want = (r0 + lax.broadcasted_iota(jnp.int32, (unit, 1), 0)).astype(F32)
            lo_w = jnp.minimum(lo, nb - gw)
            pos_t = jnp.concatenate([post_ref[0, lo_w + k, pl.ds(e, 1), :] for k in range(gw)], axis=1)
            onehot = jnp.where(pos_t == want, 1.0, 0.0).astype(BF16)
            t0 = pl.multiple_of(lo_w * blk, blk)
            hc_scr[pl.ds(r0, unit), :] = _dot(onehot, h_ref[pl.ds(t0, gw * blk), :]).astype(BF16)

            def per_blk(k, _):
                t1 = pl.multiple_of(k * blk, blk)
                oh = jnp.where(post_ref[0, k, pl.ds(e, 1), :] == want, 1.0, 0.0).astype(BF16)
                hc_scr[pl.ds(r0, unit), :] += _dot(oh, h_ref[pl.ds(t1, blk), :]).astype(BF16)
                return 0

            lax.fori_loop(lo_w + gw, hi, per_blk, 0)
            acc_scr[pl.ds(r0, unit), :] = jnp.zeros((unit, d), F32)
            return 0

        lax.fori_loop(0, n_units, per_unit, 0)

    def ffn_rows(r0, m):
        hc = hc_scr[pl.ds(r0, m), :]
        a = _dot(hc, wg_ref[0, 0])
        act = (a * _sigmoid(a) * _dot(hc, wu_ref[0, 0])).astype(BF16)
        acc_scr[pl.ds(r0, m), :] += _dot(act, wd_ref[0, 0])

    big = min(MOE_ARM_MAX, tm)
    n_min = (cnt + MOE_ARM_MIN - 1) // MOE_ARM_MIN
    per_big = big // MOE_ARM_MIN
    n_big = n_min // per_big
    rem = n_min - per_big * n_big

    def big_arm(k, _):
        ffn_rows(pl.multiple_of(k * big, big), big)
        return 0

    lax.fori_loop(0, n_big, big_arm, 0)
    r_arm = n_big * big
    size = big // 2
    while size >= MOE_ARM_MIN:
        bit = size // MOE_ARM_MIN
        if size <= tm:
            taken = (rem // bit) % 2 == 1

            @pl.when(taken)
            def _(r_arm=r_arm, size=size):
                ffn_rows(pl.multiple_of(r_arm, size), size)

            r_arm = r_arm + jnp.where(taken, size, 0)
        size //= 2

    @pl.when(f == nf - 1)
    def _():
        lane = lax.broadcasted_iota(jnp.int32, (1, V7X_LANES), 1)
        posc_scr[...] = jnp.sum(jnp.where(lane == e, pos_ref[...], 0.0), axis=-1, keepdims=True)
        combc_scr[...] = jnp.sum(jnp.where(lane == e, comb_ref[...], 0.0), axis=-1, keepdims=True)

        sw = min(MOE_SCATTER_UNITS * unit, tm)

        def per_blk(k, _):
            start = bnd_ref[base + k]
            stop = bnd_ref[base + k + 1]
            t0 = pl.multiple_of(k * blk, blk)
            r0 = pl.multiple_of(jnp.minimum((start // unit) * unit, tm - sw), unit)
            pc = posc_scr[pl.ds(t0, blk), :]
            cc = combc_scr[pl.ds(t0, blk), :]
            want = (r0 + lax.broadcasted_iota(jnp.int32, (1, sw), 1)).astype(F32)
            onehot = jnp.where(pc == want, 1.0, 0.0).astype(BF16)
            o_ref[pl.ds(t0, blk), :] += cc * _dot(onehot, acc_scr[pl.ds(r0, sw), :].astype(BF16))

            def per_unit(u, _):
                r1 = pl.multiple_of(u * unit, unit)
                want1 = (r1 + lax.broadcasted_iota(jnp.int32, (1, unit), 1)).astype(F32)
                oh = jnp.where(pc == want1, 1.0, 0.0).astype(BF16)
                o_ref[pl.ds(t0, blk), :] += cc * _dot(oh, acc_scr[pl.ds(r1, unit), :].astype(BF16))
                return 0

            lax.fori_loop((r0 + sw) // unit, (stop + unit - 1) // unit, per_unit, 0)
            return 0

        lax.fori_loop(0, nb, per_blk, 0)


def _moe_experts(h, pos, comb, post, bounds, w_gu_all, w_down_all, layer, tm, tf, blk):
    t, d = h.shape
    ne, ff = w_down_all.shape[1], w_down_all.shape[2]
    nf = ff // tf
    nt = t // tm
    nb = tm // blk
    bnd = jnp.transpose(bounds[:, :, :ne], (0, 2, 1)).astype(jnp.int32).reshape(nt * ne * MOE_BOUNDS_ROWS)
    once = pl.Buffered(1)
    tok = lambda w: pl.BlockSpec((tm, w), lambda i, e, f, b_: (i, 0), pipeline_mode=once)
    grid_spec = pltpu.PrefetchScalarGridSpec(
        num_scalar_prefetch=1,
        grid=(nt, ne, nf),
        in_specs=[
            tok(d), tok(V7X_LANES), tok(V7X_LANES),
            pl.BlockSpec((1, nb, ne, blk), lambda i, e, f, b_: (i, 0, 0, 0)),
            pl.BlockSpec((1, 1, d, tf), lambda i, e, f, b_: (layer, e, 0, f)),
            pl.BlockSpec((1, 1, d, tf), lambda i, e, f, b_: (layer, e, 0, nf + f)),
            pl.BlockSpec((1, 1, tf, d), lambda i, e, f, b_: (layer, e, f, 0)),
        ],
        out_specs=pl.BlockSpec((tm, d), lambda i, e, f, b_: (i, 0), pipeline_mode=once),
        scratch_shapes=[
            pltpu.VMEM((tm, d), BF16),
            pltpu.VMEM((tm, d), F32),
            pltpu.VMEM((tm, 1), F32),
            pltpu.VMEM((tm, 1), F32),
        ],
    )
    return pl.pallas_call(
        functools.partial(_moe_expert_kernel, blk=blk),
        out_shape=jax.ShapeDtypeStruct((t, d), F32),
        grid_spec=grid_spec,
        compiler_params=_cparams("arbitrary", "arbitrary", "arbitrary"),
        name="moe_experts",
    )(bnd, h, pos, comb, post, w_gu_all, w_gu_all, w_down_all)


def _residual_kernel(x_ref, f_ref, mod_ref, g_ref, o_ref, *, final_norm):
    d = x_ref.shape[1]
    x = x_ref[...] + mod_ref[0][:, 5 * d:6 * d] * f_ref[...]
    if final_norm:
        x = (x * lax.rsqrt(jnp.mean(x * x, axis=-1, keepdims=True) + EPS)) * g_ref[...]
    o_ref[...] = x


def _residual(x, f, mod3, tpb, g_final, tm, final_norm):
    t, d = x.shape
    tok = pl.BlockSpec((tm, d), lambda i: (i, 0))
    return pl.pallas_call(
        functools.partial(_residual_kernel, final_norm=final_norm),
        out_shape=jax.ShapeDtypeStruct((t, d), F32),
        grid=(t // tm,),
        in_specs=[tok, tok, _mod_spec(mod3, tpb), _full((1, d))],
        out_specs=tok,
        compiler_params=_cparams("arbitrary"),
        name="residual",
    )(x, f, mod3, g_final)


def _moe(x, mod3_of, g, w_router, b_router, w_gu_all, w_down_all, layer, g_final, tm, final_norm):
    blk = min(256, tm)
    h, pos, comb, post, bounds = _moe_route(x, *mod3_of(tm), g, w_router, b_router, tm, blk)
    f = _moe_experts(h, pos, comb, post, bounds, w_gu_all, w_down_all, layer, tm, MOE_FF_CHUNK, blk)
    tm_res = min(1024, tm)
    return _residual(x, f, *mod3_of(tm_res), g_final, tm_res, final_norm)


def _final_norm_kernel(x_ref, g_ref, o_ref):
    x = x_ref[...]
    o_ref[...] = (x * lax.rsqrt(jnp.mean(x * x, axis=-1, keepdims=True) + EPS)) * g_ref[...]


def _final_norm(x, g, tm):
    t, d = x.shape
    return pl.pallas_call(
        _final_norm_kernel,
        out_shape=jax.ShapeDtypeStruct((t, d), F32),
        grid=(t // tm,),
        in_specs=[pl.BlockSpec((tm, d), lambda i: (i, 0)), pl.BlockSpec((1, d), lambda i: (0, 0))],
        out_specs=pl.BlockSpec((tm, d), lambda i: (i, 0)),
        compiler_params=_cparams("arbitrary"),
        name="final_norm",
    )(x, g)


def _nsa_layer_weights(w_in, w_out, w_ck, pe_ck, w_cv, pe_cv):
    n_main = N_HEADS * HEAD_DIM + 6 * KV_WIDTH
    w_gate = jnp.pad(w_in[:, n_main:], ((0, 0), (0, V7X_LANES - N_HEADS * N_BRANCH)))
    return {"w_main": w_in[:, :n_main].astype(BF16), "w_gate": w_gate.astype(BF16),
            "w_out": w_out.astype(BF16), "cw": _compress_weights(w_ck, pe_ck, w_cv, pe_cv)}


def _rows_from_feature_major(xt):
    b, _, s = xt.shape
    return jnp.transpose(xt.reshape(b, 2, KV_HEADS, HEAD_DIM, s), (0, 4, 1, 2, 3))


def _feature_major_from_rows(x, lead):
    n = len(lead)
    perm = tuple(range(n)) + (n + 1, n + 2, n + 3, n)
    xt = jnp.transpose(x, perm)
    return xt.reshape(lead + (2 * KV_WIDTH, x.shape[n]))


def _nsa_prompt_layer(x, mod, g, lw, b, s_len):
    tm = min(512, s_len)
    mod3 = mod.reshape(b, 1, mod.shape[-1])
    rope_t = _rope_tables(jnp.arange(s_len), V7X_LANES)
    q, gates, cmp_rows, cmpt, selt, wint = _nsa_proj(x, mod3, s_len // tm, g, lw["w_main"], lw["w_gate"], rope_t,
                                                     s_len, tm, True)
    ck, cv = _compress(cmp_rows.reshape(b, s_len, 2 * KV_WIDTH), lw["cw"])
    o = _attn_prompt(q, gates, ck, cv, selt, wint, tq=min(256, s_len))
    x = _out_proj(o, lw["w_out"], x, mod3, s_len // tm, tm)
    return x, cmpt, selt, wint


def _nsa_decode_layer(x, mod3, g, lw, b, t_seq, caches, layer):
    t = x.shape[0]
    page_table = caches["page_table"]
    past_len = page_table.shape[1] * caches["cmp"].shape[3]
    rope_t = _rope_tables(past_len + jnp.arange(t) % t_seq, V7X_LANES)
    q, gates, cmp_rows, sel_rows, win_rows = _nsa_proj(x, mod3, 1, g, lw["w_main"], lw["w_gate"], rope_t, t, t, False)
    pad3 = lambda a: jnp.pad(a.reshape(b, t_seq, a.shape[-1]), ((0, 0), (0, DECODE_TOK_PAD - t_seq), (0, 0)))
    o = _attn_decode(page_table, pad3(q), pad3(gates), pad3(sel_rows), pad3(win_rows),
                     caches["cmp"], caches["sel"], caches["win"], layer, lw["cw"])
    o = o[:, :t_seq].reshape(t, o.shape[-1]).astype(BF16)
    x = _out_proj(o, lw["w_out"], x, mod3, 1, t)
    return x, cmp_rows, sel_rows, win_rows


def _trunk(x3, mod, caches, p):
    b, t_seq, d = x3.shape
    t = b * t_seq
    kvw = 2 * KV_WIDTH
    prompt = caches is None
    x = x3.reshape(t, d)
    new_cmp, new_sel, new_win, new_conv = [], [], [], []
    depth = mod.shape[0]
    g_final = p["g_norm_final"].reshape(1, d)
    finished = False
    for i in range(depth):
        j = i // 2
        if prompt:
            mod_i = mod[i].reshape(b, 1, 6 * d)
            mod3_of = lambda tm, mod_i=mod_i: (mod_i, t_seq // tm)
            tm_small, tm_big, tm_moe = min(512, t_seq), min(1024, t_seq), min(2048, t_seq)
        else:
            mod_i = jnp.repeat(mod[i], t_seq, axis=0).reshape(1, t, 6 * d)
            mod3_of = lambda tm, mod_i=mod_i: (mod_i, 1)
            tm_small = tm_big = tm_moe = t
        g_mix = p["g_norm_mix"][i].reshape(1, d)
        g_ffn = p["g_norm_ffn"][i].reshape(1, d)
        if i % 2 == 0:
            lw = _nsa_layer_weights(p["w_nsa_in"][j], p["w_nsa_out"][j], p["w_cmp_k"][j], p["pe_cmp_k"][j],
                                    p["w_cmp_v"][j], p["pe_cmp_v"][j])
            if prompt:
                x, cmpt, selt, wint = _nsa_prompt_layer(x, mod[i], g_mix, lw, b, t_seq)
                new_cmp.append(_rows_from_feature_major(cmpt))
                new_sel.append(_rows_from_feature_major(selt))
                new_win.append(_rows_from_feature_major(wint[:, :, t_seq - min(WINDOW, t_seq):]))
            else:
                x, c_rows, s_rows, w_rows = _nsa_decode_layer(x, mod_i, g_mix, lw, b, t_seq, caches, j)
                rows5 = lambda r: r.reshape(b, t_seq, 2, KV_HEADS, HEAD_DIM)
                new_cmp.append(rows5(c_rows))
                new_sel.append(rows5(s_rows))
                new_win.append(jnp.concatenate([caches["win_rows"][j], rows5(w_rows)], axis=1)[:, t_seq:])
            x = _ffn(x, *mod3_of(tm_big), g_ffn, p["w_ff_gu"], p["w_ff_down"], j, tm_big, FFN_FF_CHUNK)
        else:
            state = None if prompt else caches["conv"][j]
            x, st = _conv_mixer(x, *mod3_of(tm_small), g_mix, p["w_conv_in"][j].astype(BF16), p["w_conv"][j],
                                p["w_conv_out"][j].astype(BF16), state, b, t_seq, tm_small)
            new_conv.append(st)
            finished = i == depth - 1
            x = _moe(x, mod3_of, g_ffn, p["w_router"][j], p["b_router"][j], p["w_moe_gu"], p["w_moe_down"], j,
                     g_final, tm_moe, finished)
    y = x if finished else _final_norm(x, g_final, tm_big)
    return (y.reshape(b, t_seq, d), jnp.stack(new_cmp), jnp.stack(new_sel), jnp.stack(new_win), jnp.stack(new_conv))


def kernel(x_prompt, x_sample, cache_cmp_kv, cache_sel_kv, cache_win_kv, state_conv, page_table, c_prompt, c_sample,
           w_ada, b_ada, g_norm_mix, g_norm_ffn, g_norm_final, w_nsa_in, w_nsa_out, w_cmp_k, pe_cmp_k, w_cmp_v,
           pe_cmp_v, w_conv_in, w_conv, w_conv_out, w_ff_gu, w_ff_down, w_router, b_router, w_moe_gu, w_moe_down):
    p = {"g_norm_mix": g_norm_mix, "g_norm_ffn": g_norm_ffn, "g_norm_final": g_norm_final,
         "w_nsa_in": w_nsa_in, "w_nsa_out": w_nsa_out, "w_cmp_k": w_cmp_k, "pe_cmp_k": pe_cmp_k, "w_cmp_v": w_cmp_v,
         "pe_cmp_v": pe_cmp_v, "w_conv_in": w_conv_in, "w_conv": w_conv, "w_conv_out": w_conv_out,
         "w_ff_gu": w_ff_gu.astype(BF16), "w_ff_down": w_ff_down.astype(BF16), "w_router": w_router,
         "b_router": b_router,
         "w_moe_gu": w_moe_gu.astype(BF16), "w_moe_down": w_moe_down.astype(BF16)}
    nb = x_prompt.shape[0]
    mod = _ada(jnp.concatenate([c_prompt, c_sample], axis=0), w_ada, b_ada)
    y_p, cmp_p, sel_p, win_p, conv_p = _trunk(x_prompt, mod[:, :nb], None, p)
    caches = {"cmp": _feature_major_from_rows(cache_cmp_kv, cache_cmp_kv.shape[:2]),
              "sel": _feature_major_from_rows(cache_sel_kv, cache_sel_kv.shape[:2]),
              "win": _feature_major_from_rows(cache_win_kv, cache_win_kv.shape[:2]),
              "win_rows": cache_win_kv, "conv": state_conv, "page_table": page_table}
    y_s, cmp_s, sel_s, win_s, conv_s = _trunk(x_sample, mod[:, nb:], caches, p)
    return (y_p, y_s, cmp_p, cmp_s, sel_p, sel_s, win_p, win_s, conv_p, conv_s)
```
